```python
import jax
import jax.numpy as jnp
from jax import lax
import numpy as np

D_MODEL = 1024
BATCH = 8
SEQ = 4096
DEPTH = 4

HEAD_DIM = 64
FOX_HEADS = 4
SWA_HEADS = 8
SWA_KV_HEADS = 2
SWA_WINDOW = 128
MOBA_HEADS = 4
MOBA_BLOCK = 256
MOBA_TOPK = 3
MOBA_Q_CHUNK = 64
Q_BLOCK = 128
N_BRANCH = 3
ALIBI_HEADS = SWA_HEADS + MOBA_HEADS
N_EXPERTS = 32
TOP_K = 4
D_EXPERT = D_MODEL
SWIGLU_ALPHA = 1.702
SWIGLU_LIMIT = 7.0
EPS = 1e-6
NEG = -1e30
W_FOX = FOX_HEADS * HEAD_DIM
W_SWA = SWA_HEADS * HEAD_DIM
W_SWA_KV = SWA_KV_HEADS * HEAD_DIM
W_MOBA = MOBA_HEADS * HEAD_DIM
IN_SPLITS = (W_FOX, W_FOX, W_FOX, FOX_HEADS, W_SWA, W_SWA_KV, W_SWA_KV, W_MOBA, W_MOBA, W_MOBA, N_BRANCH * D_MODEL)
D_IN = sum(IN_SPLITS)

kernel_name = 'hybrid_fox_swa_moba_moe_block'


def rms_norm(x, gain):
    xf = x.astype(jnp.float32)
    y = xf * lax.rsqrt(jnp.mean(xf * xf, axis=-1, keepdims=True) + EPS)
    return (y * gain.astype(jnp.float32)).astype(x.dtype)


def alibi_slopes(n):
    return jnp.exp2(-8.0 * jnp.arange(1, n + 1, dtype=jnp.float32) / n)


def fox_attention(q, k, v, log_f):
    b, s, h, d = q.shape
    nq = s // Q_BLOCK
    cum = jnp.cumsum(log_f, axis=1).transpose(0, 2, 1)
    q_blocks = q.reshape(b, nq, Q_BLOCK, h, d).transpose(1, 0, 2, 3, 4)
    c_blocks = cum.reshape(b, h, nq, Q_BLOCK).transpose(2, 0, 1, 3)
    k_pos = jnp.arange(s)

    def block(args):
        i, q_i, c_i = args
        logits = jnp.einsum('bqhd,bkhd->bhqk', q_i, k, preferred_element_type=jnp.float32) * HEAD_DIM ** -0.5
        logits = logits + (c_i[..., :, None] - cum[:, :, None, :])
        q_pos = i * Q_BLOCK + jnp.arange(Q_BLOCK)
        logits = jnp.where(k_pos[None, :] <= q_pos[:, None], logits, NEG)
        p = jax.nn.softmax(logits, axis=-1).astype(v.dtype)
        return jnp.einsum('bhqk,bkhd->bqhd', p, v)

    out = lax.map(block, (jnp.arange(nq), q_blocks, c_blocks))
    return out.transpose(1, 0, 2, 3, 4).reshape(b, s, h, d)


def swa_attention(q, k, v, sinks, slopes):
    b, s, hq, d = q.shape
    hkv = k.shape[2]
    g = hq // hkv
    w = SWA_WINDOW
    nb = s // w
    q_b = q.reshape(b, nb, w, hkv, g, d)

    def band(t):
        t_b = t.reshape(b, nb, w, hkv, d)
        prev = jnp.pad(t_b, ((0, 0), (1, 0), (0, 0), (0, 0), (0, 0)))[:, :-1]
        return jnp.concatenate([prev, t_b], axis=2)

    k_b, v_b = band(k), band(v)
    logits = jnp.einsum('bnqhgd,bnkhd->bnhgqk', q_b, k_b, preferred_element_type=jnp.float32) * HEAD_DIM ** -0.5
    dist = (jnp.arange(w)[:, None] + w - jnp.arange(2 * w)[None, :]).astype(jnp.float32)
    in_window = (dist >= 0) & (dist < w)
    has_prev = (jnp.arange(nb) > 0)[:, None, None] | (jnp.arange(2 * w) >= w)[None, None, :]
    valid = in_window[None] & has_prev
    logits = logits - slopes.reshape(hkv, g, 1, 1) * dist
    logits = jnp.where(valid[None, :, None, None], logits, NEG)
    sink = jnp.broadcast_to(sinks.astype(jnp.float32).reshape(hkv, g, 1, 1), logits.shape[:-1] + (1,))
    p = jax.nn.softmax(jnp.concatenate([logits, sink], axis=-1), axis=-1)[..., :-1].astype(v.dtype)
    out = jnp.einsum('bnhgqk,bnkhd->bnqhgd', p, v_b)
    return out.reshape(b, s, hq, d)


def moba_attention(q, k, v, slopes):
    b, s, h, d = q.shape
    n_blk = max(-(-s // MOBA_BLOCK), MOBA_TOPK)
    s_pad = n_blk * MOBA_BLOCK
    pad = ((0, 0), (0, s_pad - s), (0, 0), (0, 0))
    q, k, v = jnp.pad(q, pad), jnp.pad(k, pad), jnp.pad(v, pad)
    k_blk = k.reshape(b, n_blk, MOBA_BLOCK, h, d).transpose(0, 3, 1, 2, 4)
    v_blk = v.reshape(b, n_blk, MOBA_BLOCK, h, d).transpose(0, 3, 1, 2, 4)
    k_mean = jnp.mean(k_blk.astype(jnp.float32), axis=3)
    nc = s_pad // MOBA_Q_CHUNK
    q_chunks = q.reshape(b, nc, MOBA_Q_CHUNK, h, d).transpose(1, 0, 3, 2, 4)
    gather = jax.vmap(jax.vmap(lambda blocks, ix: blocks[ix]))
    offsets = jnp.arange(MOBA_BLOCK)
    n_sel = MOBA_TOPK * MOBA_BLOCK
    scale = HEAD_DIM ** -0.5

    def chunk(args):
        i, q_i = args
        q_pos = i * MOBA_Q_CHUNK + jnp.arange(MOBA_Q_CHUNK)
        own = (i * MOBA_Q_CHUNK) // MOBA_BLOCK
        gate = jnp.einsum('bhqd,bhnd->bhqn', q_i.astype(jnp.float32), k_mean)
        gate = jnp.where(jnp.arange(n_blk) < own, gate, NEG)
        _, idx = lax.top_k(gate, MOBA_TOPK)
        k_sel = gather(k_blk, idx)
        v_sel = gather(v_blk, idx)
        l_sel = jnp.einsum('bhqd,bhqjkd->bhqjk', q_i, k_sel, preferred_element_type=jnp.float32) * scale
        dist_sel = (q_pos[:, None, None] - (idx[..., None] * MOBA_BLOCK + offsets)).astype(jnp.float32)
        l_sel = l_sel - slopes[None, :, None, None, None] * dist_sel
        l_sel = jnp.where((jnp.arange(MOBA_TOPK) < own)[:, None], l_sel, NEG)
        k_own = lax.dynamic_index_in_dim(k_blk, own, axis=2, keepdims=False)
        v_own = lax.dynamic_index_in_dim(v_blk, own, axis=2, keepdims=False)
        l_own = jnp.einsum('bhqd,bhkd->bhqk', q_i, k_own, preferred_element_type=jnp.float32) * scale
        dist_own = (q_pos[:, None] - (own * MOBA_BLOCK + offsets)[None, :]).astype(jnp.float32)
        l_own = jnp.where(dist_own >= 0, l_own - slopes[None, :, None, None] * dist_own, NEG)
        logits = jnp.concatenate([l_sel.reshape(b, h, MOBA_Q_CHUNK, n_sel), l_own], axis=-1)
        p = jax.nn.softmax(logits, axis=-1).astype(v_blk.dtype)
        p_sel = p[..., :n_sel].reshape(b, h, MOBA_Q_CHUNK, MOBA_TOPK, MOBA_BLOCK)
        return (jnp.einsum('bhqjk,bhqjkd->bhqd', p_sel, v_sel)
                + jnp.einsum('bhqk,bhkd->bhqd', p[..., n_sel:], v_own))

    out = lax.map(chunk, (jnp.arange(nc), q_chunks))
    return out.transpose(1, 0, 3, 2, 4).reshape(b, s_pad, h, d)[:, :s]


def hybrid_mixer(h, w_in, b_f, qk_gain, sinks, w_br_fox, w_br_swa, w_br_moba, w_out, slopes_swa, slopes_moba):
    b, s, _ = h.shape
    proj = h @ w_in
    points = np.cumsum(IN_SPLITS)[:-1].tolist()
    qa, ka, va, fa, qb, kb, vb, qc, kc, vc, gates = jnp.split(proj, points, axis=-1)
    heads = lambda t: t.reshape(b, s, -1, HEAD_DIM)
    log_f = jax.nn.log_sigmoid(fa.astype(jnp.float32) + b_f.astype(jnp.float32))
    o_a = fox_attention(rms_norm(heads(qa), qk_gain[0]), rms_norm(heads(ka), qk_gain[1]), heads(va), log_f)
    o_b = swa_attention(rms_norm(heads(qb), qk_gain[2]), rms_norm(heads(kb), qk_gain[3]), heads(vb), sinks, slopes_swa)
    o_c = moba_attention(rms_norm(heads(qc), qk_gain[4]), rms_norm(heads(kc), qk_gain[5]), heads(vc), slopes_moba)
    y_a = o_a.reshape(b, s, W_FOX) @ w_br_fox
    y_b = o_b.reshape(b, s, W_SWA) @ w_br_swa
    y_c = o_c.reshape(b, s, W_MOBA) @ w_br_moba
    g_a, g_b, g_c = jnp.split(jax.nn.sigmoid(gates), N_BRANCH, axis=-1)
    return (g_a * y_a + g_b * y_b + g_c * y_c) @ w_out


def clamped_swiglu(u):
    x_glu, x_lin = u[..., ::2], u[..., 1::2]
    x_glu = jnp.minimum(x_glu, SWIGLU_LIMIT)
    x_lin = jnp.clip(x_lin, -SWIGLU_LIMIT, SWIGLU_LIMIT)
    return x_glu * jax.nn.sigmoid(SWIGLU_ALPHA * x_glu) * (x_lin + 1.0)


def moe_ffn(h, w_router, b_router, w1, b1, w2, b2):
    b, s, d = h.shape
    t = h.reshape(b * s, d)
    logits = (t @ w_router + b_router).astype(jnp.float32)
    top_vals, top_idx = lax.top_k(logits, TOP_K)
    top_w = jax.nn.softmax(top_vals, axis=-1)
    gates = jnp.sum(jax.nn.one_hot(top_idx, N_EXPERTS, dtype=jnp.float32) * top_w[..., None], axis=1).astype(t.dtype)
    out = jnp.zeros_like(t)
    for e in range(N_EXPERTS):
        y = clamped_swiglu(t @ w1[e] + b1[e]) @ w2[e] + b2[e]
        out = out + gates[:, e:e + 1] * y
    return out.reshape(b, s, d)


def setup_inputs(seed: int = 0) -> dict:
    key = jax.random.key(seed)
    ks = jax.random.split(key, 20)
    f32 = jnp.float32
    nrm = lambda k, shape, sc: jax.random.normal(k, shape, f32) * sc
    return {
        'x': nrm(ks[0], (BATCH, SEQ, D_MODEL), 1.0),
        'c': nrm(ks[1], (BATCH, D_MODEL), 1.0),
        'w_ada': nrm(ks[2], (DEPTH, D_MODEL, 6 * D_MODEL), 0.5 * D_MODEL ** -0.5),
        'b_ada': nrm(ks[3], (DEPTH, 6 * D_MODEL), 0.02),
        'norm_gain': 1.0 + nrm(ks[4], (DEPTH, 2, D_MODEL), 0.02),
        'w_in': nrm(ks[5], (DEPTH, D_MODEL, D_IN), D_MODEL ** -0.5),
        'b_fgate': 2.0 + nrm(ks[6], (DEPTH, FOX_HEADS), 0.1),
        'qk_gain': 1.0 + nrm(ks[7], (DEPTH, 6, HEAD_DIM), 0.02),
        'attn_sinks': nrm(ks[8], (DEPTH, SWA_HEADS), 0.5),
        'w_br_fox': nrm(ks[9], (DEPTH, W_FOX, D_MODEL), W_FOX ** -0.5),
        'w_br_swa': nrm(ks[10], (DEPTH, W_SWA, D_MODEL), W_SWA ** -0.5),
        'w_br_moba': nrm(ks[11], (DEPTH, W_MOBA, D_MODEL), W_MOBA ** -0.5),
        'w_out': nrm(ks[12], (DEPTH, D_MODEL, D_MODEL), D_MODEL ** -0.5),
        'w_router': nrm(ks[13], (DEPTH, D_MODEL, N_EXPERTS), D_MODEL ** -0.5),
        'b_router': nrm(ks[14], (DEPTH, N_EXPERTS), 0.01),
        'w_exp1': nrm(ks[15], (DEPTH, N_EXPERTS, D_MODEL, 2 * D_EXPERT), D_MODEL ** -0.5),
        'b_exp1': nrm(ks[16], (DEPTH, N_EXPERTS, 2 * D_EXPERT), 0.01),
        'w_exp2': nrm(ks[17], (DEPTH, N_EXPERTS, D_EXPERT, D_MODEL), D_EXPERT ** -0.5),
        'b_exp2': nrm(ks[18], (DEPTH, N_EXPERTS, D_MODEL), 0.01),
    }


def reference(x, c, w_ada, b_ada, norm_gain, w_in, b_fgate, qk_gain, attn_sinks, w_br_fox, w_br_swa, w_br_moba, w_out, w_router, b_router, w_exp1, b_exp1, w_exp2, b_exp2):
    slopes = alibi_slopes(ALIBI_HEADS)
    slopes_swa, slopes_moba = slopes[:SWA_HEADS], slopes[SWA_HEADS:]
    cond = jax.nn.silu(c)
    for l in range(DEPTH):
        mod = cond @ w_ada[l] + b_ada[l]
        sh1, sc1, g1, sh2, sc2, g2 = jnp.split(mod[:, None, :], 6, axis=-1)
        h = rms_norm(x, norm_gain[l, 0]) * (1.0 + sc1) + sh1
        x = x + g1 * hybrid_mixer(h, w_in[l], b_fgate[l], qk_gain[l], attn_sinks[l], w_br_fox[l], w_br_swa[l], w_br_moba[l], w_out[l], slopes_swa, slopes_moba)
        h = rms_norm(x, norm_gain[l, 1]) * (1.0 + sc2) + sh2
        x = x + g2 * moe_ffn(h, w_router[l], b_router[l], w_exp1[l], b_exp1[l], w_exp2[l], b_exp2[l])
    return x
```

```python
import functools

import numpy as np
import jax
import jax.numpy as jnp
from jax import lax
from jax.experimental import pallas as pl
from jax.experimental.pallas import tpu as pltpu

HEAD_DIM = 64
FOX_HEADS = 4
SWA_HEADS = 8
SWA_KV_HEADS = 2
SWA_WINDOW = 128
MOBA_HEADS = 4
MOBA_BLOCK = 256
MOBA_TOPK = 3
TOP_K = 4
SWIGLU_ALPHA = 1.702
SWIGLU_LIMIT = 7.0
EPS = 1e-6
NEG = -1e30
MOBA_MASK_BIAS = -float(2 ** 30)

LANES = 128
MXU_DIM = 256
VMEM_LIMIT = 60 * 1024 * 1024

W_FOX = FOX_HEADS * HEAD_DIM
W_SWA = SWA_HEADS * HEAD_DIM
W_SWA_KV = SWA_KV_HEADS * HEAD_DIM
W_MOBA = MOBA_HEADS * HEAD_DIM
W_MOBA_AUG = MOBA_HEADS * LANES
SWA_HEAD_ORDER = (0, 4, 1, 5, 2, 6, 3, 7)

F32 = jnp.float32
BF16 = jnp.bfloat16
HIGHEST = lax.Precision.HIGHEST
NT_DIMS = (((1,), (1,)), ((), ()))


def _params(semantics):
    return pltpu.CompilerParams(dimension_semantics=semantics, vmem_limit_bytes=VMEM_LIMIT)


def _iota(shape, dim, dtype=jnp.int32):
    return lax.broadcasted_iota(dtype, shape, dim)


def _mod_kernel(c_ref, w_ref, b_ref, o_ref):
    c = c_ref[...]
    s = c * jax.nn.sigmoid(c)
    o_ref[0] = jnp.dot(s, w_ref[0], precision=HIGHEST, preferred_element_type=F32) + b_ref[0]


def _modulation(c, w_ada, b_ada):
    depth, d, n = w_ada.shape
    b = c.shape[0]
    tn = d
    return pl.pallas_call(
        _mod_kernel,
        grid=(depth, n // tn),
        in_specs=[
            pl.BlockSpec((b, d), lambda l, j: (0, 0)),
            pl.BlockSpec((1, d, tn), lambda l, j: (l, 0, j)),
            pl.BlockSpec((1, 1, tn), lambda l, j: (l, 0, j)),
        ],
        out_specs=pl.BlockSpec((1, b, tn), lambda l, j: (l, 0, j)),
        out_shape=jax.ShapeDtypeStruct((depth, b, n), F32),
        compiler_params=_params(("parallel", "parallel")),
    )(c, w_ada, b_ada.reshape(depth, 1, n))


def _head_norm(z, g_ref, gain):
    w = z.shape[1]
    cw = min(w, MXU_DIM)
    outs = []
    for c0 in range(0, w, cw):
        zz = z[:, c0:c0 + cw]
        ss = jnp.dot((zz * zz).astype(BF16), g_ref[:cw, :cw], preferred_element_type=F32)
        outs.append(zz * lax.rsqrt(ss * (1.0 / HEAD_DIM) + EPS))
    zn = outs[0] if len(outs) == 1 else jnp.concatenate(outs, axis=1)
    return zn * gain


def _in_kernel(*refs, has_prev, tm, d):
    if has_prev:
        x_ref, moe_ref, g2_ref = refs[:3]
        refs = refs[3:]
    else:
        x_ref = refs[0]
        refs = refs[1:]
    (sh_ref, sc_ref, gain_ref, w_ref, wf_ref, bf_ref, gvec_ref, g_ref, tri_ref,
     qf_ref, kf_ref, vf_ref, qs_ref, ks_ref, vs_ref, qm_ref, km_ref, vm_ref,
     kmean_ref, gates_ref, cum_ref) = refs[:21]
    refs = refs[21:]
    if has_prev:
        xn_ref, carry_ref = refs
    else:
        (carry_ref,) = refs
    j = pl.program_id(1)

    x = x_ref[0]
    if has_prev:
        x = x + g2_ref[0] * moe_ref[0]
        xn_ref[0] = x
    ms = jnp.mean(x * x, axis=-1, keepdims=True)
    h = x * lax.rsqrt(ms + EPS) * gain_ref[...]
    h = h * (1.0 + sc_ref[0]) + sh_ref[0]
    hb = h.astype(BF16)

    def proj(c0, c1):
        return jnp.dot(hb, w_ref[:, c0:c1], preferred_element_type=F32)

    def gain(c0, c1):
        return gvec_ref[:, c0:c1]

    o = 0
    z = proj(o, o + 3 * W_FOX)
    qf_ref[0] = _head_norm(z[:, :W_FOX], g_ref, gain(o, o + W_FOX)).astype(BF16)
    kf_ref[0] = _head_norm(z[:, W_FOX:2 * W_FOX], g_ref, gain(o + W_FOX, o + 2 * W_FOX)).astype(BF16)
    vf_ref[0] = z[:, 2 * W_FOX:].astype(BF16)
    o += 3 * W_FOX
    z = proj(o, o + W_SWA + 2 * W_SWA_KV)
    qs_ref[0] = _head_norm(z[:, :W_SWA], g_ref, gain(o, o + W_SWA)).astype(BF16)
    ks_ref[0] = _head_norm(z[:, W_SWA:W_SWA + W_SWA_KV], g_ref,
                           gain(o + W_SWA, o + W_SWA + W_SWA_KV)).astype(BF16)
    vs_ref[0] = z[:, W_SWA + W_SWA_KV:].astype(BF16)
    o += W_SWA + 2 * W_SWA_KV
    z = proj(o, o + 2 * W_MOBA_AUG + W_MOBA)
    qm_ref[0] = _head_norm(z[:, :W_MOBA_AUG], g_ref, gain(o, o + W_MOBA_AUG))
    kn = _head_norm(z[:, W_MOBA_AUG:2 * W_MOBA_AUG], g_ref, gain(o + W_MOBA_AUG, o + 2 * W_MOBA_AUG))
    nb = tm // MOBA_BLOCK
    kmean_ref[0] = jnp.mean(kn.reshape(nb, MOBA_BLOCK, W_MOBA_AUG), axis=1).reshape(nb, 1, W_MOBA_AUG)
    lane = _iota((tm, W_MOBA_AUG), 1) % LANES
    blk = (j * tm + _iota((tm, W_MOBA_AUG), 0)) // MOBA_BLOCK
    km_ref[0] = jnp.where(lane == HEAD_DIM + blk, 1.0, kn).astype(BF16)
    vm_ref[0] = z[:, 2 * W_MOBA_AUG:].astype(BF16)
    o += 2 * W_MOBA_AUG + W_MOBA
    for br in range(3):
        zg = proj(o + br * d, o + (br + 1) * d)
        gates_ref[0, :, br * d:(br + 1) * d] = jax.nn.sigmoid(zg).astype(BF16)

    @pl.when(j == 0)
    def _():
        carry_ref[...] = jnp.zeros_like(carry_ref)

    fa = lax.dot_general(wf_ref[...], hb, NT_DIMS, preferred_element_type=F32) + bf_ref[...]
    logf = -(jnp.maximum(-fa, 0.0) + jnp.log(1.0 + jnp.exp(-jnp.abs(fa))))
    cs = jnp.dot(logf, tri_ref[...], precision=HIGHEST, preferred_element_type=F32) + carry_ref[:, 0:1]
    cum_ref[0] = cs
    carry_ref[...] = jnp.broadcast_to(cs[:, tm - 1:tm], carry_ref.shape)


def _in_proj(x, prev, sh, sc, gain, wp, tm):
    b, s, d = x.shape
    has_prev = prev is not None
    nw = wp["w_cat"].shape[1]
    tok = lambda bi, j: (bi, j, 0)
    row = lambda bi, j: (bi, 0, 0)
    const2 = lambda bi, j: (0, 0)
    in_specs = [pl.BlockSpec((1, tm, d), tok)]
    args = [x]
    if has_prev:
        in_specs += [pl.BlockSpec((1, tm, d), tok), pl.BlockSpec((1, 1, d), row)]
        args += [prev[0], prev[1]]
    in_specs += [
        pl.BlockSpec((1, 1, d), row), pl.BlockSpec((1, 1, d), row), pl.BlockSpec((1, d), const2),
        pl.BlockSpec((d, nw), const2), pl.BlockSpec((8, d), const2), pl.BlockSpec((8, 1), const2),
        pl.BlockSpec((1, wp["gvec"].shape[1]), const2), pl.BlockSpec((MXU_DIM, MXU_DIM), const2),
        pl.BlockSpec((tm, tm), const2),
    ]
    args += [sh, sc, gain, wp["w_cat"], wp["w_f"], wp["b_f"], wp["gvec"], wp["gsum"], wp["tri"]]
    nblk = s // MOBA_BLOCK
    out_shape = [
        jax.ShapeDtypeStruct((b, s, W_FOX), BF16), jax.ShapeDtypeStruct((b, s, W_FOX), BF16),
        jax.ShapeDtypeStruct((b, s, W_FOX), BF16),
        jax.ShapeDtypeStruct((b, s, W_SWA), BF16), jax.ShapeDtypeStruct((b, s, W_SWA_KV), BF16),
        jax.ShapeDtypeStruct((b, s, W_SWA_KV), BF16),
        jax.ShapeDtypeStruct((b, s, W_MOBA_AUG), F32), jax.ShapeDtypeStruct((b, s, W_MOBA_AUG), BF16),
        jax.ShapeDtypeStruct((b, s, W_MOBA), BF16),
        jax.ShapeDtypeStruct((b, nblk, 1, W_MOBA_AUG), F32),
        jax.ShapeDtypeStruct((b, s, 3 * d), BF16),
        jax.ShapeDtypeStruct((b, 8, s), F32),
    ]
    widths = [W_FOX, W_FOX, W_FOX, W_SWA, W_SWA_KV, W_SWA_KV, W_MOBA_AUG, W_MOBA_AUG, W_MOBA]
    out_specs = [pl.BlockSpec((1, tm, w), tok) for w in widths]
    out_specs += [
        pl.BlockSpec((1, tm // MOBA_BLOCK, 1, W_MOBA_AUG), lambda bi, j: (bi, j, 0, 0)),
        pl.BlockSpec((1, tm, 3 * d), tok),
        pl.BlockSpec((1, 8, tm), lambda bi, j: (bi, 0, j)),
    ]
    if has_prev:
        out_shape.append(jax.ShapeDtypeStruct((b, s, d), F32))
        out_specs.append(pl.BlockSpec((1, tm, d), tok))
    return pl.pallas_call(
        functools.partial(_in_kernel, has_prev=has_prev, tm=tm, d=d),
        grid=(b, s // tm),
        in_specs=in_specs,
        out_specs=out_specs,
        out_shape=out_shape,
        scratch_shapes=[pltpu.VMEM((8, LANES), F32)],
        compiler_params=_params(("parallel", "arbitrary")),
    )(*args)


def _fox_kernel(q_ref, k_ref, v_ref, crow_ref, ccol_ref, o_ref, m_ref, l_ref, acc_ref, *, tq):
    i = pl.program_id(1)
    q = q_ref[0]
    lane_head = _iota((1, W_FOX), 1) // HEAD_DIM
    qh = [jnp.where(lane_head == h, q, jnp.zeros_like(q)) for h in range(FOX_HEADS)]
    ccol = ccol_ref[0]
    m_ref[...] = jnp.full_like(m_ref, NEG)
    l_ref[...] = jnp.zeros_like(l_ref)
    acc_ref[...] = jnp.zeros_like(acc_ref)
    causal = _iota((tq, tq), 1) <= _iota((tq, tq), 0)

    def block(j, diagonal):
        start = pl.multiple_of(j * tq, tq)
        ks = k_ref[0, pl.ds(start, tq), :]
        vs = v_ref[0, pl.ds(start, tq), :]
        scale = jnp.zeros((tq, W_FOX), F32)
        add = jnp.zeros((tq, W_FOX), F32)
        for h in range(FOX_HEADS):
            s = lax.dot_general(qh[h], ks, NT_DIMS, preferred_element_type=F32)
            s = s + (ccol[:, h:h + 1] - crow_ref[0, h:h + 1, pl.ds(start, tq)])
            if diagonal:
                s = jnp.where(causal, s, NEG)
            m_old = m_ref[h]
            m_new = jnp.maximum(m_old, jnp.max(s, axis=-1, keepdims=True))
            alpha = jnp.exp(m_old - m_new)
            p = jnp.exp(s - m_new)
            l_ref[h] = alpha * l_ref[h] + jnp.sum(p, axis=-1, keepdims=True)
            m_ref[h] = m_new
            pv = jnp.dot(p.astype(BF16), vs, preferred_element_type=F32)
            sel = lane_head == h
            scale = jnp.where(sel, alpha, scale)
            add = jnp.where(sel, pv, add)
        acc_ref[...] = acc_ref[...] * scale + add

    def body(j, carry):
        block(j, False)
        return carry

    lax.fori_loop(0, i, body, 0)
    block(i, True)
    inv = jnp.zeros((tq, W_FOX), F32)
    for h in range(FOX_HEADS):
        inv = jnp.where(lane_head == h, 1.0 / l_ref[h], inv)
    o_ref[0] = (acc_ref[...] * inv).astype(BF16)


def _fox_attention(q, k, v, cum_row, cum_col, tq):
    b, s, w = q.shape
    return pl.pallas_call(
        functools.partial(_fox_kernel, tq=tq),
        grid=(b, s // tq),
        in_specs=[
            pl.BlockSpec((1, tq, w), lambda bi, i: (bi, i, 0)),
            pl.BlockSpec((1, s, w), lambda bi, i: (bi, 0, 0)),
            pl.BlockSpec((1, s, w), lambda bi, i: (bi, 0, 0)),
            pl.BlockSpec((1, 8, s), lambda bi, i: (bi, 0, 0)),
            pl.BlockSpec((1, tq, 8), lambda bi, i: (bi, i, 0)),
        ],
        out_specs=pl.BlockSpec((1, tq, w), lambda bi, i: (bi, i, 0)),
        out_shape=jax.ShapeDtypeStruct((b, s, w), BF16),
        scratch_shapes=[pltpu.VMEM((FOX_HEADS, tq, 1), F32), pltpu.VMEM((FOX_HEADS, tq, 1), F32),
                        pltpu.VMEM((tq, w), F32)],
        compiler_params=_params(("parallel", "parallel")),
    )(q, k, v, cum_row, cum_col)


def _swa_kernel(sinks_ref, slopes_ref, q_ref, k_ref, v_ref, o_ref, *, tq):
    i = pl.program_id(1)
    tk = tq + SWA_WINDOW
    start = pl.multiple_of(jnp.maximum(i * tq - SWA_WINDOW, 0), SWA_WINDOW)
    kw = k_ref[0, pl.ds(start, tk), :]
    vw = v_ref[0, pl.ds(start, tk), :]
    dist = ((i * tq + _iota((tq, tk), 0)) - (start + _iota((tq, tk), 1))).astype(F32)
    valid = (dist >= 0.0) & (dist < float(SWA_WINDOW))
    low_lanes = _iota((1, LANES), 1) < HEAD_DIM
    pairs = SWA_HEADS // SWA_KV_HEADS
    for p in range(pairs):
        qg = q_ref[0, :, p * LANES:(p + 1) * LANES]
        outs = []
        for half in range(SWA_KV_HEADS):
            head = p + pairs * half
            mine = low_lanes if half == 0 else jnp.logical_not(low_lanes)
            qm = jnp.where(mine, qg, jnp.zeros_like(qg))
            s = lax.dot_general(qm, kw, NT_DIMS, preferred_element_type=F32)
            s = jnp.where(valid, s - slopes_ref[head] * dist, NEG)
            sink = sinks_ref[head]
            m = jnp.maximum(jnp.max(s, axis=-1, keepdims=True), sink)
            pr = jnp.exp(s - m)
            den = jnp.sum(pr, axis=-1, keepdims=True) + jnp.exp(sink - m)
            outs.append(jnp.dot(pr.astype(BF16), vw, preferred_element_type=F32) / den)
        o_ref[0, :, p * LANES:(p + 1) * LANES] = jnp.where(low_lanes, outs[0], outs[1]).astype(BF16)


def _swa_attention(q, k, v, sinks, slopes, tq):
    b, s, w = q.shape
    wk = k.shape[2]
    smem = pl.BlockSpec(memory_space=pltpu.SMEM)
    return pl.pallas_call(
        functools.partial(_swa_kernel, tq=tq),
        grid=(b, s // tq),
        in_specs=[
            smem, smem,
            pl.BlockSpec((1, tq, w), lambda bi, i: (bi, i, 0)),
            pl.BlockSpec((1, s, wk), lambda bi, i: (bi, 0, 0)),
            pl.BlockSpec((1, s, wk), lambda bi, i: (bi, 0, 0)),
        ],
        out_specs=pl.BlockSpec((1, tq, w), lambda bi, i: (bi, i, 0)),
        out_shape=jax.ShapeDtypeStruct((b, s, w), BF16),
        compiler_params=_params(("parallel", "parallel")),
    )(sinks, slopes, q, k, v)


def _moba_kernel(slopes_ref, q_ref, k_ref, v_ref, kmean_ref, o_ref, m_ref, l_ref, acc_ref):
    tq = MOBA_BLOCK
    i = pl.program_id(1)
    lane = _iota((tq, LANES), 1)
    lanef = lane.astype(F32)
    past = (lane >= HEAD_DIM) & (lane < HEAD_DIM + i)
    qa = []
    for h in range(MOBA_HEADS):
        qh = q_ref[0, :, h * LANES:(h + 1) * LANES]
        gate = jnp.dot(qh, kmean_ref[0, h], precision=HIGHEST, preferred_element_type=F32)
        g = jnp.where(past, gate, NEG)
        chosen = jnp.zeros((tq, LANES), jnp.bool_)
        for _ in range(MOBA_TOPK):
            mx = jnp.max(g, axis=-1, keepdims=True)
            first = jnp.min(jnp.where(g == mx, lanef, float(LANES)), axis=-1, keepdims=True)
            pick = (lanef == first) & (mx > 0.5 * NEG)
            chosen = chosen | pick
            g = jnp.where(pick, NEG, g)
        bias = jnp.where(past & jnp.logical_not(chosen), MOBA_MASK_BIAS, 0.0)
        qa.append((qh + bias).astype(BF16))
    m_ref[...] = jnp.full_like(m_ref, NEG)
    l_ref[...] = jnp.zeros_like(l_ref)
    acc_ref[...] = jnp.zeros_like(acc_ref)
    lane_head = _iota((1, W_MOBA), 1) // HEAD_DIM
    rel = (_iota((tq, tq), 0) - _iota((tq, tq), 1)).astype(F32)

    def block(j, diagonal):
        start = pl.multiple_of(j * tq, tq)
        vs = v_ref[0, pl.ds(start, tq), :]
        dist = rel + ((i - j) * tq).astype(F32)
        scale = jnp.zeros((tq, W_MOBA), F32)
        add = jnp.zeros((tq, W_MOBA), F32)
        for h in range(MOBA_HEADS):
            ks = k_ref[0, pl.ds(start, tq), h * LANES:(h + 1) * LANES]
            s = lax.dot_general(qa[h], ks, NT_DIMS, preferred_element_type=F32)
            s = s - slopes_ref[SWA_HEADS + h] * dist
            if diagonal:
                s = jnp.where(rel >= 0.0, s, NEG)
            m_old = m_ref[h]
            m_new = jnp.maximum(m_old, jnp.max(s, axis=-1, keepdims=True))
            alpha = jnp.exp(m_old - m_new)
            p = jnp.exp(s - m_new)
            l_ref[h] = alpha * l_ref[h] + jnp.sum(p, axis=-1, keepdims=True)
            m_ref[h] = m_new
            pv = jnp.dot(p.astype(BF16), vs, preferred_element_type=F32)
            sel = lane_head == h
            scale = jnp.where(sel, alpha, scale)
            add = jnp.where(sel, pv, add)
        acc_ref[...] = acc_ref[...] * scale + add

    def body(j, carry):
        block(j, False)
        return carry

    lax.fori_loop(0, i, body, 0)
    block(i, True)
    inv = jnp.zeros((tq, W_MOBA), F32)
    for h in range(MOBA_HEADS):
        inv = jnp.where(lane_head == h, 1.0 / l_ref[h], inv)
    o_ref[0] = (acc_ref[...] * inv).astype(BF16)


def _moba_attention(q, k, v, kmean_mat, slopes):
    b, s, wa = q.shape
    w = v.shape[2]
    tq = MOBA_BLOCK
    return pl.pallas_call(
        _moba_kernel,
        grid=(b, s // tq),
        in_specs=[
            pl.BlockSpec(memory_space=pltpu.SMEM),
            pl.BlockSpec((1, tq, wa), lambda bi, i: (bi, i, 0)),
            pl.BlockSpec((1, s, wa), lambda bi, i: (bi, 0, 0)),
            pl.BlockSpec((1, s, w), lambda bi, i: (bi, 0, 0)),
            pl.BlockSpec((1, MOBA_HEADS, LANES, LANES), lambda bi, i: (bi, 0, 0, 0)),
        ],
        out_specs=pl.BlockSpec((1, tq, w), lambda bi, i: (bi, i, 0)),
        out_shape=jax.ShapeDtypeStruct((b, s, w), BF16),
        scratch_shapes=[pltpu.VMEM((MOBA_HEADS, tq, 1), F32), pltpu.VMEM((MOBA_HEADS, tq, 1), F32),
                        pltpu.VMEM((tq, w), F32)],
        compiler_params=_params(("parallel", "parallel")),
    )(slopes, q, k, v, kmean_mat)


def _out_kernel(oa_ref, ob_ref, oc_ref, gates_ref, x_ref, g1_ref, sh_ref, sc_ref, gain_ref,
                wa_ref, wb_ref, wc_ref, wo_ref, wr_ref, br_ref,
                x1_ref, hx_ref, route_ref, *, d):
    ya = jnp.dot(oa_ref[0], wa_ref[...], preferred_element_type=F32)
    yb = jnp.dot(ob_ref[0], wb_ref[...], preferred_element_type=F32)
    yc = jnp.dot(oc_ref[0], wc_ref[...], preferred_element_type=F32)
    g = gates_ref[0]
    mix = (g[:, :d].astype(F32) * ya + g[:, d:2 * d].astype(F32) * yb + g[:, 2 * d:].astype(F32) * yc)
    y = jnp.dot(mix.astype(BF16), wo_ref[...], preferred_element_type=F32)
    x1 = x_ref[0] + g1_ref[0] * y
    x1_ref[0] = x1
    ms = jnp.mean(x1 * x1, axis=-1, keepdims=True)
    h = x1 * lax.rsqrt(ms + EPS) * gain_ref[...]
    h = h * (1.0 + sc_ref[0]) + sh_ref[0]
    hx_ref[0, :, :d] = h
    logits = jnp.dot(h, wr_ref[...], precision=HIGHEST, preferred_element_type=F32) + br_ref[...]
    tm = logits.shape[0]
    lanef = _iota((tm, LANES), 1).astype(F32)
    vals, idxs = [], []
    for _ in range(TOP_K):
        mx = jnp.max(logits, axis=-1, keepdims=True)
        first = jnp.min(jnp.where(logits == mx, lanef, float(LANES)), axis=-1, keepdims=True)
        vals.append(mx)
        idxs.append(first)
        logits = jnp.where(lanef == first, -jnp.inf, logits)
    exps = [jnp.exp(v - vals[0]) for v in vals]
    den = exps[0]
    for e in exps[1:]:
        den = den + e
    gates = jnp.zeros((tm, LANES), F32)
    info = jnp.zeros((tm, LANES), F32)
    for k in range(TOP_K):
        wk = exps[k] / den
        gates = jnp.where(lanef == idxs[k], wk, gates)
        info = jnp.where(lanef == float(k), idxs[k], info)
    hx_ref[0, :, d:] = gates
    route_ref[0] = info


def _out_proj(oa, ob, oc, gates, x, g1, sh, sc, gain, wp, tm):
    b, s, d = x.shape
    tok = lambda bi, j: (bi, j, 0)
    row = lambda bi, j: (bi, 0, 0)
    const2 = lambda bi, j: (0, 0)
    return pl.pallas_call(
        functools.partial(_out_kernel, d=d),
        grid=(b, s // tm),
        in_specs=[
            pl.BlockSpec((1, tm, W_FOX), tok), pl.BlockSpec((1, tm, W_SWA), tok),
            pl.BlockSpec((1, tm, W_MOBA), tok), pl.BlockSpec((1, tm, 3 * d), tok),
            pl.BlockSpec((1, tm, d), tok),
            pl.BlockSpec((1, 1, d), row), pl.BlockSpec((1, 1, d), row), pl.BlockSpec((1, 1, d), row),
            pl.BlockSpec((1, d), const2),
            pl.BlockSpec((W_FOX, d), const2), pl.BlockSpec((W_SWA, d), const2),
            pl.BlockSpec((W_MOBA, d), const2), pl.BlockSpec((d, d), const2),
            pl.BlockSpec((d, LANES), const2), pl.BlockSpec((1, LANES), const2),
        ],
        out_specs=[pl.BlockSpec((1, tm, d), tok), pl.BlockSpec((1, tm, d + LANES), tok),
                   pl.BlockSpec((1, tm, LANES), tok)],
        out_shape=[jax.ShapeDtypeStruct((b, s, d), F32), jax.ShapeDtypeStruct((b, s, d + LANES), F32),
                   jax.ShapeDtypeStruct((b, s, LANES), F32)],
        compiler_params=_params(("parallel", "parallel")),
    )(oa, ob, oc, gates, x, g1, sh, sc, gain, wp["w_a"], wp["w_b"], wp["w_c"], wp["w_o"],
      wp["w_r"], wp["b_r"])


def _moe_kernel(tile_e_ref, tile_p0_ref, tile_n_ref, tok_ref,
                hx_ref, w1g_ref, w1l_ref, b1g_ref, b1l_ref, w2_ref, b2_ref,
                out_ref, xg_ref, y_ref, *, nt, tm, d):
    c = pl.program_id(0)
    i = pl.program_id(1)
    t = c * nt + i
    e = tile_e_ref[t]
    p0 = tile_p0_ref[t]
    n = tile_n_ref[t]
    last = tok_ref.shape[0] - 1

    @pl.when(i == 0)
    def _():
        out_ref[...] = jnp.zeros_like(out_ref)

    @pl.when(n > 0)
    def _():
        def gather(r, carry):
            tk = tok_ref[jnp.minimum(p0 + r, last)]
            xg_ref[pl.ds(r, 1), :] = hx_ref[0, pl.ds(tk, 1), :]
            return carry

        lax.fori_loop(0, tm, gather, 0, unroll=8)
        xg = xg_ref[...]
        xb = xg[:, :d].astype(BF16)
        lane = _iota((tm, LANES), 1)
        wcol = jnp.sum(jnp.where(lane == e, xg[:, d:], 0.0), axis=-1, keepdims=True)
        ug = jnp.dot(xb, w1g_ref[0], preferred_element_type=F32) + b1g_ref[0]
        ul = jnp.dot(xb, w1l_ref[0], preferred_element_type=F32) + b1l_ref[0]
        ug = jnp.minimum(ug, SWIGLU_LIMIT)
        ul = jnp.clip(ul, -SWIGLU_LIMIT, SWIGLU_LIMIT)
        act = ug * jax.nn.sigmoid(SWIGLU_ALPHA * ug) * (ul + 1.0)
        y = jnp.dot(act.astype(BF16), w2_ref[0], preferred_element_type=F32) + b2_ref[0]
        y_ref[...] = y * wcol

        def scatter(r, carry):
            tk = tok_ref[p0 + r]
            out_ref[0, pl.ds(tk, 1), :] = out_ref[0, pl.ds(tk, 1), :] + y_ref[pl.ds(r, 1), :]
            return carry

        lax.fori_loop(0, n, scatter, 0)


def _moe(hx, tables, wp, tc, tm):
    b, s, dx = hx.shape
    d = dx - LANES
    nc = (b * s) // tc
    tile_e, tile_p0, tile_n, tok = tables
    nt = tile_e.shape[0] // nc
    f = wp["w1g"].shape[2]
    hx = hx.reshape(nc, tc, dx)
    chunk = lambda c, i, te, tp, tn, tk: (c, 0, 0)
    expert = lambda c, i, te, tp, tn, tk: (te[c * nt + i], 0, 0)
    out = pl.pallas_call(
        functools.partial(_moe_kernel, nt=nt, tm=tm, d=d),
        grid_spec=pltpu.PrefetchScalarGridSpec(
            num_scalar_prefetch=4,
            grid=(nc, nt),
            in_specs=[
                pl.BlockSpec((1, tc, dx), chunk),
                pl.BlockSpec((1, d, f), expert), pl.BlockSpec((1, d, f), expert),
                pl.BlockSpec((1, 1, f), expert), pl.BlockSpec((1, 1, f), expert),
                pl.BlockSpec((1, f, d), expert), pl.BlockSpec((1, 1, d), expert),
            ],
            out_specs=pl.BlockSpec((1, tc, d), chunk),
            scratch_shapes=[pltpu.VMEM((tm, dx), F32), pltpu.VMEM((tm, d), F32)],
        ),
        out_shape=jax.ShapeDtypeStruct((nc, tc, d), F32),
        compiler_params=_params(("parallel", "arbitrary")),
    )(tile_e, tile_p0, tile_n, tok, hx, wp["w1g"], wp["w1l"], wp["b1g"], wp["b1l"], wp["w2"], wp["b2"])
    return out.reshape(b, s, d)


def _routing_tables(route, n_experts, tc, tm):
    b, s, _ = route.shape
    t = b * s
    nc = t // tc
    idx = route[:, :, :TOP_K].astype(jnp.int32).reshape(nc, tc, TOP_K)
    key = idx * tc + jnp.arange(tc, dtype=jnp.int32)[None, :, None]
    key = jnp.sort(key.reshape(nc, tc * TOP_K), axis=1)
    tok = key % tc
    e_sorted = key // tc
    experts = jnp.arange(n_experts, dtype=jnp.int32)
    cstart = jnp.sum(e_sorted[:, :, None] < experts[None, None, :], axis=1).astype(jnp.int32)
    cend = jnp.concatenate([cstart[:, 1:], jnp.full((nc, 1), tc * TOP_K, jnp.int32)], axis=1)
    count = cend - cstart
    tiles = (count + tm - 1) // tm
    tend = jnp.cumsum(tiles, axis=1)
    tstart = tend - tiles
    nt = (tc * TOP_K) // tm + n_experts
    slot = jnp.arange(nt, dtype=jnp.int32)
    total = tend[:, -1:]
    live = slot[None, :] < total
    slot_c = jnp.minimum(slot[None, :], total - 1)
    e_of = jnp.sum(slot_c[:, :, None] >= tend[:, None, :], axis=2).astype(jnp.int32)
    pick = lambda tbl: jnp.sum(jnp.where(e_of[:, :, None] == experts[None, None, :], tbl[:, None, :], 0), axis=2)
    jj = slot_c - pick(tstart)
    p0 = pick(cstart) + jj * tm
    n = jnp.where(live, jnp.clip(pick(count) - jj * tm, 0, tm), 0)
    p0 = p0 + (jnp.arange(nc, dtype=jnp.int32) * (tc * TOP_K))[:, None]
    return (e_of.reshape(-1).astype(jnp.int32), p0.reshape(-1).astype(jnp.int32),
            n.reshape(-1).astype(jnp.int32), tok.reshape(-1).astype(jnp.int32))


def _residual_kernel(x_ref, moe_ref, g_ref, o_ref):
    o_ref[0] = x_ref[0] + g_ref[0] * moe_ref[0]


def _residual(x, moe, g, tm):
    b, s, d = x.shape
    tok = lambda bi, j: (bi, j, 0)
    return pl.pallas_call(
        _residual_kernel,
        grid=(b, s // tm),
        in_specs=[pl.BlockSpec((1, tm, d), tok), pl.BlockSpec((1, tm, d), tok),
                  pl.BlockSpec((1, 1, d), lambda bi, j: (bi, 0, 0))],
        out_specs=pl.BlockSpec((1, tm, d), tok),
        out_shape=jax.ShapeDtypeStruct((b, s, d), F32),
        compiler_params=_params(("parallel", "parallel")),
    )(x, moe, g)


def _pad_heads(w, n_heads):
    lead = w.shape[:-1]
    w = w.reshape(lead + (n_heads, HEAD_DIM))
    w = jnp.concatenate([w, jnp.zeros_like(w)], axis=-1)
    return w.reshape(lead + (n_heads * LANES,))


def _layer_weights(l, d, tm, w_in, b_fgate, qk_gain, w_br_fox, w_br_swa, w_br_moba, w_out,
                   w_router, b_router, w_exp1, b_exp1, w_exp2, b_exp2):
    splits = (W_FOX, W_FOX, W_FOX, FOX_HEADS, W_SWA, W_SWA_KV, W_SWA_KV, W_MOBA, W_MOBA, W_MOBA, 3 * d)
    pts = np.cumsum(splits)[:-1].tolist()
    qa, ka, va, fa, qb, kb, vb, qc, kc, vc, gt = jnp.split(w_in[l], pts, axis=1)
    order = np.asarray(SWA_HEAD_ORDER)
    qb = qb.reshape(d, SWA_HEADS, HEAD_DIM)[:, order].reshape(d, W_SWA)
    w_cat = jnp.concatenate([qa, ka, va, qb, kb, vb, _pad_heads(qc, MOBA_HEADS), _pad_heads(kc, MOBA_HEADS),
                             vc, gt], axis=1).astype(BF16)
    w_f = jnp.zeros((8, d), F32).at[:FOX_HEADS].set(fa.T).astype(BF16)
    b_f = jnp.zeros((8, 1), F32).at[:FOX_HEADS, 0].set(b_fgate[l])
    g = qk_gain[l].astype(F32)
    qscale = HEAD_DIM ** -0.5
    gvec = jnp.concatenate([
        jnp.tile(g[0], FOX_HEADS) * qscale, jnp.tile(g[1], FOX_HEADS), jnp.ones((W_FOX,), F32),
        jnp.tile(g[2], SWA_HEADS) * qscale, jnp.tile(g[3], SWA_KV_HEADS), jnp.ones((W_SWA_KV,), F32),
        _pad_heads(jnp.tile(g[4], MOBA_HEADS) * qscale, MOBA_HEADS),
        _pad_heads(jnp.tile(g[5], MOBA_HEADS), MOBA_HEADS), jnp.ones((W_MOBA,), F32),
    ])[None, :]
    heads = MXU_DIM // HEAD_DIM
    gsum = jnp.asarray(np.kron(np.eye(heads), np.ones((HEAD_DIM, HEAD_DIM))), BF16)
    tri = jnp.asarray(np.triu(np.ones((tm, tm))), F32)
    n_exp = w_router.shape[2]
    w_r = jnp.zeros((d, LANES), F32).at[:, :n_exp].set(w_router[l])
    b_r = jnp.full((1, LANES), NEG, F32).at[0, :n_exp].set(b_router[l])
    w_b = w_br_swa[l].reshape(SWA_HEADS, HEAD_DIM, d)[order].reshape(W_SWA, d)
    f = w_exp2.shape[2]
    return dict(
        w_cat=w_cat, w_f=w_f, b_f=b_f, gvec=gvec, gsum=gsum, tri=tri,
        w_a=w_br_fox[l].astype(BF16), w_b=w_b.astype(BF16), w_c=w_br_moba[l].astype(BF16),
        w_o=w_out[l].astype(BF16), w_r=w_r, b_r=b_r,
        w1g=w_exp1[l, :, :, 0::2].astype(BF16), w1l=w_exp1[l, :, :, 1::2].astype(BF16),
        b1g=b_exp1[l, :, None, 0::2], b1l=b_exp1[l, :, None, 1::2],
        w2=w_exp2[l].astype(BF16), b2=b_exp2[l][:, None, :],
    )


def kernel(x, c, w_ada, b_ada, norm_gain, w_in, b_fgate, qk_gain, attn_sinks, w_br_fox, w_br_swa, w_br_moba, w_out, w_router, b_router, w_exp1, b_exp1, w_exp2, b_exp2):
    b, s, d = x.shape
    depth = w_ada.shape[0]
    n_experts = w_router.shape[2]
    tm = min(s, 512)
    tq_fox = min(s, 512)
    tq_swa = min(s, 256)
    tc = min(s, 2048)
    tm_moe = 256
    n_alibi = SWA_HEADS + MOBA_HEADS
    slopes = jnp.exp2(-8.0 * jnp.arange(1, n_alibi + 1, dtype=F32) / n_alibi)

    mod = _modulation(c, w_ada, b_ada)
    mod = mod.reshape(depth, b, 6, 1, d)
    prev = None
    for l in range(depth):
        sh1, sc1, g1, sh2, sc2, g2 = (mod[l, :, k] for k in range(6))
        wp = _layer_weights(l, d, tm, w_in, b_fgate, qk_gain, w_br_fox, w_br_swa, w_br_moba, w_out,
                            w_router, b_router, w_exp1, b_exp1, w_exp2, b_exp2)
        outs = _in_proj(x, prev, sh1, sc1, norm_gain[l, 0][None, :], wp, tm)
        qf, kf, vf, qs, ks, vs, qm, km, vm, kmean, gates, cum = outs[:12]
        if prev is not None:
            x = outs[12]
        o_a = _fox_attention(qf, kf, vf, cum, jnp.swapaxes(cum, 1, 2), tq_fox)
        o_b = _swa_attention(qs, ks, vs, attn_sinks[l].astype(F32), slopes, tq_swa)
        nblk = s // MOBA_BLOCK
        km4 = kmean.reshape(b, nblk, MOBA_HEADS, LANES)[..., :HEAD_DIM]
        kmat = jnp.zeros((b, MOBA_HEADS, LANES, LANES), F32)
        kmat = kmat.at[:, :, :HEAD_DIM, HEAD_DIM:HEAD_DIM + nblk].set(jnp.transpose(km4, (0, 2, 3, 1)))
        o_c = _moba_attention(qm, km, vm, kmat, slopes)
        x, hx, route = _out_proj(o_a, o_b, o_c, gates, x, g1, sh2, sc2, norm_gain[l, 1][None, :], wp, tm)
        tables = _routing_tables(route, n_experts, tc, tm_moe)
        moe = _moe(hx, tables, wp, tc, tm_moe)
        prev = (moe, g2)
    return _residual(x, prev[0], prev[1], tm)
```

```python
import functools

import numpy as np
import jax
import jax.numpy as jnp
from jax import lax
from jax.experimental import pallas as pl
from jax.experimental.pallas import tpu as pltpu

HEAD_DIM = 64
FOX_HEADS = 4
SWA_HEADS = 8
SWA_KV_HEADS = 2
SWA_WINDOW = 128
MOBA_HEADS = 4
MOBA_BLOCK = 256
MOBA_TOPK = 3
TOP_K = 4
SWIGLU_ALPHA = 1.702
SWIGLU_LIMIT = 7.0
EPS = 1e-6
NEG = -1e30
MOBA_MASK_BIAS = -float(2 ** 30)

LANES = 128
MXU_DIM = 256
VMEM_LIMIT = 60 * 1024 * 1024

W_FOX = FOX_HEADS * HEAD_DIM
W_SWA = SWA_HEADS * HEAD_DIM
W_SWA_KV = SWA_KV_HEADS * HEAD_DIM
W_MOBA = MOBA_HEADS * HEAD_DIM
W_MOBA_AUG = MOBA_HEADS * LANES
SWA_HEAD_ORDER = (0, 4, 1, 5, 2, 6, 3, 7)

F32 = jnp.float32
BF16 = jnp.bfloat16
HIGHEST = lax.Precision.HIGHEST
NT_DIMS = (((1,), (1,)), ((), ()))


def _params(semantics):
    return pltpu.CompilerParams(dimension_semantics=semantics, vmem_limit_bytes=VMEM_LIMIT)


def _iota(shape, dim, dtype=jnp.int32):
    return lax.broadcasted_iota(dtype, shape, dim)


def _mod_kernel(c_ref, w_ref, b_ref, o_ref):
    c = c_ref[...]
    s = c * jax.nn.sigmoid(c)
    o_ref[0] = jnp.dot(s, w_ref[0], precision=HIGHEST, preferred_element_type=F32) + b_ref[0]


def _modulation(c, w_ada, b_ada):
    depth, d, n = w_ada.shape
    b = c.shape[0]
    tn = d
    return pl.pallas_call(
        _mod_kernel,
        grid=(depth, n // tn),
        in_specs=[
            pl.BlockSpec((b, d), lambda l, j: (0, 0)),
            pl.BlockSpec((1, d, tn), lambda l, j: (l, 0, j)),
            pl.BlockSpec((1, 1, tn), lambda l, j: (l, 0, j)),
        ],
        out_specs=pl.BlockSpec((1, b, tn), lambda l, j: (l, 0, j)),
        out_shape=jax.ShapeDtypeStruct((depth, b, n), F32),
        compiler_params=_params(("parallel", "parallel")),
    )(c, w_ada, b_ada.reshape(depth, 1, n))


def _head_norm(z, g_ref, gain):
    w = z.shape[1]
    cw = min(w, MXU_DIM)
    outs = []
    for c0 in range(0, w, cw):
        zz = z[:, c0:c0 + cw]
        ss = jnp.dot((zz * zz).astype(BF16), g_ref[:cw, :cw], preferred_element_type=F32)
        outs.append(zz * lax.rsqrt(ss * (1.0 / HEAD_DIM) + EPS))
    zn = outs[0] if len(outs) == 1 else jnp.concatenate(outs, axis=1)
    return zn * gain


def _in_kernel(*refs, has_prev, tm, d):
    if has_prev:
        x_ref, moe_ref, g2_ref = refs[:3]
        refs = refs[3:]
    else:
        x_ref = refs[0]
        refs = refs[1:]
    (sh_ref, sc_ref, gain_ref, w_ref, wf_ref, bf_ref, gvec_ref, g_ref, tri_ref,
     qf_ref, kf_ref, vf_ref, qs_ref, ks_ref, vs_ref, qm_ref, km_ref, vm_ref,
     kmean_ref, gates_ref, cum_ref) = refs[:21]
    refs = refs[21:]
    if has_prev:
        xn_ref, carry_ref = refs
    else:
        (carry_ref,) = refs
    j = pl.program_id(1)

    x = x_ref[0]
    if has_prev:
        x = x + g2_ref[0] * moe_ref[0]
        xn_ref[0] = x
    ms = jnp.mean(x * x, axis=-1, keepdims=True)
    h = x * lax.rsqrt(ms + EPS) * gain_ref[...]
    h = h * (1.0 + sc_ref[0]) + sh_ref[0]
    hb = h.astype(BF16)

    def proj(c0, c1):
        return jnp.dot(hb, w_ref[:, c0:c1], preferred_element_type=F32)

    def gain(c0, c1):
        return gvec_ref[:, c0:c1]

    o = 0
    z = proj(o, o + 3 * W_FOX)
    qf_ref[0] = _head_norm(z[:, :W_FOX], g_ref, gain(o, o + W_FOX)).astype(BF16)
    kf_ref[0] = _head_norm(z[:, W_FOX:2 * W_FOX], g_ref, gain(o + W_FOX, o + 2 * W_FOX)).astype(BF16)
    vf_ref[0] = z[:, 2 * W_FOX:].astype(BF16)
    o += 3 * W_FOX
    z = proj(o, o + W_SWA + 2 * W_SWA_KV)
    qs_ref[0] = _head_norm(z[:, :W_SWA], g_ref, gain(o, o + W_SWA)).astype(BF16)
    ks_ref[0] = _head_norm(z[:, W_SWA:W_SWA + W_SWA_KV], g_ref,
                           gain(o + W_SWA, o + W_SWA + W_SWA_KV)).astype(BF16)
    vs_ref[0] = z[:, W_SWA + W_SWA_KV:].astype(BF16)
    o += W_SWA + 2 * W_SWA_KV
    z = proj(o, o + 2 * W_MOBA_AUG + W_MOBA)
    qm_ref[0] = _head_norm(z[:, :W_MOBA_AUG], g_ref, gain(o, o + W_MOBA_AUG))
    kn = _head_norm(z[:, W_MOBA_AUG:2 * W_MOBA_AUG], g_ref, gain(o + W_MOBA_AUG, o + 2 * W_MOBA_AUG))
    nb = tm // MOBA_BLOCK
    kmean_ref[0] = jnp.mean(kn.reshape(nb, MOBA_BLOCK, W_MOBA_AUG), axis=1).reshape(nb, 1, W_MOBA_AUG)
    lane = _iota((tm, W_MOBA_AUG), 1) % LANES
    blk = (j * tm + _iota((tm, W_MOBA_AUG), 0)) // MOBA_BLOCK
    km_ref[0] = jnp.where(lane == HEAD_DIM + blk, 1.0, kn).astype(BF16)
    vm_ref[0] = z[:, 2 * W_MOBA_AUG:].astype(BF16)
    o += 2 * W_MOBA_AUG + W_MOBA
    for br in range(3):
        zg = proj(o + br * d, o + (br + 1) * d)
        gates_ref[0, :, br * d:(br + 1) * d] = jax.nn.sigmoid(zg).astype(BF16)

    @pl.when(j == 0)
    def _():
        carry_ref[...] = jnp.zeros_like(carry_ref)

    fa = lax.dot_general(wf_ref[...], hb, NT_DIMS, preferred_element_type=F32) + bf_ref[...]
    logf = -(jnp.maximum(-fa, 0.0) + jnp.log(1.0 + jnp.exp(-jnp.abs(fa))))
    cs = jnp.dot(logf, tri_ref[...], precision=HIGHEST, preferred_element_type=F32) + carry_ref[:, 0:1]
    cum_ref[0] = cs
    carry_ref[...] = jnp.broadcast_to(cs[:, tm - 1:tm], carry_ref.shape)


def _in_proj(x, prev, sh, sc, gain, wp, tm):
    b, s, d = x.shape
    has_prev = prev is not None
    nw = wp["w_cat"].shape[1]
    tok = lambda bi, j: (bi, j, 0)
    row = lambda bi, j: (bi, 0, 0)
    const2 = lambda bi, j: (0, 0)
    in_specs = [pl.BlockSpec((1, tm, d), tok)]
    args = [x]
    if has_prev:
        in_specs += [pl.BlockSpec((1, tm, d), tok), pl.BlockSpec((1, 1, d), row)]
        args += [prev[0], prev[1]]
    in_specs += [
        pl.BlockSpec((1, 1, d), row), pl.BlockSpec((1, 1, d), row), pl.BlockSpec((1, d), const2),
        pl.BlockSpec((d, nw), const2), pl.BlockSpec((8, d), const2), pl.BlockSpec((8, 1), const2),
        pl.BlockSpec((1, wp["gvec"].shape[1]), const2), pl.BlockSpec((MXU_DIM, MXU_DIM), const2),
        pl.BlockSpec((tm, tm), const2),
    ]
    args += [sh, sc, gain, wp["w_cat"], wp["w_f"], wp["b_f"], wp["gvec"], wp["gsum"], wp["tri"]]
    nblk = s // MOBA_BLOCK
    out_shape = [
        jax.ShapeDtypeStruct((b, s, W_FOX), BF16), jax.ShapeDtypeStruct((b, s, W_FOX), BF16),
        jax.ShapeDtypeStruct((b, s, W_FOX), BF16),
        jax.ShapeDtypeStruct((b, s, W_SWA), BF16), jax.ShapeDtypeStruct((b, s, W_SWA_KV), BF16),
        jax.ShapeDtypeStruct((b, s, W_SWA_KV), BF16),
        jax.ShapeDtypeStruct((b, s, W_MOBA_AUG), F32), jax.ShapeDtypeStruct((b, s, W_MOBA_AUG), BF16),
        jax.ShapeDtypeStruct((b, s, W_MOBA), BF16),
        jax.ShapeDtypeStruct((b, nblk, 1, W_MOBA_AUG), F32),
        jax.ShapeDtypeStruct((b, s, 3 * d), BF16),
        jax.ShapeDtypeStruct((b, 8, s), F32),
    ]
    widths = [W_FOX, W_FOX, W_FOX, W_SWA, W_SWA_KV, W_SWA_KV, W_MOBA_AUG, W_MOBA_AUG, W_MOBA]
    out_specs = [pl.BlockSpec((1, tm, w), tok) for w in widths]
    out_specs += [
        pl.BlockSpec((1, tm // MOBA_BLOCK, 1, W_MOBA_AUG), lambda bi, j: (bi, j, 0, 0)),
        pl.BlockSpec((1, tm, 3 * d), tok),
        pl.BlockSpec((1, 8, tm), lambda bi, j: (bi, 0, j)),
    ]
    if has_prev:
        out_shape.append(jax.ShapeDtypeStruct((b, s, d), F32))
        out_specs.append(pl.BlockSpec((1, tm, d), tok))
    return pl.pallas_call(
        functools.partial(_in_kernel, has_prev=has_prev, tm=tm, d=d),
        grid=(b, s // tm),
        in_specs=in_specs,
        out_specs=out_specs,
        out_shape=out_shape,
        scratch_shapes=[pltpu.VMEM((8, LANES), F32)],
        compiler_params=_params(("parallel", "arbitrary")),
    )(*args)


def _fox_kernel(q_ref, k_ref, v_ref, crow_ref, ccol_ref, o_ref, m_ref, l_ref, acc_ref, *, tq):
    i = pl.program_id(1)
    q = q_ref[0]
    lane_head = _iota((1, W_FOX), 1) // HEAD_DIM
    qh = [jnp.where(lane_head == h, q, jnp.zeros_like(q)) for h in range(FOX_HEADS)]
    ccol = ccol_ref[0]
    m_ref[...] = jnp.full_like(m_ref, NEG)
    l_ref[...] = jnp.zeros_like(l_ref)
    acc_ref[...] = jnp.zeros_like(acc_ref)
    causal = _iota((tq, tq), 1) <= _iota((tq, tq), 0)

    def block(j, diagonal):
        start = pl.multiple_of(j * tq, tq)
        ks = k_ref[0, pl.ds(start, tq), :]
        vs = v_ref[0, pl.ds(start, tq), :]
        scale = jnp.zeros((tq, W_FOX), F32)
        add = jnp.zeros((tq, W_FOX), F32)
        for h in range(FOX_HEADS):
            s = lax.dot_general(qh[h], ks, NT_DIMS, preferred_element_type=F32)
            s = s + (ccol[:, h:h + 1] - crow_ref[0, h:h + 1, pl.ds(start, tq)])
            if diagonal:
                s = jnp.where(causal, s, NEG)
            m_old = m_ref[h]
            m_new = jnp.maximum(m_old, jnp.max(s, axis=-1, keepdims=True))
            alpha = jnp.exp(m_old - m_new)
            p = jnp.exp(s - m_new)
            l_ref[h] = alpha * l_ref[h] + jnp.sum(p, axis=-1, keepdims=True)
            m_ref[h] = m_new
            pv = jnp.dot(p.astype(BF16), vs, preferred_element_type=F32)
            sel = lane_head == h
            scale = jnp.where(sel, alpha, scale)
            add = jnp.where(sel, pv, add)
        acc_ref[...] = acc_ref[...] * scale + add

    def body(j, carry):
        block(j, False)
        return carry

    lax.fori_loop(0, i, body, 0)
    block(i, True)
    inv = jnp.zeros((tq, W_FOX), F32)
    for h in range(FOX_HEADS):
        inv = jnp.where(lane_head == h, 1.0 / l_ref[h], inv)
    o_ref[0] = (acc_ref[...] * inv).astype(BF16)


def _fox_attention(q, k, v, cum_row, cum_col, tq):
    b, s, w = q.shape
    return pl.pallas_call(
        functools.partial(_fox_kernel, tq=tq),
        grid=(b, s // tq),
        in_specs=[
            pl.BlockSpec((1, tq, w), lambda bi, i: (bi, i, 0)),
            pl.BlockSpec((1, s, w), lambda bi, i: (bi, 0, 0)),
            pl.BlockSpec((1, s, w), lambda bi, i: (bi, 0, 0)),
            pl.BlockSpec((1, 8, s), lambda bi, i: (bi, 0, 0)),
            pl.BlockSpec((1, tq, 8), lambda bi, i: (bi, i, 0)),
        ],
        out_specs=pl.BlockSpec((1, tq, w), lambda bi, i: (bi, i, 0)),
        out_shape=jax.ShapeDtypeStruct((b, s, w), BF16),
        scratch_shapes=[pltpu.VMEM((FOX_HEADS, tq, 1), F32), pltpu.VMEM((FOX_HEADS, tq, 1), F32),
                        pltpu.VMEM((tq, w), F32)],
        compiler_params=_params(("parallel", "parallel")),
    )(q, k, v, cum_row, cum_col)


def _swa_kernel(sinks_ref, slopes_ref, q_ref, k_ref, v_ref, o_ref, *, tq):
    i = pl.program_id(1)
    tk = tq + SWA_WINDOW
    start = pl.multiple_of(jnp.maximum(i * tq - SWA_WINDOW, 0), SWA_WINDOW)
    kw = k_ref[0, pl.ds(start, tk), :]
    vw = v_ref[0, pl.ds(start, tk), :]
    dist = ((i * tq + _iota((tq, tk), 0)) - (start + _iota((tq, tk), 1))).astype(F32)
    valid = (dist >= 0.0) & (dist < float(SWA_WINDOW))
    low_lanes = _iota((1, LANES), 1) < HEAD_DIM
    pairs = SWA_HEADS // SWA_KV_HEADS
    for p in range(pairs):
        qg = q_ref[0, :, p * LANES:(p + 1) * LANES]
        outs = []
        for half in range(SWA_KV_HEADS):
            head = p + pairs * half
            mine = low_lanes if half == 0 else jnp.logical_not(low_lanes)
            qm = jnp.where(mine, qg, jnp.zeros_like(qg))
            s = lax.dot_general(qm, kw, NT_DIMS, preferred_element_type=F32)
            s = jnp.where(valid, s - slopes_ref[head] * dist, NEG)
            sink = sinks_ref[head]
            m = jnp.maximum(jnp.max(s, axis=-1, keepdims=True), sink)
            pr = jnp.exp(s - m)
            den = jnp.sum(pr, axis=-1, keepdims=True) + jnp.exp(sink - m)
            outs.append(jnp.dot(pr.astype(BF16), vw, preferred_element_type=F32) / den)
        o_ref[0, :, p * LANES:(p + 1) * LANES] = jnp.where(low_lanes, outs[0], outs[1]).astype(BF16)


def _swa_attention(q, k, v, sinks, slopes, tq):
    b, s, w = q.shape
    wk = k.shape[2]
    smem = pl.BlockSpec(memory_space=pltpu.SMEM)
    return pl.pallas_call(
        functools.partial(_swa_kernel, tq=tq),
        grid=(b, s // tq),
        in_specs=[
            smem, smem,
            pl.BlockSpec((1, tq, w), lambda bi, i: (bi, i, 0)),
            pl.BlockSpec((1, s, wk), lambda bi, i: (bi, 0, 0)),
            pl.BlockSpec((1, s, wk), lambda bi, i: (bi, 0, 0)),
        ],
        out_specs=pl.BlockSpec((1, tq, w), lambda bi, i: (bi, i, 0)),
        out_shape=jax.ShapeDtypeStruct((b, s, w), BF16),
        compiler_params=_params(("parallel", "parallel")),
    )(sinks, slopes, q, k, v)


def _moba_kernel(slopes_ref, q_ref, k_ref, v_ref, kmean_ref, o_ref, m_ref, l_ref, acc_ref):
    tq = MOBA_BLOCK
    i = pl.program_id(1)
    lane = _iota((tq, LANES), 1)
    lanef = lane.astype(F32)
    past = (lane >= HEAD_DIM) & (lane < HEAD_DIM + i)
    qa = []
    for h in range(MOBA_HEADS):
        qh = q_ref[0, :, h * LANES:(h + 1) * LANES]
        gate = jnp.dot(qh, kmean_ref[0, h], precision=HIGHEST, preferred_element_type=F32)
        g = jnp.where(past, gate, NEG)
        chosen = jnp.zeros((tq, LANES), jnp.bool_)
        for _ in range(MOBA_TOPK):
            mx = jnp.max(g, axis=-1, keepdims=True)
            first = jnp.min(jnp.where(g == mx, lanef, float(LANES)), axis=-1, keepdims=True)
            pick = (lanef == first) & (mx > 0.5 * NEG)
            chosen = chosen | pick
            g = jnp.where(pick, NEG, g)
        bias = jnp.where(past & jnp.logical_not(chosen), MOBA_MASK_BIAS, 0.0)
        qa.append((qh + bias).astype(BF16))
    m_ref[...] = jnp.full_like(m_ref, NEG)
    l_ref[...] = jnp.zeros_like(l_ref)
    acc_ref[...] = jnp.zeros_like(acc_ref)
    lane_head = _iota((1, W_MOBA), 1) // HEAD_DIM
    rel = (_iota((tq, tq), 0) - _iota((tq, tq), 1)).astype(F32)

    def block(j, diagonal):
        start = pl.multiple_of(j * tq, tq)
        vs = v_ref[0, pl.ds(start, tq), :]
        dist = rel + ((i - j) * tq).astype(F32)
        scale = jnp.zeros((tq, W_MOBA), F32)
        add = jnp.zeros((tq, W_MOBA), F32)
        for h in range(MOBA_HEADS):
            ks = k_ref[0, pl.ds(start, tq), h * LANES:(h + 1) * LANES]
            s = lax.dot_general(qa[h], ks, NT_DIMS, preferred_element_type=F32)
            s = s - slopes_ref[SWA_HEADS + h] * dist
            if diagonal:
                s = jnp.where(rel >= 0.0, s, NEG)
            m_old = m_ref[h]
            m_new = jnp.maximum(m_old, jnp.max(s, axis=-1, keepdims=True))
            alpha = jnp.exp(m_old - m_new)
            p = jnp.exp(s - m_new)
            l_ref[h] = alpha * l_ref[h] + jnp.sum(p, axis=-1, keepdims=True)
            m_ref[h] = m_new
            pv = jnp.dot(p.astype(BF16), vs, preferred_element_type=F32)
            sel = lane_head == h
            scale = jnp.where(sel, alpha, scale)
            add = jnp.where(sel, pv, add)
        acc_ref[...] = acc_ref[...] * scale + add

    def body(j, carry):
        block(j, False)
        return carry

    lax.fori_loop(0, i, body, 0)
    block(i, True)
    inv = jnp.zeros((tq, W_MOBA), F32)
    for h in range(MOBA_HEADS):
        inv = jnp.where(lane_head == h, 1.0 / l_ref[h], inv)
    o_ref[0] = (acc_ref[...] * inv).astype(BF16)


def _moba_attention(q, k, v, kmean_mat, slopes):
    b, s, wa = q.shape
    w = v.shape[2]
    tq = MOBA_BLOCK
    return pl.pallas_call(
        _moba_kernel,
        grid=(b, s // tq),
        in_specs=[
            pl.BlockSpec(memory_space=pltpu.SMEM),
            pl.BlockSpec((1, tq, wa), lambda bi, i: (bi, i, 0)),
            pl.BlockSpec((1, s, wa), lambda bi, i: (bi, 0, 0)),
            pl.BlockSpec((1, s, w), lambda bi, i: (bi, 0, 0)),
            pl.BlockSpec((1, MOBA_HEADS, LANES, LANES), lambda bi, i: (bi, 0, 0, 0)),
        ],
        out_specs=pl.BlockSpec((1, tq, w), lambda bi, i: (bi, i, 0)),
        out_shape=jax.ShapeDtypeStruct((b, s, w), BF16),
        scratch_shapes=[pltpu.VMEM((MOBA_HEADS, tq, 1), F32), pltpu.VMEM((MOBA_HEADS, tq, 1), F32),
                        pltpu.VMEM((tq, w), F32)],
        compiler_params=_params(("parallel", "parallel")),
    )(slopes, q, k, v, kmean_mat)


def _out_kernel(oa_ref, ob_ref, oc_ref, gates_ref, x_ref, g1_ref, sh_ref, sc_ref, gain_ref,
                wa_ref, wb_ref, wc_ref, wo_ref, wr_ref, br_ref,
                x1_ref, hx_ref, route_ref, *, d):
    ya = jnp.dot(oa_ref[0], wa_ref[...], preferred_element_type=F32)
    yb = jnp.dot(ob_ref[0], wb_ref[...], preferred_element_type=F32)
    yc = jnp.dot(oc_ref[0], wc_ref[...], preferred_element_type=F32)
    g = gates_ref[0]
    mix = (g[:, :d].astype(F32) * ya + g[:, d:2 * d].astype(F32) * yb + g[:, 2 * d:].astype(F32) * yc)
    y = jnp.dot(mix.astype(BF16), wo_ref[...], preferred_element_type=F32)
    x1 = x_ref[0] + g1_ref[0] * y
    x1_ref[0] = x1
    ms = jnp.mean(x1 * x1, axis=-1, keepdims=True)
    h = x1 * lax.rsqrt(ms + EPS) * gain_ref[...]
    h = h * (1.0 + sc_ref[0]) + sh_ref[0]
    hx_ref[0, :, :d] = h
    logits = jnp.dot(h, wr_ref[...], precision=HIGHEST, preferred_element_type=F32) + br_ref[...]
    tm = logits.shape[0]
    lanef = _iota((tm, LANES), 1).astype(F32)
    vals, idxs = [], []
    for _ in range(TOP_K):
        mx = jnp.max(logits, axis=-1, keepdims=True)
        first = jnp.min(jnp.where(logits == mx, lanef, float(LANES)), axis=-1, keepdims=True)
        vals.append(mx)
        idxs.append(first)
        logits = jnp.where(lanef == first, -jnp.inf, logits)
    exps = [jnp.exp(v - vals[0]) for v in vals]
    den = exps[0]
    for e in exps[1:]:
        den = den + e
    gates = jnp.zeros((tm, LANES), F32)
    info = jnp.zeros((tm, LANES), F32)
    for k in range(TOP_K):
        wk = exps[k] / den
        gates = jnp.where(lanef == idxs[k], wk, gates)
        info = jnp.where(lanef == float(k), idxs[k], info)
    hx_ref[0, :, d:] = gates
    route_ref[0] = info


def _out_proj(oa, ob, oc, gates, x, g1, sh, sc, gain, wp, tm):
    b, s, d = x.shape
    tok = lambda bi, j: (bi, j, 0)
    row = lambda bi, j: (bi, 0, 0)
    const2 = lambda bi, j: (0, 0)
    return pl.pallas_call(
        functools.partial(_out_kernel, d=d),
        grid=(b, s // tm),
        in_specs=[
            pl.BlockSpec((1, tm, W_FOX), tok), pl.BlockSpec((1, tm, W_SWA), tok),
            pl.BlockSpec((1, tm, W_MOBA), tok), pl.BlockSpec((1, tm, 3 * d), tok),
            pl.BlockSpec((1, tm, d), tok),
            pl.BlockSpec((1, 1, d), row), pl.BlockSpec((1, 1, d), row), pl.BlockSpec((1, 1, d), row),
            pl.BlockSpec((1, d), const2),
            pl.BlockSpec((W_FOX, d), const2), pl.BlockSpec((W_SWA, d), const2),
            pl.BlockSpec((W_MOBA, d), const2), pl.BlockSpec((d, d), const2),
            pl.BlockSpec((d, LANES), const2), pl.BlockSpec((1, LANES), const2),
        ],
        out_specs=[pl.BlockSpec((1, tm, d), tok), pl.BlockSpec((1, tm, d + LANES), tok),
                   pl.BlockSpec((1, tm, LANES), tok)],
        out_shape=[jax.ShapeDtypeStruct((b, s, d), F32), jax.ShapeDtypeStruct((b, s, d + LANES), F32),
                   jax.ShapeDtypeStruct((b, s, LANES), F32)],
        compiler_params=_params(("parallel", "parallel")),
    )(oa, ob, oc, gates, x, g1, sh, sc, gain, wp["w_a"], wp["w_b"], wp["w_c"], wp["w_o"],
      wp["w_r"], wp["b_r"])


def _moe_kernel(tile_e_ref, tile_p0_ref, tile_off_ref, tile_n_ref, tok_ref,
                hx_ref, w1g_ref, w1l_ref, b1g_ref, b1l_ref, w2_ref, b2_ref,
                out_ref, xg_ref, y_ref, *, nt, tm, d):
    c = pl.program_id(0)
    i = pl.program_id(1)
    t = c * nt + i
    e = tile_e_ref[t]
    p0 = tile_p0_ref[t]
    off = tile_off_ref[t]
    n = tile_n_ref[t]

    @pl.when(i == 0)
    def _():
        out_ref[...] = jnp.zeros_like(out_ref)

    @pl.when(n > 0)
    def _():
        def gather(g, carry):
            base = pl.multiple_of(g * 8, 8)
            for u in range(8):
                tk = tok_ref[p0 + base + u]
                xg_ref[pl.ds(base + u, 1), :] = hx_ref[0, pl.ds(tk, 1), :]
            return carry

        lax.fori_loop(0, tm // 8, gather, 0)
        xg = xg_ref[...]
        xb = xg[:, :d].astype(BF16)
        lane = _iota((tm, LANES), 1)
        wcol = jnp.sum(jnp.where(lane == e, xg[:, d:], 0.0), axis=-1, keepdims=True)
        ug = jnp.dot(xb, w1g_ref[0], preferred_element_type=F32) + b1g_ref[0]
        ul = jnp.dot(xb, w1l_ref[0], preferred_element_type=F32) + b1l_ref[0]
        ug = jnp.minimum(ug, SWIGLU_LIMIT)
        ul = jnp.clip(ul, -SWIGLU_LIMIT, SWIGLU_LIMIT)
        act = ug * jax.nn.sigmoid(SWIGLU_ALPHA * ug) * (ul + 1.0)
        y = jnp.dot(act.astype(BF16), w2_ref[0], preferred_element_type=F32) + b2_ref[0]
        y_ref[...] = y * wcol

        def scatter(r, carry):
            tk = tok_ref[p0 + r]
            out_ref[0, pl.ds(tk, 1), :] = out_ref[0, pl.ds(tk, 1), :] + y_ref[pl.ds(r, 1), :]
            return carry

        def scatter4(g, carry):
            r = off + g * 4
            toks = [tok_ref[p0 + r + u] for u in range(4)]
            rows = [out_ref[0, pl.ds(tk, 1), :] for tk in toks]
            for u in range(4):
                out_ref[0, pl.ds(toks[u], 1), :] = rows[u] + y_ref[pl.ds(r + u, 1), :]
            return carry

        n4 = n // 4
        lax.fori_loop(0, n4, scatter4, 0)
        lax.fori_loop(off + n4 * 4, off + n, scatter, 0)


def _moe(hx, tables, wp, tc, tm):
    b, s, dx = hx.shape
    d = dx - LANES
    nc = (b * s) // tc
    tile_e = tables[0]
    nt = tile_e.shape[0] // nc
    f = wp["w1g"].shape[2]
    hx = hx.reshape(nc, tc, dx)
    chunk = lambda c, i, *prefetch: (c, 0, 0)
    expert = lambda c, i, te, *prefetch: (te[c * nt + i], 0, 0)
    out = pl.pallas_call(
        functools.partial(_moe_kernel, nt=nt, tm=tm, d=d),
        grid_spec=pltpu.PrefetchScalarGridSpec(
            num_scalar_prefetch=len(tables),
            grid=(nc, nt),
            in_specs=[
                pl.BlockSpec((1, tc, dx), chunk),
                pl.BlockSpec((1, d, f), expert), pl.BlockSpec((1, d, f), expert),
                pl.BlockSpec((1, 1, f), expert), pl.BlockSpec((1, 1, f), expert),
                pl.BlockSpec((1, f, d), expert), pl.BlockSpec((1, 1, d), expert),
            ],
            out_specs=pl.BlockSpec((1, tc, d), chunk),
            scratch_shapes=[pltpu.VMEM((tm, dx), F32), pltpu.VMEM((tm, d), F32)],
        ),
        out_shape=jax.ShapeDtypeStruct((nc, tc, d), F32),
        compiler_params=_params(("parallel", "arbitrary")),
    )(*tables, hx, wp["w1g"], wp["w1l"], wp["b1g"], wp["b1l"], wp["w2"], wp["b2"])
    return out.reshape(b, s, d)


def _routing_tables(route, n_experts, tc, tm):
    b, s, _ = route.shape
    t = b * s
    nc = t // tc
    idx = route[:, :, :TOP_K].astype(jnp.int32).reshape(nc, tc, TOP_K)
    key = idx * tc + jnp.arange(tc, dtype=jnp.int32)[None, :, None]
    key = jnp.sort(key.reshape(nc, tc * TOP_K), axis=1)
    tok = key % tc
    e_sorted = key // tc
    experts = jnp.arange(n_experts, dtype=jnp.int32)
    cstart = jnp.sum(e_sorted[:, :, None] < experts[None, None, :], axis=1).astype(jnp.int32)
    cend = jnp.concatenate([cstart[:, 1:], jnp.full((nc, 1), tc * TOP_K, jnp.int32)], axis=1)
    count = cend - cstart
    tiles = (count + tm - 1) // tm
    tend = jnp.cumsum(tiles, axis=1)
    tstart = tend - tiles
    nt = (tc * TOP_K) // tm + n_experts
    slot = jnp.arange(nt, dtype=jnp.int32)
    total = tend[:, -1:]
    live = slot[None, :] < total
    slot_c = jnp.minimum(slot[None, :], total - 1)
    e_of = jnp.sum(slot_c[:, :, None] >= tend[:, None, :], axis=2).astype(jnp.int32)
    pick = lambda tbl: jnp.sum(jnp.where(e_of[:, :, None] == experts[None, None, :], tbl[:, None, :], 0), axis=2)
    jj = slot_c - pick(tstart)
    p0 = pick(cstart) + jj * tm
    n = jnp.where(live, jnp.clip(pick(count) - jj * tm, 0, tm), 0)
    p0 = p0 + (jnp.arange(nc, dtype=jnp.int32) * (tc * TOP_K))[:, None]
    first = jnp.minimum(p0, nc * tc * TOP_K - tm)
    flat = lambda a: a.reshape(-1).astype(jnp.int32)
    return flat(e_of), flat(first), flat(p0 - first), flat(n), flat(tok)


def _residual_kernel(x_ref, moe_ref, g_ref, o_ref):
    o_ref[0] = x_ref[0] + g_ref[0] * moe_ref[0]


def _residual(x, moe, g, tm):
    b, s, d = x.shape
    tok = lambda bi, j: (bi, j, 0)
    return pl.pallas_call(
        _residual_kernel,
        grid=(b, s // tm),
        in_specs=[pl.BlockSpec((1, tm, d), tok), pl.BlockSpec((1, tm, d), tok),
                  pl.BlockSpec((1, 1, d), lambda bi, j: (bi, 0, 0))],
        out_specs=pl.BlockSpec((1, tm, d), tok),
        out_shape=jax.ShapeDtypeStruct((b, s, d), F32),
        compiler_params=_params(("parallel", "parallel")),
    )(x, moe, g)


def _split_kernel(w_ref, pe_ref, po_ref, g_ref, l_ref):
    w = w_ref[0].astype(BF16)
    g_ref[0] = jnp.dot(w, pe_ref[...], preferred_element_type=F32).astype(BF16)
    l_ref[0] = jnp.dot(w, po_ref[...], preferred_element_type=F32).astype(BF16)


def _split_glu_weights(w):
    n, d, f2 = w.shape
    cw = 2 * MXU_DIM
    k = np.arange(MXU_DIM)
    pe = np.zeros((cw, MXU_DIM), np.float32)
    po = np.zeros((cw, MXU_DIM), np.float32)
    pe[2 * k, k] = 1.0
    po[2 * k + 1, k] = 1.0
    const2 = lambda i, j: (0, 0)
    out = jax.ShapeDtypeStruct((n, d, f2 // 2), BF16)
    return pl.pallas_call(
        _split_kernel,
        grid=(n, f2 // cw),
        in_specs=[pl.BlockSpec((1, d, cw), lambda i, j: (i, 0, j)),
                  pl.BlockSpec((cw, MXU_DIM), const2), pl.BlockSpec((cw, MXU_DIM), const2)],
        out_specs=[pl.BlockSpec((1, d, MXU_DIM), lambda i, j: (i, 0, j))] * 2,
        out_shape=[out, out],
        compiler_params=_params(("parallel", "parallel")),
    )(w, jnp.asarray(pe, BF16), jnp.asarray(po, BF16))


def _pad_heads(w, n_heads):
    lead = w.shape[:-1]
    w = w.reshape(lead + (n_heads, HEAD_DIM))
    w = jnp.concatenate([w, jnp.zeros_like(w)], axis=-1)
    return w.reshape(lead + (n_heads * LANES,))


def _layer_weights(l, d, tm, w_in, b_fgate, qk_gain, w_br_fox, w_br_swa, w_br_moba, w_out,
                   w_router, b_router, w1g, w1l, b_exp1, w_exp2, b_exp2):
    splits = (W_FOX, W_FOX, W_FOX, FOX_HEADS, W_SWA, W_SWA_KV, W_SWA_KV, W_MOBA, W_MOBA, W_MOBA, 3 * d)
    pts = np.cumsum(splits)[:-1].tolist()
    qa, ka, va, fa, qb, kb, vb, qc, kc, vc, gt = jnp.split(w_in[l], pts, axis=1)
    order = np.asarray(SWA_HEAD_ORDER)
    qb = qb.reshape(d, SWA_HEADS, HEAD_DIM)[:, order].reshape(d, W_SWA)
    w_cat = jnp.concatenate([qa, ka, va, qb, kb, vb, _pad_heads(qc, MOBA_HEADS), _pad_heads(kc, MOBA_HEADS),
                             vc, gt], axis=1).astype(BF16)
    w_f = jnp.zeros((8, d), F32).at[:FOX_HEADS].set(fa.T).astype(BF16)
    b_f = jnp.zeros((8, 1), F32).at[:FOX_HEADS, 0].set(b_fgate[l])
    g = qk_gain[l].astype(F32)
    qscale = HEAD_DIM ** -0.5
    gvec = jnp.concatenate([
        jnp.tile(g[0], FOX_HEADS) * qscale, jnp.tile(g[1], FOX_HEADS), jnp.ones((W_FOX,), F32),
        jnp.tile(g[2], SWA_HEADS) * qscale, jnp.tile(g[3], SWA_KV_HEADS), jnp.ones((W_SWA_KV,), F32),
        _pad_heads(jnp.tile(g[4], MOBA_HEADS) * qscale, MOBA_HEADS),
        _pad_heads(jnp.tile(g[5], MOBA_HEADS), MOBA_HEADS), jnp.ones((W_MOBA,), F32),
    ])[None, :]
    heads = MXU_DIM // HEAD_DIM
    gsum = jnp.asarray(np.kron(np.eye(heads), np.ones((HEAD_DIM, HEAD_DIM))), BF16)
    tri = jnp.asarray(np.triu(np.ones((tm, tm))), F32)
    n_exp = w_router.shape[2]
    w_r = jnp.zeros((d, LANES), F32).at[:, :n_exp].set(w_router[l])
    b_r = jnp.full((1, LANES), NEG, F32).at[0, :n_exp].set(b_router[l])
    w_b = w_br_swa[l].reshape(SWA_HEADS, HEAD_DIM, d)[order].reshape(W_SWA, d)
    return dict(
        w_cat=w_cat, w_f=w_f, b_f=b_f, gvec=gvec, gsum=gsum, tri=tri,
        w_a=w_br_fox[l].astype(BF16), w_b=w_b.astype(BF16), w_c=w_br_moba[l].astype(BF16),
        w_o=w_out[l].astype(BF16), w_r=w_r, b_r=b_r,
        w1g=w1g[l], w1l=w1l[l],
        b1g=b_exp1[l, :, None, 0::2], b1l=b_exp1[l, :, None, 1::2],
        w2=w_exp2[l].astype(BF16), b2=b_exp2[l][:, None, :],
    )


def kernel(x, c, w_ada, b_ada, norm_gain, w_in, b_fgate, qk_gain, attn_sinks, w_br_fox, w_br_swa, w_br_moba, w_out, w_router, b_router, w_exp1, b_exp1, w_exp2, b_exp2):
    b, s, d = x.shape
    depth = w_ada.shape[0]
    n_experts = w_router.shape[2]
    tm = min(s, 512)
    tq_fox = min(s, 512)
    tq_swa = min(s, 256)
    tc = min(s, 2048)
    tm_moe = 256
    n_alibi = SWA_HEADS + MOBA_HEADS
    slopes = jnp.exp2(-8.0 * jnp.arange(1, n_alibi + 1, dtype=F32) / n_alibi)

    mod = _modulation(c, w_ada, b_ada)
    mod = mod.reshape(depth, b, 6, 1, d)
    w1g, w1l = _split_glu_weights(w_exp1.reshape((depth * n_experts,) + w_exp1.shape[2:]))
    w1g = w1g.reshape((depth, n_experts) + w1g.shape[1:])
    w1l = w1l.reshape((depth, n_experts) + w1l.shape[1:])
    prev = None
    for l in range(depth):
        sh1, sc1, g1, sh2, sc2, g2 = (mod[l, :, k] for k in range(6))
        wp = _layer_weights(l, d, tm, w_in, b_fgate, qk_gain, w_br_fox, w_br_swa, w_br_moba, w_out,
                            w_router, b_router, w1g, w1l, b_exp1, w_exp2, b_exp2)
        outs = _in_proj(x, prev, sh1, sc1, norm_gain[l, 0][None, :], wp, tm)
        qf, kf, vf, qs, ks, vs, qm, km, vm, kmean, gates, cum = outs[:12]
        if prev is not None:
            x = outs[12]
        o_a = _fox_attention(qf, kf, vf, cum, jnp.swapaxes(cum, 1, 2), tq_fox)
        o_b = _swa_attention(qs, ks, vs, attn_sinks[l].astype(F32), slopes, tq_swa)
        nblk = s // MOBA_BLOCK
        km4 = kmean.reshape(b, nblk, MOBA_HEADS, LANES)[..., :HEAD_DIM]
        kmat = jnp.zeros((b, MOBA_HEADS, LANES, LANES), F32)
        kmat = kmat.at[:, :, :HEAD_DIM, HEAD_DIM:HEAD_DIM + nblk].set(jnp.transpose(km4, (0, 2, 3, 1)))
        o_c = _moba_attention(qm, km, vm, kmat, slopes)
        x, hx, route = _out_proj(o_a, o_b, o_c, gates, x, g1, sh2, sc2, norm_gain[l, 1][None, :], wp, tm)
        tables = _routing_tables(route, n_experts, tc, tm_moe)
        moe = _moe(hx, tables, wp, tc, tm_moe)
        prev = (moe, g2)
    return _residual(x, prev[0], prev[1], tm)
```

```python
import functools

import numpy as np
import jax
import jax.numpy as jnp
from jax import lax
from jax.experimental import pallas as pl
from jax.experimental.pallas import tpu as pltpu

HEAD_DIM = 64
FOX_HEADS = 4
SWA_HEADS = 8
SWA_KV_HEADS = 2
SWA_WINDOW = 128
MOBA_HEADS = 4
MOBA_BLOCK = 256
MOBA_TOPK = 3
TOP_K = 4
SWIGLU_ALPHA = 1.702
SWIGLU_LIMIT = 7.0
EPS = 1e-6
NEG = -1e30
LOG2E = 1.4426950408889634
MOBA_MASK_BIAS = -float(2 ** 30)

LANES = 128
MXU_DIM = 256
VMEM_LIMIT = 60 * 1024 * 1024

W_FOX = FOX_HEADS * HEAD_DIM
W_SWA = SWA_HEADS * HEAD_DIM
W_SWA_KV = SWA_KV_HEADS * HEAD_DIM
W_MOBA = MOBA_HEADS * HEAD_DIM
W_FOX_AUG = FOX_HEADS * LANES
FOX_CUM_LANE = HEAD_DIM
W_MOBA_AUG = MOBA_HEADS * LANES
MOBA_BLOCK_LANE = HEAD_DIM
MOBA_MAX_BLOCKS = 16
MOBA_ALIBI_LANE = MOBA_BLOCK_LANE + MOBA_MAX_BLOCKS
N_PIECES = 3
SWA_HEAD_ORDER = (0, 4, 1, 5, 2, 6, 3, 7)

F32 = jnp.float32
BF16 = jnp.bfloat16
HIGHEST = lax.Precision.HIGHEST
NT_DIMS = (((1,), (1,)), ((), ()))


def _params(semantics):
    return pltpu.CompilerParams(dimension_semantics=semantics, vmem_limit_bytes=VMEM_LIMIT)


def _iota(shape, dim, dtype=jnp.int32):
    return lax.broadcasted_iota(dtype, shape, dim)


def _pieces(x):
    hi = x.astype(BF16).astype(F32)
    r = x - hi
    mid = r.astype(BF16).astype(F32)
    return hi, mid, r - mid


def _mod_kernel(c_ref, w_ref, b_ref, o_ref):
    c = c_ref[...]
    s = c * jax.nn.sigmoid(c)
    o_ref[0] = jnp.dot(s, w_ref[0], precision=HIGHEST, preferred_element_type=F32) + b_ref[0]


def _modulation(c, w_ada, b_ada):
    depth, d, n = w_ada.shape
    b = c.shape[0]
    tn = d
    return pl.pallas_call(
        _mod_kernel,
        grid=(depth, n // tn),
        in_specs=[
            pl.BlockSpec((b, d), lambda l, j: (0, 0)),
            pl.BlockSpec((1, d, tn), lambda l, j: (l, 0, j)),
            pl.BlockSpec((1, 1, tn), lambda l, j: (l, 0, j)),
        ],
        out_specs=pl.BlockSpec((1, b, tn), lambda l, j: (l, 0, j)),
        out_shape=jax.ShapeDtypeStruct((depth, b, n), F32),
        compiler_params=_params(("parallel", "parallel")),
    )(c, w_ada, b_ada.reshape(depth, 1, n))


def _head_norm(z, g_ref, gain):
    w = z.shape[1]
    cw = min(w, MXU_DIM)
    outs = []
    for c0 in range(0, w, cw):
        zz = z[:, c0:c0 + cw]
        ss = jnp.dot((zz * zz).astype(BF16), g_ref[:cw, :cw], preferred_element_type=F32)
        outs.append(zz * lax.rsqrt(ss * (1.0 / HEAD_DIM) + EPS))
    zn = outs[0] if len(outs) == 1 else jnp.concatenate(outs, axis=1)
    return zn * gain


def _head_norm_t(zt, gain_t):
    rows, cols = zt.shape
    z3 = zt.reshape(rows // HEAD_DIM, HEAD_DIM, cols)
    ms = jnp.mean(z3 * z3, axis=1, keepdims=True)
    return (z3 * lax.rsqrt(ms + EPS)).reshape(rows, cols) * gain_t


def _in_kernel(*refs, has_prev, tm, d):
    if has_prev:
        x_ref, moe_ref, g2_ref = refs[:3]
        refs = refs[3:]
    else:
        x_ref = refs[0]
        refs = refs[1:]
    (sh_ref, sc_ref, gain_ref, w_ref, wt_ref, bf_ref, gvec_ref, gt_ref, g_ref, tril_ref, perm_ref, slope_ref,
     qft_ref, kfa_ref, vft_ref, qs_ref, ks_ref, vs_ref, qmt_ref, kma_ref, vmt_ref,
     kmean_ref, gates_ref) = refs[:23]
    refs = refs[23:]
    if has_prev:
        xn_ref, carry_ref = refs
    else:
        (carry_ref,) = refs
    j = pl.program_id(1)

    x = x_ref[0]
    if has_prev:
        x = x + g2_ref[0] * moe_ref[0]
        xn_ref[0] = x
    ms = jnp.mean(x * x, axis=-1, keepdims=True)
    h = x * lax.rsqrt(ms + EPS) * gain_ref[...]
    h = h * (1.0 + sc_ref[0]) + sh_ref[0]
    hb = h.astype(BF16)

    def proj(c0, c1):
        return jnp.dot(hb, w_ref[:, c0:c1], preferred_element_type=F32)

    def gain(c0, c1):
        return gvec_ref[:, c0:c1]

    zt = lax.dot_general(wt_ref[...], hb, NT_DIMS, preferred_element_type=F32)
    qft_ref[0] = _head_norm_t(zt[:W_FOX], gt_ref[:W_FOX]).astype(BF16)
    vft_ref[0] = zt[W_FOX:2 * W_FOX].astype(BF16)
    o = 2 * W_FOX
    qmt_ref[0] = _head_norm_t(zt[o:o + W_MOBA], gt_ref[W_FOX:W_FOX + W_MOBA])
    vmt_ref[0] = zt[o + W_MOBA:o + 2 * W_MOBA].astype(BF16)

    @pl.when(j == 0)
    def _():
        carry_ref[...] = jnp.zeros_like(carry_ref)

    o = 0
    z = proj(o, o + W_FOX_AUG + LANES)
    kf = _head_norm(z[:, :W_FOX_AUG], g_ref, gain(o, o + W_FOX_AUG))
    fa = z[:, W_FOX_AUG:] + bf_ref[...]
    logf = -(jnp.maximum(-fa, 0.0) + jnp.log(1.0 + jnp.exp(-jnp.abs(fa))))
    cum = jnp.dot(tril_ref[...], logf, precision=HIGHEST, preferred_element_type=F32) + carry_ref[0:1, :]
    carry_ref[...] = jnp.broadcast_to(cum[tm - 1:tm, :], carry_ref.shape)
    parts = jnp.concatenate(_pieces(cum * (-LOG2E)), axis=1).astype(BF16)
    kfa_ref[0] = (kf + jnp.dot(parts, perm_ref[...], preferred_element_type=F32)).astype(BF16)
    o += W_FOX_AUG + LANES
    z = proj(o, o + W_SWA + 2 * W_SWA_KV)
    qs_ref[0] = _head_norm(z[:, :W_SWA], g_ref, gain(o, o + W_SWA)).astype(BF16)
    ks_ref[0] = _head_norm(z[:, W_SWA:W_SWA + W_SWA_KV], g_ref,
                           gain(o + W_SWA, o + W_SWA + W_SWA_KV)).astype(BF16)
    vs_ref[0] = z[:, W_SWA + W_SWA_KV:].astype(BF16)
    o += W_SWA + 2 * W_SWA_KV
    z = proj(o, o + W_MOBA_AUG)
    kn = _head_norm(z, g_ref, gain(o, o + W_MOBA_AUG))
    nb = tm // MOBA_BLOCK
    kmean_ref[0] = jnp.mean(kn.reshape(nb, MOBA_BLOCK, W_MOBA_AUG), axis=1).reshape(nb, 1, W_MOBA_AUG)
    lane = _iota((tm, W_MOBA_AUG), 1) % LANES
    pos = j * tm + _iota((tm, W_MOBA_AUG), 0)
    a_hi, a_mid, a_lo = _pieces(slope_ref[...] * pos.astype(F32))
    ka = jnp.where(lane == MOBA_BLOCK_LANE + pos // MOBA_BLOCK, 1.0, kn)
    ka = jnp.where(lane == MOBA_ALIBI_LANE, a_hi, ka)
    ka = jnp.where(lane == MOBA_ALIBI_LANE + 1, a_mid, ka)
    ka = jnp.where(lane == MOBA_ALIBI_LANE + 2, a_lo, ka)
    kma_ref[0] = ka.astype(BF16)
    o += W_MOBA_AUG
    for br in range(3):
        zg = proj(o + br * d, o + (br + 1) * d)
        gates_ref[0, :, br * d:(br + 1) * d] = jax.nn.sigmoid(zg).astype(BF16)


def _in_proj(x, prev, sh, sc, gain, wp, tm):
    b, s, d = x.shape
    has_prev = prev is not None
    tok = lambda bi, j: (bi, j, 0)
    tok_t = lambda bi, j: (bi, 0, j)
    row = lambda bi, j: (bi, 0, 0)
    const2 = lambda bi, j: (0, 0)
    full2 = lambda a: pl.BlockSpec(a.shape, const2)
    in_specs = [pl.BlockSpec((1, tm, d), tok)]
    args = [x]
    if has_prev:
        in_specs += [pl.BlockSpec((1, tm, d), tok), pl.BlockSpec((1, 1, d), row)]
        args += [prev[0], prev[1]]
    consts = [gain, wp["w_cat"], wp["w_t"], wp["b_f"], wp["gvec"], wp["gain_t"], wp["gsum"], wp["tril"],
              wp["perm"], wp["slope_lanes"]]
    in_specs += [pl.BlockSpec((1, 1, d), row), pl.BlockSpec((1, 1, d), row)] + [full2(a) for a in consts]
    args += [sh, sc] + consts
    nblk = s // MOBA_BLOCK
    out_shape = [
        jax.ShapeDtypeStruct((b, W_FOX, s), BF16), jax.ShapeDtypeStruct((b, s, W_FOX_AUG), BF16),
        jax.ShapeDtypeStruct((b, W_FOX, s), BF16),
        jax.ShapeDtypeStruct((b, s, W_SWA), BF16), jax.ShapeDtypeStruct((b, s, W_SWA_KV), BF16),
        jax.ShapeDtypeStruct((b, s, W_SWA_KV), BF16),
        jax.ShapeDtypeStruct((b, W_MOBA, s), F32), jax.ShapeDtypeStruct((b, s, W_MOBA_AUG), BF16),
        jax.ShapeDtypeStruct((b, W_MOBA, s), BF16),
        jax.ShapeDtypeStruct((b, nblk, 1, W_MOBA_AUG), F32),
        jax.ShapeDtypeStruct((b, s, 3 * d), BF16),
    ]
    out_specs = [
        pl.BlockSpec((1, W_FOX, tm), tok_t), pl.BlockSpec((1, tm, W_FOX_AUG), tok),
        pl.BlockSpec((1, W_FOX, tm), tok_t),
        pl.BlockSpec((1, tm, W_SWA), tok), pl.BlockSpec((1, tm, W_SWA_KV), tok),
        pl.BlockSpec((1, tm, W_SWA_KV), tok),
        pl.BlockSpec((1, W_MOBA, tm), tok_t), pl.BlockSpec((1, tm, W_MOBA_AUG), tok),
        pl.BlockSpec((1, W_MOBA, tm), tok_t),
        pl.BlockSpec((1, tm // MOBA_BLOCK, 1, W_MOBA_AUG), lambda bi, j: (bi, j, 0, 0)),
        pl.BlockSpec((1, tm, 3 * d), tok),
    ]
    if has_prev:
        out_shape.append(jax.ShapeDtypeStruct((b, s, d), F32))
        out_specs.append(pl.BlockSpec((1, tm, d), tok))
    return pl.pallas_call(
        functools.partial(_in_kernel, has_prev=has_prev, tm=tm, d=d),
        grid=(b, s // tm),
        in_specs=in_specs,
        out_specs=out_specs,
        out_shape=out_shape,
        scratch_shapes=[pltpu.VMEM((8, LANES), F32)],
        compiler_params=_params(("parallel", "arbitrary")),
    )(*args)


def _flash_t(i, tq, n_heads, scores, values, o_ref, m_ref, l_ref, acc_ref):
    m_ref[...] = jnp.full_like(m_ref, NEG)
    l_ref[...] = jnp.zeros_like(l_ref)
    acc_ref[...] = jnp.zeros_like(acc_ref)
    causal = _iota((tq, tq), 0) <= _iota((tq, tq), 1)

    def block(j, diagonal):
        for h in range(n_heads):
            s = scores(j, h)
            if diagonal:
                s = jnp.where(causal, s, NEG)
            m_old = m_ref[h]
            m_new = jnp.maximum(m_old, jnp.max(s, axis=0, keepdims=True))
            alpha = jnp.exp2(m_old - m_new)
            p = jnp.exp2(s - m_new)
            l_ref[h] = alpha * l_ref[h] + jnp.sum(p, axis=0, keepdims=True)
            m_ref[h] = m_new
            pv = jnp.dot(values(j, h), p.astype(BF16), preferred_element_type=F32)
            rows = slice(h * HEAD_DIM, (h + 1) * HEAD_DIM)
            acc_ref[rows, :] = acc_ref[rows, :] * alpha + pv

    def body(j, carry):
        block(j, False)
        return carry

    lax.fori_loop(0, i, body, 0)
    block(i, True)
    for h in range(n_heads):
        rows = slice(h * HEAD_DIM, (h + 1) * HEAD_DIM)
        acc_ref[rows, :] = acc_ref[rows, :] * (1.0 / l_ref[h])
    o_ref[0] = jnp.transpose(acc_ref[...]).astype(BF16)


def _fox_kernel(qt_ref, k_ref, vt_ref, o_ref, m_ref, l_ref, acc_ref, *, tq):
    i = pl.program_id(1)
    tail = jnp.where(_iota((LANES - HEAD_DIM, tq), 0) < N_PIECES, 1.0, 0.0).astype(BF16)
    qa = [jnp.concatenate([qt_ref[0, h * HEAD_DIM:(h + 1) * HEAD_DIM, :], tail], axis=0)
          for h in range(FOX_HEADS)]

    def scores(j, h):
        start = pl.multiple_of(j * tq, tq)
        return jnp.dot(k_ref[0, pl.ds(start, tq), h * LANES:(h + 1) * LANES], qa[h],
                       preferred_element_type=F32)

    def values(j, h):
        start = pl.multiple_of(j * tq, tq)
        return vt_ref[0, h * HEAD_DIM:(h + 1) * HEAD_DIM, pl.ds(start, tq)]

    _flash_t(i, tq, FOX_HEADS, scores, values, o_ref, m_ref, l_ref, acc_ref)


def _fox_attention(qt, k_aug, vt, tq):
    b, w, s = qt.shape
    return pl.pallas_call(
        functools.partial(_fox_kernel, tq=tq),
        grid=(b, s // tq),
        in_specs=[
            pl.BlockSpec((1, w, tq), lambda bi, i: (bi, 0, i)),
            pl.BlockSpec((1, s, k_aug.shape[2]), lambda bi, i: (bi, 0, 0)),
            pl.BlockSpec((1, w, s), lambda bi, i: (bi, 0, 0)),
        ],
        out_specs=pl.BlockSpec((1, tq, w), lambda bi, i: (bi, i, 0)),
        out_shape=jax.ShapeDtypeStruct((b, s, w), BF16),
        scratch_shapes=[pltpu.VMEM((FOX_HEADS, 1, tq), F32), pltpu.VMEM((FOX_HEADS, 1, tq), F32),
                        pltpu.VMEM((w, tq), F32)],
        compiler_params=_params(("parallel", "parallel")),
    )(qt, k_aug, vt)


def _moba_kernel(qt_ref, k_ref, vt_ref, kmean_ref, o_ref, m_ref, l_ref, acc_ref, *, tq):
    i = pl.program_id(1)
    row = _iota((LANES, tq), 0)
    rowf = row.astype(F32)
    own = (i * tq + _iota((LANES, tq), 1)) // MOBA_BLOCK
    past = row < own
    tail = jnp.where(_iota((LANES - MOBA_ALIBI_LANE, tq), 0) < N_PIECES, 1.0, 0.0)
    qa = []
    for h in range(MOBA_HEADS):
        qh = qt_ref[0, h * HEAD_DIM:(h + 1) * HEAD_DIM, :]
        gate = jnp.dot(kmean_ref[0, h], qh, precision=HIGHEST, preferred_element_type=F32)
        g = jnp.where(past, gate, NEG)
        chosen = jnp.zeros((LANES, tq), jnp.bool_)
        for _ in range(MOBA_TOPK):
            mx = jnp.max(g, axis=0, keepdims=True)
            first = jnp.min(jnp.where(g == mx, rowf, float(LANES)), axis=0, keepdims=True)
            pick = (rowf == first) & (mx > 0.5 * NEG)
            chosen = chosen | pick
            g = jnp.where(pick, NEG, g)
        bias = jnp.where(past & jnp.logical_not(chosen), MOBA_MASK_BIAS, 0.0)
        qa.append(jnp.concatenate([qh, bias[:MOBA_MAX_BLOCKS], tail], axis=0).astype(BF16))

    def scores(j, h):
        start = pl.multiple_of(j * tq, tq)
        return jnp.dot(k_ref[0, pl.ds(start, tq), h * LANES:(h + 1) * LANES], qa[h],
                       preferred_element_type=F32)

    def values(j, h):
        start = pl.multiple_of(j * tq, tq)
        return vt_ref[0, h * HEAD_DIM:(h + 1) * HEAD_DIM, pl.ds(start, tq)]

    _flash_t(i, tq, MOBA_HEADS, scores, values, o_ref, m_ref, l_ref, acc_ref)


def _moba_attention(qt, k_aug, vt, kmean_mat, tq):
    b, w, s = qt.shape
    assert s // MOBA_BLOCK <= MOBA_MAX_BLOCKS and tq % MOBA_BLOCK == 0
    return pl.pallas_call(
        functools.partial(_moba_kernel, tq=tq),
        grid=(b, s // tq),
        in_specs=[
            pl.BlockSpec((1, w, tq), lambda bi, i: (bi, 0, i)),
            pl.BlockSpec((1, s, k_aug.shape[2]), lambda bi, i: (bi, 0, 0)),
            pl.BlockSpec((1, w, s), lambda bi, i: (bi, 0, 0)),
            pl.BlockSpec((1, MOBA_HEADS, LANES, HEAD_DIM), lambda bi, i: (bi, 0, 0, 0)),
        ],
        out_specs=pl.BlockSpec((1, tq, w), lambda bi, i: (bi, i, 0)),
        out_shape=jax.ShapeDtypeStruct((b, s, w), BF16),
        scratch_shapes=[pltpu.VMEM((MOBA_HEADS, 1, tq), F32), pltpu.VMEM((MOBA_HEADS, 1, tq), F32),
                        pltpu.VMEM((w, tq), F32)],
        compiler_params=_params(("parallel", "parallel")),
    )(qt, k_aug, vt, kmean_mat)


def _swa_kernel(sinks_ref, slopes_ref, q_ref, k_ref, v_ref, o_ref, *, tq):
    i = pl.program_id(1)
    tk = tq + SWA_WINDOW
    start = pl.multiple_of(jnp.maximum(i * tq - SWA_WINDOW, 0), SWA_WINDOW)
    kw = k_ref[0, pl.ds(start, tk), :]
    vw = v_ref[0, pl.ds(start, tk), :]
    dist = ((i * tq + _iota((tq, tk), 0)) - (start + _iota((tq, tk), 1))).astype(F32)
    valid = (dist >= 0.0) & (dist < float(SWA_WINDOW))
    low_lanes = _iota((1, LANES), 1) < HEAD_DIM
    pairs = SWA_HEADS // SWA_KV_HEADS
    for p in range(pairs):
        qg = q_ref[0, :, p * LANES:(p + 1) * LANES]
        outs = []
        for half in range(SWA_KV_HEADS):
            head = p + pairs * half
            mine = low_lanes if half == 0 else jnp.logical_not(low_lanes)
            qm = jnp.where(mine, qg, jnp.zeros_like(qg))
            s = lax.dot_general(qm, kw, NT_DIMS, preferred_element_type=F32)
            s = jnp.where(valid, s - slopes_ref[head] * dist, NEG)
            sink = sinks_ref[head]
            m = jnp.maximum(jnp.max(s, axis=-1, keepdims=True), sink)
            pr = jnp.exp(s - m)
            den = jnp.sum(pr, axis=-1, keepdims=True) + jnp.exp(sink - m)
            outs.append(jnp.dot(pr.astype(BF16), vw, preferred_element_type=F32) / den)
        o_ref[0, :, p * LANES:(p + 1) * LANES] = jnp.where(low_lanes, outs[0], outs[1]).astype(BF16)


def _swa_attention(q, k, v, sinks, slopes, tq):
    b, s, w = q.shape
    wk = k.shape[2]
    smem = pl.BlockSpec(memory_space=pltpu.SMEM)
    return pl.pallas_call(
        functools.partial(_swa_kernel, tq=tq),
        grid=(b, s // tq),
        in_specs=[
            smem, smem,
            pl.BlockSpec((1, tq, w), lambda bi, i: (bi, i, 0)),
            pl.BlockSpec((1, s, wk), lambda bi, i: (bi, 0, 0)),
            pl.BlockSpec((1, s, wk), lambda bi, i: (bi, 0, 0)),
        ],
        out_specs=pl.BlockSpec((1, tq, w), lambda bi, i: (bi, i, 0)),
        out_shape=jax.ShapeDtypeStruct((b, s, w), BF16),
        compiler_params=_params(("parallel", "parallel")),
    )(sinks, slopes, q, k, v)


def _out_kernel(oa_ref, ob_ref, oc_ref, gates_ref, x_ref, g1_ref, sh_ref, sc_ref, gain_ref,
                wa_ref, wb_ref, wc_ref, wo_ref, wr_ref, br_ref,
                x1_ref, hx_ref, route_ref, *, d):
    ya = jnp.dot(oa_ref[0], wa_ref[...], preferred_element_type=F32)
    yb = jnp.dot(ob_ref[0], wb_ref[...], preferred_element_type=F32)
    yc = jnp.dot(oc_ref[0], wc_ref[...], preferred_element_type=F32)
    g = gates_ref[0]
    mix = (g[:, :d].astype(F32) * ya + g[:, d:2 * d].astype(F32) * yb + g[:, 2 * d:].astype(F32) * yc)
    y = jnp.dot(mix.astype(BF16), wo_ref[...], preferred_element_type=F32)
    x1 = x_ref[0] + g1_ref[0] * y
    x1_ref[0] = x1
    ms = jnp.mean(x1 * x1, axis=-1, keepdims=True)
    h = x1 * lax.rsqrt(ms + EPS) * gain_ref[...]
    h = h * (1.0 + sc_ref[0]) + sh_ref[0]
    hx_ref[0, :, :d] = h
    logits = jnp.dot(h, wr_ref[...], precision=HIGHEST, preferred_element_type=F32) + br_ref[...]
    tm = logits.shape[0]
    lanef = _iota((tm, LANES), 1).astype(F32)
    vals, idxs = [], []
    for _ in range(TOP_K):
        mx = jnp.max(logits, axis=-1, keepdims=True)
        first = jnp.min(jnp.where(logits == mx, lanef, float(LANES)), axis=-1, keepdims=True)
        vals.append(mx)
        idxs.append(first)
        logits = jnp.where(lanef == first, -jnp.inf, logits)
    exps = [jnp.exp(v - vals[0]) for v in vals]
    den = exps[0]
    for e in exps[1:]:
        den = den + e
    gates = jnp.zeros((tm, LANES), F32)
    info = jnp.zeros((tm, LANES), F32)
    for k in range(TOP_K):
        wk = exps[k] / den
        gates = jnp.where(lanef == idxs[k], wk, gates)
        info = jnp.where(lanef == float(k), idxs[k], info)
    hx_ref[0, :, d:] = gates
    route_ref[0] = info


def _out_proj(oa, ob, oc, gates, x, g1, sh, sc, gain, wp, tm):
    b, s, d = x.shape
    tok = lambda bi, j: (bi, j, 0)
    row = lambda bi, j: (bi, 0, 0)
    const2 = lambda bi, j: (0, 0)
    return pl.pallas_call(
        functools.partial(_out_kernel, d=d),
        grid=(b, s // tm),
        in_specs=[
            pl.BlockSpec((1, tm, W_FOX), tok), pl.BlockSpec((1, tm, W_SWA), tok),
            pl.BlockSpec((1, tm, W_MOBA), tok), pl.BlockSpec((1, tm, 3 * d), tok),
            pl.BlockSpec((1, tm, d), tok),
            pl.BlockSpec((1, 1, d), row), pl.BlockSpec((1, 1, d), row), pl.BlockSpec((1, 1, d), row),
            pl.BlockSpec((1, d), const2),
            pl.BlockSpec((W_FOX, d), const2), pl.BlockSpec((W_SWA, d), const2),
            pl.BlockSpec((W_MOBA, d), const2), pl.BlockSpec((d, d), const2),
            pl.BlockSpec((d, LANES), const2), pl.BlockSpec((1, LANES), const2),
        ],
        out_specs=[pl.BlockSpec((1, tm, d), tok), pl.BlockSpec((1, tm, d + LANES), tok),
                   pl.BlockSpec((1, tm, LANES), tok)],
        out_shape=[jax.ShapeDtypeStruct((b, s, d), F32), jax.ShapeDtypeStruct((b, s, d + LANES), F32),
                   jax.ShapeDtypeStruct((b, s, LANES), F32)],
        compiler_params=_params(("parallel", "parallel")),
    )(oa, ob, oc, gates, x, g1, sh, sc, gain, wp["w_a"], wp["w_b"], wp["w_c"], wp["w_o"],
      wp["w_r"], wp["b_r"])


def _moe_kernel(tile_e_ref, tile_p0_ref, tile_off_ref, tile_n_ref, tok_ref,
                hx_ref, w1g_ref, w1l_ref, b1g_ref, b1l_ref, w2_ref, b2_ref,
                out_ref, xg_ref, y_ref, *, nt, tm, d):
    c = pl.program_id(0)
    i = pl.program_id(1)
    t = c * nt + i
    e = tile_e_ref[t]
    p0 = tile_p0_ref[t]
    off = tile_off_ref[t]
    n = tile_n_ref[t]

    @pl.when(i == 0)
    def _():
        out_ref[...] = jnp.zeros_like(out_ref)

    @pl.when(n > 0)
    def _():
        def gather(g, carry):
            base = pl.multiple_of(g * 8, 8)
            for u in range(8):
                tk = tok_ref[p0 + base + u]
                xg_ref[pl.ds(base + u, 1), :] = hx_ref[0, pl.ds(tk, 1), :]
            return carry

        lax.fori_loop(0, tm // 8, gather, 0)
        xg = xg_ref[...]
        xb = xg[:, :d].astype(BF16)
        lane = _iota((tm, LANES), 1)
        wcol = jnp.sum(jnp.where(lane == e, xg[:, d:], 0.0), axis=-1, keepdims=True)
        ug = jnp.dot(xb, w1g_ref[0], preferred_element_type=F32) + b1g_ref[0]
        ul = jnp.dot(xb, w1l_ref[0], preferred_element_type=F32) + b1l_ref[0]
        ug = jnp.minimum(ug, SWIGLU_LIMIT)
        ul = jnp.clip(ul, -SWIGLU_LIMIT, SWIGLU_LIMIT)
        act = ug * jax.nn.sigmoid(SWIGLU_ALPHA * ug) * (ul + 1.0)
        y = jnp.dot(act.astype(BF16), w2_ref[0], preferred_element_type=F32) + b2_ref[0]
        y_ref[...] = y * wcol

        def scatter(r, carry):
            tk = tok_ref[p0 + r]
            out_ref[0, pl.ds(tk, 1), :] = out_ref[0, pl.ds(tk, 1), :] + y_ref[pl.ds(r, 1), :]
            return carry

        def scatter4(g, carry):
            r = off + g * 4
            toks = [tok_ref[p0 + r + u] for u in range(4)]
            rows = [out_ref[0, pl.ds(tk, 1), :] for tk in toks]
            for u in range(4):
                out_ref[0, pl.ds(toks[u], 1), :] = rows[u] + y_ref[pl.ds(r + u, 1), :]
            return carry

        n4 = n // 4
        lax.fori_loop(0, n4, scatter4, 0)
        lax.fori_loop(off + n4 * 4, off + n, scatter, 0)


def _moe(hx, tables, wp, tc, tm):
    b, s, dx = hx.shape
    d = dx - LANES
    nc = (b * s) // tc
    tile_e = tables[0]
    nt = tile_e.shape[0] // nc
    f = wp["w1g"].shape[2]
    hx = hx.reshape(nc, tc, dx)
    chunk = lambda c, i, *prefetch: (c, 0, 0)
    expert = lambda c, i, te, *prefetch: (te[c * nt + i], 0, 0)
    out = pl.pallas_call(
        functools.partial(_moe_kernel, nt=nt, tm=tm, d=d),
        grid_spec=pltpu.PrefetchScalarGridSpec(
            num_scalar_prefetch=len(tables),
            grid=(nc, nt),
            in_specs=[
                pl.BlockSpec((1, tc, dx), chunk),
                pl.BlockSpec((1, d, f), expert), pl.BlockSpec((1, d, f), expert),
                pl.BlockSpec((1, 1, f), expert), pl.BlockSpec((1, 1, f), expert),
                pl.BlockSpec((1, f, d), expert), pl.BlockSpec((1, 1, d), expert),
            ],
            out_specs=pl.BlockSpec((1, tc, d), chunk),
            scratch_shapes=[pltpu.VMEM((tm, dx), F32), pltpu.VMEM((tm, d), F32)],
        ),
        out_shape=jax.ShapeDtypeStruct((nc, tc, d), F32),
        compiler_params=_params(("parallel", "arbitrary")),
    )(*tables, hx, wp["w1g"], wp["w1l"], wp["b1g"], wp["b1l"], wp["w2"], wp["b2"])
    return out.reshape(b, s, d)


def _routing_tables(route, n_experts, tc, tm):
    b, s, _ = route.shape
    t = b * s
    nc = t // tc
    idx = route[:, :, :TOP_K].astype(jnp.int32).reshape(nc, tc, TOP_K)
    key = idx * tc + jnp.arange(tc, dtype=jnp.int32)[None, :, None]
    key = jnp.sort(key.reshape(nc, tc * TOP_K), axis=1)
    tok = key % tc
    e_sorted = key // tc
    experts = jnp.arange(n_experts, dtype=jnp.int32)
    cstart = jnp.sum(e_sorted[:, :, None] < experts[None, None, :], axis=1).astype(jnp.int32)
    cend = jnp.concatenate([cstart[:, 1:], jnp.full((nc, 1), tc * TOP_K, jnp.int32)], axis=1)
    count = cend - cstart
    tiles = (count + tm - 1) // tm
    tend = jnp.cumsum(tiles, axis=1)
    tstart = tend - tiles
    nt = (tc * TOP_K) // tm + n_experts
    slot = jnp.arange(nt, dtype=jnp.int32)
    total = tend[:, -1:]
    live = slot[None, :] < total
    slot_c = jnp.minimum(slot[None, :], total - 1)
    e_of = jnp.sum(slot_c[:, :, None] >= tend[:, None, :], axis=2).astype(jnp.int32)
    pick = lambda tbl: jnp.sum(jnp.where(e_of[:, :, None] == experts[None, None, :], tbl[:, None, :], 0), axis=2)
    jj = slot_c - pick(tstart)
    p0 = pick(cstart) + jj * tm
    n = jnp.where(live, jnp.clip(pick(count) - jj * tm, 0, tm), 0)
    p0 = p0 + (jnp.arange(nc, dtype=jnp.int32) * (tc * TOP_K))[:, None]
    first = jnp.minimum(p0, nc * tc * TOP_K - tm)
    flat = lambda a: a.reshape(-1).astype(jnp.int32)
    return flat(e_of), flat(first), flat(p0 - first), flat(n), flat(tok)


def _residual_kernel(x_ref, moe_ref, g_ref, o_ref):
    o_ref[0] = x_ref[0] + g_ref[0] * moe_ref[0]


def _residual(x, moe, g, tm):
    b, s, d = x.shape
    tok = lambda bi, j: (bi, j, 0)
    return pl.pallas_call(
        _residual_kernel,
        grid=(b, s // tm),
        in_specs=[pl.BlockSpec((1, tm, d), tok), pl.BlockSpec((1, tm, d), tok),
                  pl.BlockSpec((1, 1, d), lambda bi, j: (bi, 0, 0))],
        out_specs=pl.BlockSpec((1, tm, d), tok),
        out_shape=jax.ShapeDtypeStruct((b, s, d), F32),
        compiler_params=_params(("parallel", "parallel")),
    )(x, moe, g)


def _split_kernel(w_ref, pe_ref, po_ref, g_ref, l_ref):
    w = w_ref[0].astype(BF16)
    g_ref[0] = jnp.dot(w, pe_ref[...], preferred_element_type=F32).astype(BF16)
    l_ref[0] = jnp.dot(w, po_ref[...], preferred_element_type=F32).astype(BF16)


def _split_glu_weights(w):
    n, d, f2 = w.shape
    cw = 2 * MXU_DIM
    k = np.arange(MXU_DIM)
    pe = np.zeros((cw, MXU_DIM), np.float32)
    po = np.zeros((cw, MXU_DIM), np.float32)
    pe[2 * k, k] = 1.0
    po[2 * k + 1, k] = 1.0
    const2 = lambda i, j: (0, 0)
    out = jax.ShapeDtypeStruct((n, d, f2 // 2), BF16)
    return pl.pallas_call(
        _split_kernel,
        grid=(n, f2 // cw),
        in_specs=[pl.BlockSpec((1, d, cw), lambda i, j: (i, 0, j)),
                  pl.BlockSpec((cw, MXU_DIM), const2), pl.BlockSpec((cw, MXU_DIM), const2)],
        out_specs=[pl.BlockSpec((1, d, MXU_DIM), lambda i, j: (i, 0, j))] * 2,
        out_shape=[out, out],
        compiler_params=_params(("parallel", "parallel")),
    )(w, jnp.asarray(pe, BF16), jnp.asarray(po, BF16))


def _pad_heads(w, n_heads):
    lead = w.shape[:-1]
    w = w.reshape(lead + (n_heads, HEAD_DIM))
    w = jnp.concatenate([w, jnp.zeros_like(w)], axis=-1)
    return w.reshape(lead + (n_heads * LANES,))


def _layer_weights(l, d, tm, slopes, w_in, b_fgate, qk_gain, w_br_fox, w_br_swa, w_br_moba, w_out,
                   w_router, b_router, w1g, w1l, b_exp1, w_exp2, b_exp2):
    splits = (W_FOX, W_FOX, W_FOX, FOX_HEADS, W_SWA, W_SWA_KV, W_SWA_KV, W_MOBA, W_MOBA, W_MOBA, 3 * d)
    pts = np.cumsum(splits)[:-1].tolist()
    qa, ka, va, fa, qb, kb, vb, qc, kc, vc, gt = jnp.split(w_in[l], pts, axis=1)
    order = np.asarray(SWA_HEAD_ORDER)
    qb = qb.reshape(d, SWA_HEADS, HEAD_DIM)[:, order].reshape(d, W_SWA)
    fa = jnp.zeros((d, LANES), F32).at[:, :FOX_HEADS].set(fa)
    w_cat = jnp.concatenate([_pad_heads(ka, FOX_HEADS), fa, qb, kb, vb, _pad_heads(kc, MOBA_HEADS), gt],
                            axis=1).astype(BF16)
    w_t = jnp.concatenate([qa, va, qc, vc], axis=1).T.astype(BF16)
    b_f = jnp.zeros((1, LANES), F32).at[0, :FOX_HEADS].set(b_fgate[l])
    g = qk_gain[l].astype(F32)
    qscale = HEAD_DIM ** -0.5
    gvec = jnp.concatenate([
        _pad_heads(jnp.tile(g[1], FOX_HEADS), FOX_HEADS), jnp.ones((LANES,), F32),
        jnp.tile(g[2], SWA_HEADS) * qscale, jnp.tile(g[3], SWA_KV_HEADS), jnp.ones((W_SWA_KV,), F32),
        _pad_heads(jnp.tile(g[5], MOBA_HEADS), MOBA_HEADS),
    ])[None, :]
    gain_t = jnp.concatenate([jnp.tile(g[0], FOX_HEADS), jnp.tile(g[4], MOBA_HEADS)]) * (qscale * LOG2E)
    gain_t = jnp.broadcast_to(gain_t[:, None], (W_FOX + W_MOBA, tm))
    heads = MXU_DIM // HEAD_DIM
    gsum = jnp.asarray(np.kron(np.eye(heads), np.ones((HEAD_DIM, HEAD_DIM))), BF16)
    tril = jnp.asarray(np.tril(np.ones((tm, tm))), F32)
    perm = np.zeros((N_PIECES * LANES, W_FOX_AUG), np.float32)
    for h in range(FOX_HEADS):
        for p in range(N_PIECES):
            perm[p * LANES + h, h * LANES + FOX_CUM_LANE + p] = 1.0
    slope_lanes = jnp.repeat(slopes[SWA_HEADS:] * LOG2E, LANES)[None, :]
    n_exp = w_router.shape[2]
    w_r = jnp.zeros((d, LANES), F32).at[:, :n_exp].set(w_router[l])
    b_r = jnp.full((1, LANES), NEG, F32).at[0, :n_exp].set(b_router[l])
    w_b = w_br_swa[l].reshape(SWA_HEADS, HEAD_DIM, d)[order].reshape(W_SWA, d)
    return dict(
        w_cat=w_cat, w_t=w_t, b_f=b_f, gvec=gvec, gain_t=gain_t, gsum=gsum, tril=tril,
        perm=jnp.asarray(perm, BF16), slope_lanes=slope_lanes,
        w_a=w_br_fox[l].astype(BF16), w_b=w_b.astype(BF16), w_c=w_br_moba[l].astype(BF16),
        w_o=w_out[l].astype(BF16), w_r=w_r, b_r=b_r,
        w1g=w1g[l], w1l=w1l[l],
        b1g=b_exp1[l, :, None, 0::2], b1l=b_exp1[l, :, None, 1::2],
        w2=w_exp2[l].astype(BF16), b2=b_exp2[l][:, None, :],
    )


def kernel(x, c, w_ada, b_ada, norm_gain, w_in, b_fgate, qk_gain, attn_sinks, w_br_fox, w_br_swa, w_br_moba, w_out, w_router, b_router, w_exp1, b_exp1, w_exp2, b_exp2):
    b, s, d = x.shape
    depth = w_ada.shape[0]
    n_experts = w_router.shape[2]
    tm = min(s, 512)
    tq_fox = min(s, 512)
    tq_swa = min(s, 256)
    tc = min(s, 2048)
    tm_moe = 256
    n_alibi = SWA_HEADS + MOBA_HEADS
    slopes = jnp.exp2(-8.0 * jnp.arange(1, n_alibi + 1, dtype=F32) / n_alibi)

    mod = _modulation(c, w_ada, b_ada)
    mod = mod.reshape(depth, b, 6, 1, d)
    w1g, w1l = _split_glu_weights(w_exp1.reshape((depth * n_experts,) + w_exp1.shape[2:]))
    w1g = w1g.reshape((depth, n_experts) + w1g.shape[1:])
    w1l = w1l.reshape((depth, n_experts) + w1l.shape[1:])
    prev = None
    for l in range(depth):
        sh1, sc1, g1, sh2, sc2, g2 = (mod[l, :, k] for k in range(6))
        wp = _layer_weights(l, d, tm, slopes, w_in, b_fgate, qk_gain, w_br_fox, w_br_swa, w_br_moba, w_out,
                            w_router, b_router, w1g, w1l, b_exp1, w_exp2, b_exp2)
        outs = _in_proj(x, prev, sh1, sc1, norm_gain[l, 0][None, :], wp, tm)
        qft, kfa, vft, qs, ks, vs, qmt, kma, vmt, kmean, gates = outs[:11]
        if prev is not None:
            x = outs[11]
        o_a = _fox_attention(qft, kfa, vft, tq_fox)
        o_b = _swa_attention(qs, ks, vs, attn_sinks[l].astype(F32), slopes, tq_swa)
        nblk = s // MOBA_BLOCK
        km4 = kmean.reshape(b, nblk, MOBA_HEADS, LANES)[..., :HEAD_DIM]
        kmat = jnp.zeros((b, MOBA_HEADS, LANES, HEAD_DIM), F32)
        kmat = kmat.at[:, :, :nblk].set(jnp.transpose(km4, (0, 2, 1, 3)))
        o_c = _moba_attention(qmt, kma, vmt, kmat, tq_fox)
        x, hx, route = _out_proj(o_a, o_b, o_c, gates, x, g1, sh2, sc2, norm_gain[l, 1][None, :], wp, tm)
        tables = _routing_tables(route, n_experts, tc, tm_moe)
        moe = _moe(hx, tables, wp, tc, tm_moe)
        prev = (moe, g2)
    return _residual(x, prev[0], prev[1], tm)
```

```python
import functools

import numpy as np
import jax
import jax.numpy as jnp
from jax import lax
from jax.experimental import pallas as pl
from jax.experimental.pallas import tpu as pltpu

HEAD_DIM = 64
FOX_HEADS = 4
SWA_HEADS = 8
SWA_KV_HEADS = 2
SWA_WINDOW = 128
MOBA_HEADS = 4
MOBA_BLOCK = 256
MOBA_TOPK = 3
TOP_K = 4
SWIGLU_ALPHA = 1.702
SWIGLU_LIMIT = 7.0
EPS = 1e-6
NEG = -1e30
LOG2E = 1.4426950408889634
MOBA_MASK_BIAS = -float(2 ** 30)

LANES = 128
MXU_DIM = 256
VMEM_LIMIT = 60 * 1024 * 1024

W_FOX = FOX_HEADS * HEAD_DIM
W_SWA = SWA_HEADS * HEAD_DIM
W_SWA_KV = SWA_KV_HEADS * HEAD_DIM
W_MOBA = MOBA_HEADS * HEAD_DIM
W_FOX_AUG = FOX_HEADS * LANES
FOX_CUM_LANE = HEAD_DIM
W_MOBA_AUG = MOBA_HEADS * LANES
MOBA_BLOCK_LANE = HEAD_DIM
MOBA_MAX_BLOCKS = 16
MOBA_ALIBI_LANE = MOBA_BLOCK_LANE + MOBA_MAX_BLOCKS
N_PIECES = 3
SWA_HEAD_ORDER = (0, 4, 1, 5, 2, 6, 3, 7)

F32 = jnp.float32
BF16 = jnp.bfloat16
HIGHEST = lax.Precision.HIGHEST
NT_DIMS = (((1,), (1,)), ((), ()))


def _params(semantics):
    return pltpu.CompilerParams(dimension_semantics=semantics, vmem_limit_bytes=VMEM_LIMIT)


def _iota(shape, dim, dtype=jnp.int32):
    return lax.broadcasted_iota(dtype, shape, dim)


def _pieces(x):
    hi = x.astype(BF16).astype(F32)
    r = x - hi
    mid = r.astype(BF16).astype(F32)
    return hi, mid, r - mid


def _mod_kernel(c_ref, w_ref, b_ref, o_ref):
    c = c_ref[...]
    s = c * jax.nn.sigmoid(c)
    o_ref[0] = jnp.dot(s, w_ref[0], precision=HIGHEST, preferred_element_type=F32) + b_ref[0]


def _modulation(c, w_ada, b_ada):
    depth, d, n = w_ada.shape
    b = c.shape[0]
    tn = d
    return pl.pallas_call(
        _mod_kernel,
        grid=(depth, n // tn),
        in_specs=[
            pl.BlockSpec((b, d), lambda l, j: (0, 0)),
            pl.BlockSpec((1, d, tn), lambda l, j: (l, 0, j)),
            pl.BlockSpec((1, 1, tn), lambda l, j: (l, 0, j)),
        ],
        out_specs=pl.BlockSpec((1, b, tn), lambda l, j: (l, 0, j)),
        out_shape=jax.ShapeDtypeStruct((depth, b, n), F32),
        compiler_params=_params(("parallel", "parallel")),
    )(c, w_ada, b_ada.reshape(depth, 1, n))


def _head_norm(z, g_ref, gain):
    w = z.shape[1]
    cw = min(w, MXU_DIM)
    outs = []
    for c0 in range(0, w, cw):
        zz = z[:, c0:c0 + cw]
        ss = jnp.dot((zz * zz).astype(BF16), g_ref[:cw, :cw], preferred_element_type=F32)
        outs.append(zz * lax.rsqrt(ss * (1.0 / HEAD_DIM) + EPS))
    zn = outs[0] if len(outs) == 1 else jnp.concatenate(outs, axis=1)
    return zn * gain


def _head_norm_t(zt, gain_t):
    rows, cols = zt.shape
    z3 = zt.reshape(rows // HEAD_DIM, HEAD_DIM, cols)
    ms = jnp.mean(z3 * z3, axis=1, keepdims=True)
    return (z3 * lax.rsqrt(ms + EPS)).reshape(rows, cols) * gain_t


def _in_kernel(*refs, has_prev, tm, d):
    if has_prev:
        x_ref, moe_ref, g2_ref = refs[:3]
        refs = refs[3:]
    else:
        x_ref = refs[0]
        refs = refs[1:]
    (sh_ref, sc_ref, gain_ref, w_ref, wt_ref, bf_ref, gvec_ref, gt_ref, g_ref, tril_ref, perm_ref, slope_ref,
     qft_ref, kfa_ref, vft_ref, qs_ref, ks_ref, vs_ref, qmt_ref, kma_ref, vmt_ref,
     kmean_ref, gates_ref) = refs[:23]
    refs = refs[23:]
    if has_prev:
        xn_ref, carry_ref = refs
    else:
        (carry_ref,) = refs
    j = pl.program_id(1)

    x = x_ref[0]
    if has_prev:
        x = x + g2_ref[0] * moe_ref[0]
        xn_ref[0] = x
    ms = jnp.mean(x * x, axis=-1, keepdims=True)
    h = x * lax.rsqrt(ms + EPS) * gain_ref[...]
    h = h * (1.0 + sc_ref[0]) + sh_ref[0]
    hb = h.astype(BF16)

    def proj(c0, c1):
        return jnp.dot(hb, w_ref[:, c0:c1], preferred_element_type=F32)

    def gain(c0, c1):
        return gvec_ref[:, c0:c1]

    zt = lax.dot_general(wt_ref[...], hb, NT_DIMS, preferred_element_type=F32)
    qft_ref[0] = _head_norm_t(zt[:W_FOX], gt_ref[:W_FOX]).astype(BF16)
    vft_ref[0] = zt[W_FOX:2 * W_FOX].astype(BF16)
    o = 2 * W_FOX
    qmt_ref[0] = _head_norm_t(zt[o:o + W_MOBA], gt_ref[W_FOX:W_FOX + W_MOBA])
    vmt_ref[0] = zt[o + W_MOBA:o + 2 * W_MOBA].astype(BF16)

    @pl.when(j == 0)
    def _():
        carry_ref[...] = jnp.zeros_like(carry_ref)

    o = 0
    z = proj(o, o + W_FOX_AUG + LANES)
    kf = _head_norm(z[:, :W_FOX_AUG], g_ref, gain(o, o + W_FOX_AUG))
    fa = z[:, W_FOX_AUG:] + bf_ref[...]
    logf = -(jnp.maximum(-fa, 0.0) + jnp.log(1.0 + jnp.exp(-jnp.abs(fa))))
    cum = jnp.dot(tril_ref[...], logf, precision=HIGHEST, preferred_element_type=F32) + carry_ref[0:1, :]
    carry_ref[...] = jnp.broadcast_to(cum[tm - 1:tm, :], carry_ref.shape)
    parts = jnp.concatenate(_pieces(cum * (-LOG2E)), axis=1).astype(BF16)
    kfa_ref[0] = (kf + jnp.dot(parts, perm_ref[...], preferred_element_type=F32)).astype(BF16)
    o += W_FOX_AUG + LANES
    z = proj(o, o + W_SWA + 2 * W_SWA_KV)
    qs_ref[0] = _head_norm(z[:, :W_SWA], g_ref, gain(o, o + W_SWA)).astype(BF16)
    ks_ref[0] = _head_norm(z[:, W_SWA:W_SWA + W_SWA_KV], g_ref,
                           gain(o + W_SWA, o + W_SWA + W_SWA_KV)).astype(BF16)
    vs_ref[0] = z[:, W_SWA + W_SWA_KV:].astype(BF16)
    o += W_SWA + 2 * W_SWA_KV
    z = proj(o, o + W_MOBA_AUG)
    kn = _head_norm(z, g_ref, gain(o, o + W_MOBA_AUG))
    nb = tm // MOBA_BLOCK
    kmean_ref[0] = jnp.mean(kn.reshape(nb, MOBA_BLOCK, W_MOBA_AUG), axis=1).reshape(nb, 1, W_MOBA_AUG)
    lane = _iota((tm, W_MOBA_AUG), 1) % LANES
    pos = j * tm + _iota((tm, W_MOBA_AUG), 0)
    a_hi, a_mid, a_lo = _pieces(slope_ref[...] * pos.astype(F32))
    ka = jnp.where(lane == MOBA_BLOCK_LANE + pos // MOBA_BLOCK, 1.0, kn)
    ka = jnp.where(lane == MOBA_ALIBI_LANE, a_hi, ka)
    ka = jnp.where(lane == MOBA_ALIBI_LANE + 1, a_mid, ka)
    ka = jnp.where(lane == MOBA_ALIBI_LANE + 2, a_lo, ka)
    kma_ref[0] = ka.astype(BF16)
    o += W_MOBA_AUG
    for br in range(3):
        zg = proj(o + br * d, o + (br + 1) * d)
        gates_ref[0, :, br * d:(br + 1) * d] = jax.nn.sigmoid(zg).astype(BF16)


def _in_proj(x, prev, sh, sc, gain, wp, tm):
    b, s, d = x.shape
    has_prev = prev is not None
    tok = lambda bi, j: (bi, j, 0)
    tok_t = lambda bi, j: (bi, 0, j)
    row = lambda bi, j: (bi, 0, 0)
    const2 = lambda bi, j: (0, 0)
    full2 = lambda a: pl.BlockSpec(a.shape, const2)
    in_specs = [pl.BlockSpec((1, tm, d), tok)]
    args = [x]
    if has_prev:
        in_specs += [pl.BlockSpec((1, tm, d), tok), pl.BlockSpec((1, 1, d), row)]
        args += [prev[0], prev[1]]
    consts = [gain, wp["w_cat"], wp["w_t"], wp["b_f"], wp["gvec"], wp["gain_t"], wp["gsum"], wp["tril"],
              wp["perm"], wp["slope_lanes"]]
    in_specs += [pl.BlockSpec((1, 1, d), row), pl.BlockSpec((1, 1, d), row)] + [full2(a) for a in consts]
    args += [sh, sc] + consts
    nblk = s // MOBA_BLOCK
    out_shape = [
        jax.ShapeDtypeStruct((b, W_FOX, s), BF16), jax.ShapeDtypeStruct((b, s, W_FOX_AUG), BF16),
        jax.ShapeDtypeStruct((b, W_FOX, s), BF16),
        jax.ShapeDtypeStruct((b, s, W_SWA), BF16), jax.ShapeDtypeStruct((b, s, W_SWA_KV), BF16),
        jax.ShapeDtypeStruct((b, s, W_SWA_KV), BF16),
        jax.ShapeDtypeStruct((b, W_MOBA, s), F32), jax.ShapeDtypeStruct((b, s, W_MOBA_AUG), BF16),
        jax.ShapeDtypeStruct((b, W_MOBA, s), BF16),
        jax.ShapeDtypeStruct((b, nblk, 1, W_MOBA_AUG), F32),
        jax.ShapeDtypeStruct((b, s, 3 * d), BF16),
    ]
    out_specs = [
        pl.BlockSpec((1, W_FOX, tm), tok_t), pl.BlockSpec((1, tm, W_FOX_AUG), tok),
        pl.BlockSpec((1, W_FOX, tm), tok_t),
        pl.BlockSpec((1, tm, W_SWA), tok), pl.BlockSpec((1, tm, W_SWA_KV), tok),
        pl.BlockSpec((1, tm, W_SWA_KV), tok),
        pl.BlockSpec((1, W_MOBA, tm), tok_t), pl.BlockSpec((1, tm, W_MOBA_AUG), tok),
        pl.BlockSpec((1, W_MOBA, tm), tok_t),
        pl.BlockSpec((1, tm // MOBA_BLOCK, 1, W_MOBA_AUG), lambda bi, j: (bi, j, 0, 0)),
        pl.BlockSpec((1, tm, 3 * d), tok),
    ]
    if has_prev:
        out_shape.append(jax.ShapeDtypeStruct((b, s, d), F32))
        out_specs.append(pl.BlockSpec((1, tm, d), tok))
    return pl.pallas_call(
        functools.partial(_in_kernel, has_prev=has_prev, tm=tm, d=d),
        grid=(b, s // tm),
        in_specs=in_specs,
        out_specs=out_specs,
        out_shape=out_shape,
        scratch_shapes=[pltpu.VMEM((8, LANES), F32)],
        compiler_params=_params(("parallel", "arbitrary")),
    )(*args)


def _flash_t(i, tq, n_heads, scores, values, o_ref, m_ref, l_ref, acc_ref):
    m_ref[...] = jnp.full_like(m_ref, NEG)
    l_ref[...] = jnp.zeros_like(l_ref)
    acc_ref[...] = jnp.zeros_like(acc_ref)
    causal = _iota((tq, tq), 0) <= _iota((tq, tq), 1)

    def block(j, diagonal):
        for h in range(n_heads):
            s = scores(j, h)
            if diagonal:
                s = jnp.where(causal, s, NEG)
            m_old = m_ref[h]
            m_new = jnp.maximum(m_old, jnp.max(s, axis=0, keepdims=True))
            alpha = jnp.exp2(m_old - m_new)
            p = jnp.exp2(s - m_new)
            l_ref[h] = alpha * l_ref[h] + jnp.sum(p, axis=0, keepdims=True)
            m_ref[h] = m_new
            pv = jnp.dot(values(j, h), p.astype(BF16), preferred_element_type=F32)
            rows = slice(h * HEAD_DIM, (h + 1) * HEAD_DIM)
            acc_ref[rows, :] = acc_ref[rows, :] * alpha + pv

    def body(j, carry):
        block(j, False)
        return carry

    lax.fori_loop(0, i, body, 0)
    block(i, True)
    for h in range(n_heads):
        rows = slice(h * HEAD_DIM, (h + 1) * HEAD_DIM)
        acc_ref[rows, :] = acc_ref[rows, :] * (1.0 / l_ref[h])
    o_ref[0] = jnp.transpose(acc_ref[...]).astype(BF16)


def _fox_kernel(qt_ref, k_ref, vt_ref, o_ref, m_ref, l_ref, acc_ref, *, tq):
    i = pl.program_id(1)
    tail = jnp.where(_iota((LANES - HEAD_DIM, tq), 0) < N_PIECES, 1.0, 0.0).astype(BF16)
    qa = [jnp.concatenate([qt_ref[0, h * HEAD_DIM:(h + 1) * HEAD_DIM, :], tail], axis=0)
          for h in range(FOX_HEADS)]

    def scores(j, h):
        start = pl.multiple_of(j * tq, tq)
        return jnp.dot(k_ref[0, pl.ds(start, tq), h * LANES:(h + 1) * LANES], qa[h],
                       preferred_element_type=F32)

    def values(j, h):
        start = pl.multiple_of(j * tq, tq)
        return vt_ref[0, h * HEAD_DIM:(h + 1) * HEAD_DIM, pl.ds(start, tq)]

    _flash_t(i, tq, FOX_HEADS, scores, values, o_ref, m_ref, l_ref, acc_ref)


def _fox_attention(qt, k_aug, vt, tq):
    b, w, s = qt.shape
    return pl.pallas_call(
        functools.partial(_fox_kernel, tq=tq),
        grid=(b, s // tq),
        in_specs=[
            pl.BlockSpec((1, w, tq), lambda bi, i: (bi, 0, i)),
            pl.BlockSpec((1, s, k_aug.shape[2]), lambda bi, i: (bi, 0, 0)),
            pl.BlockSpec((1, w, s), lambda bi, i: (bi, 0, 0)),
        ],
        out_specs=pl.BlockSpec((1, tq, w), lambda bi, i: (bi, i, 0)),
        out_shape=jax.ShapeDtypeStruct((b, s, w), BF16),
        scratch_shapes=[pltpu.VMEM((FOX_HEADS, 1, tq), F32), pltpu.VMEM((FOX_HEADS, 1, tq), F32),
                        pltpu.VMEM((w, tq), F32)],
        compiler_params=_params(("parallel", "parallel")),
    )(qt, k_aug, vt)


def _moba_kernel(qt_ref, k_ref, vt_ref, kmean_ref, o_ref, m_ref, l_ref, acc_ref, *, tq):
    i = pl.program_id(1)
    row = _iota((LANES, tq), 0)
    rowf = row.astype(F32)
    own = (i * tq + _iota((LANES, tq), 1)) // MOBA_BLOCK
    past = row < own
    tail = jnp.where(_iota((LANES - MOBA_ALIBI_LANE, tq), 0) < N_PIECES, 1.0, 0.0)
    qa = []
    for h in range(MOBA_HEADS):
        qh = qt_ref[0, h * HEAD_DIM:(h + 1) * HEAD_DIM, :]
        gate = jnp.dot(kmean_ref[0, h], qh, precision=HIGHEST, preferred_element_type=F32)
        g = jnp.where(past, gate, NEG)
        chosen = jnp.zeros((LANES, tq), jnp.bool_)
        for _ in range(MOBA_TOPK):
            mx = jnp.max(g, axis=0, keepdims=True)
            first = jnp.min(jnp.where(g == mx, rowf, float(LANES)), axis=0, keepdims=True)
            pick = (rowf == first) & (mx > 0.5 * NEG)
            chosen = chosen | pick
            g = jnp.where(pick, NEG, g)
        bias = jnp.where(past & jnp.logical_not(chosen), MOBA_MASK_BIAS, 0.0)
        qa.append(jnp.concatenate([qh, bias[:MOBA_MAX_BLOCKS], tail], axis=0).astype(BF16))

    def scores(j, h):
        start = pl.multiple_of(j * tq, tq)
        return jnp.dot(k_ref[0, pl.ds(start, tq), h * LANES:(h + 1) * LANES], qa[h],
                       preferred_element_type=F32)

    def values(j, h):
        start = pl.multiple_of(j * tq, tq)
        return vt_ref[0, h * HEAD_DIM:(h + 1) * HEAD_DIM, pl.ds(start, tq)]

    _flash_t(i, tq, MOBA_HEADS, scores, values, o_ref, m_ref, l_ref, acc_ref)


def _moba_attention(qt, k_aug, vt, kmean_mat, tq):
    b, w, s = qt.shape
    assert s // MOBA_BLOCK <= MOBA_MAX_BLOCKS and tq % MOBA_BLOCK == 0
    return pl.pallas_call(
        functools.partial(_moba_kernel, tq=tq),
        grid=(b, s // tq),
        in_specs=[
            pl.BlockSpec((1, w, tq), lambda bi, i: (bi, 0, i)),
            pl.BlockSpec((1, s, k_aug.shape[2]), lambda bi, i: (bi, 0, 0)),
            pl.BlockSpec((1, w, s), lambda bi, i: (bi, 0, 0)),
            pl.BlockSpec((1, MOBA_HEADS, LANES, HEAD_DIM), lambda bi, i: (bi, 0, 0, 0)),
        ],
        out_specs=pl.BlockSpec((1, tq, w), lambda bi, i: (bi, i, 0)),
        out_shape=jax.ShapeDtypeStruct((b, s, w), BF16),
        scratch_shapes=[pltpu.VMEM((MOBA_HEADS, 1, tq), F32), pltpu.VMEM((MOBA_HEADS, 1, tq), F32),
                        pltpu.VMEM((w, tq), F32)],
        compiler_params=_params(("parallel", "parallel")),
    )(qt, k_aug, vt, kmean_mat)


def _swa_kernel(sinks_ref, slopes_ref, q_ref, k_ref, v_ref, o_ref, *, tq):
    i = pl.program_id(1)
    tk = tq + SWA_WINDOW
    start = pl.multiple_of(jnp.maximum(i * tq - SWA_WINDOW, 0), SWA_WINDOW)
    kw = k_ref[0, pl.ds(start, tk), :]
    vw = v_ref[0, pl.ds(start, tk), :]
    dist = ((i * tq + _iota((tq, tk), 0)) - (start + _iota((tq, tk), 1))).astype(F32)
    valid = (dist >= 0.0) & (dist < float(SWA_WINDOW))
    low_lanes = _iota((1, LANES), 1) < HEAD_DIM
    pairs = SWA_HEADS // SWA_KV_HEADS
    for p in range(pairs):
        qg = q_ref[0, :, p * LANES:(p + 1) * LANES]
        outs = []
        for half in range(SWA_KV_HEADS):
            head = p + pairs * half
            mine = low_lanes if half == 0 else jnp.logical_not(low_lanes)
            qm = jnp.where(mine, qg, jnp.zeros_like(qg))
            s = lax.dot_general(qm, kw, NT_DIMS, preferred_element_type=F32)
            s = jnp.where(valid, s - slopes_ref[head] * dist, NEG)
            sink = sinks_ref[head]
            m = jnp.maximum(jnp.max(s, axis=-1, keepdims=True), sink)
            pr = jnp.exp(s - m)
            den = jnp.sum(pr, axis=-1, keepdims=True) + jnp.exp(sink - m)
            outs.append(jnp.dot(pr.astype(BF16), vw, preferred_element_type=F32) / den)
        o_ref[0, :, p * LANES:(p + 1) * LANES] = jnp.where(low_lanes, outs[0], outs[1]).astype(BF16)


def _swa_attention(q, k, v, sinks, slopes, tq):
    b, s, w = q.shape
    wk = k.shape[2]
    smem = pl.BlockSpec(memory_space=pltpu.SMEM)
    return pl.pallas_call(
        functools.partial(_swa_kernel, tq=tq),
        grid=(b, s // tq),
        in_specs=[
            smem, smem,
            pl.BlockSpec((1, tq, w), lambda bi, i: (bi, i, 0)),
            pl.BlockSpec((1, s, wk), lambda bi, i: (bi, 0, 0)),
            pl.BlockSpec((1, s, wk), lambda bi, i: (bi, 0, 0)),
        ],
        out_specs=pl.BlockSpec((1, tq, w), lambda bi, i: (bi, i, 0)),
        out_shape=jax.ShapeDtypeStruct((b, s, w), BF16),
        compiler_params=_params(("parallel", "parallel")),
    )(sinks, slopes, q, k, v)


def _out_kernel(oa_ref, ob_ref, oc_ref, gates_ref, x_ref, g1_ref, sh_ref, sc_ref, gain_ref,
                wa_ref, wb_ref, wc_ref, wo_ref, wr_ref, br_ref,
                x1_ref, hx_ref, route_ref, *, d):
    ya = jnp.dot(oa_ref[0], wa_ref[...], preferred_element_type=F32)
    yb = jnp.dot(ob_ref[0], wb_ref[...], preferred_element_type=F32)
    yc = jnp.dot(oc_ref[0], wc_ref[...], preferred_element_type=F32)
    g = gates_ref[0]
    mix = (g[:, :d].astype(F32) * ya + g[:, d:2 * d].astype(F32) * yb + g[:, 2 * d:].astype(F32) * yc)
    y = jnp.dot(mix.astype(BF16), wo_ref[...], preferred_element_type=F32)
    x1 = x_ref[0] + g1_ref[0] * y
    x1_ref[0] = x1
    ms = jnp.mean(x1 * x1, axis=-1, keepdims=True)
    h = x1 * lax.rsqrt(ms + EPS) * gain_ref[...]
    h = h * (1.0 + sc_ref[0]) + sh_ref[0]
    hx_ref[0, :, :d] = h
    logits = jnp.dot(h, wr_ref[...], precision=HIGHEST, preferred_element_type=F32) + br_ref[...]
    tm = logits.shape[0]
    lanef = _iota((tm, LANES), 1).astype(F32)
    vals, idxs = [], []
    for _ in range(TOP_K):
        mx = jnp.max(logits, axis=-1, keepdims=True)
        first = jnp.min(jnp.where(logits == mx, lanef, float(LANES)), axis=-1, keepdims=True)
        vals.append(mx)
        idxs.append(first)
        logits = jnp.where(lanef == first, -jnp.inf, logits)
    exps = [jnp.exp(v - vals[0]) for v in vals]
    den = exps[0]
    for e in exps[1:]:
        den = den + e
    gates = jnp.zeros((tm, LANES), F32)
    info = jnp.zeros((tm, LANES), F32)
    for k in range(TOP_K):
        wk = exps[k] / den
        gates = jnp.where(lanef == idxs[k], wk, gates)
        info = jnp.where(lanef == float(k), idxs[k], info)
    hx_ref[0, :, d:] = gates
    route_ref[0] = info


def _out_proj(oa, ob, oc, gates, x, g1, sh, sc, gain, wp, tm):
    b, s, d = x.shape
    tok = lambda bi, j: (bi, j, 0)
    row = lambda bi, j: (bi, 0, 0)
    const2 = lambda bi, j: (0, 0)
    return pl.pallas_call(
        functools.partial(_out_kernel, d=d),
        grid=(b, s // tm),
        in_specs=[
            pl.BlockSpec((1, tm, W_FOX), tok), pl.BlockSpec((1, tm, W_SWA), tok),
            pl.BlockSpec((1, tm, W_MOBA), tok), pl.BlockSpec((1, tm, 3 * d), tok),
            pl.BlockSpec((1, tm, d), tok),
            pl.BlockSpec((1, 1, d), row), pl.BlockSpec((1, 1, d), row), pl.BlockSpec((1, 1, d), row),
            pl.BlockSpec((1, d), const2),
            pl.BlockSpec((W_FOX, d), const2), pl.BlockSpec((W_SWA, d), const2),
            pl.BlockSpec((W_MOBA, d), const2), pl.BlockSpec((d, d), const2),
            pl.BlockSpec((d, LANES), const2), pl.BlockSpec((1, LANES), const2),
        ],
        out_specs=[pl.BlockSpec((1, tm, d), tok), pl.BlockSpec((1, tm, d + LANES), tok),
                   pl.BlockSpec((1, tm, LANES), tok)],
        out_shape=[jax.ShapeDtypeStruct((b, s, d), F32), jax.ShapeDtypeStruct((b, s, d + LANES), F32),
                   jax.ShapeDtypeStruct((b, s, LANES), F32)],
        compiler_params=_params(("parallel", "parallel")),
    )(oa, ob, oc, gates, x, g1, sh, sc, gain, wp["w_a"], wp["w_b"], wp["w_c"], wp["w_o"],
      wp["w_r"], wp["b_r"])


def _moe_kernel(tile_e_ref, tile_p0_ref, tile_off_ref, tile_n_ref, tok_ref,
                hx_ref, w1g_ref, w1l_ref, b1g_ref, b1l_ref, w2_ref, b2_ref,
                out_ref, xg_ref, y_ref, *, nt, tm, d):
    c = pl.program_id(0)
    i = pl.program_id(1)
    t = c * nt + i
    e = tile_e_ref[t]
    p0 = tile_p0_ref[t]
    off = tile_off_ref[t]
    n = tile_n_ref[t]

    @pl.when(i == 0)
    def _():
        out_ref[...] = jnp.zeros_like(out_ref)

    @pl.when(n > 0)
    def _():
        def gather(g, carry):
            for u in range(8):
                tk = tok_ref[p0 + g * 8 + u]
                xg_ref[g, u:u + 1, :] = hx_ref[0, pl.ds(tk, 1), :]
            return carry

        lax.fori_loop(0, tm // 8, gather, 0)
        xg = xg_ref[...].reshape(tm, hx_ref.shape[2])
        xb = xg[:, :d].astype(BF16)
        lane = _iota((tm, LANES), 1)
        wcol = jnp.sum(jnp.where(lane == e, xg[:, d:], 0.0), axis=-1, keepdims=True)
        ug = jnp.dot(xb, w1g_ref[0], preferred_element_type=F32) + b1g_ref[0]
        ul = jnp.dot(xb, w1l_ref[0], preferred_element_type=F32) + b1l_ref[0]
        ug = jnp.minimum(ug, SWIGLU_LIMIT)
        ul = jnp.clip(ul, -SWIGLU_LIMIT, SWIGLU_LIMIT)
        act = ug * jax.nn.sigmoid(SWIGLU_ALPHA * ug) * (ul + 1.0)
        y = jnp.dot(act.astype(BF16), w2_ref[0], preferred_element_type=F32) + b2_ref[0]
        y_ref[...] = (y * wcol).reshape(y_ref.shape)
        end = off + n

        def scatter(g, carry):
            base = g * 8
            whole = (base >= off) & (base + 8 <= end)

            @pl.when(whole)
            def _():
                for u0 in range(0, 8, 4):
                    toks = [tok_ref[p0 + base + u0 + u] for u in range(4)]
                    rows = [out_ref[0, pl.ds(tk, 1), :] for tk in toks]
                    for u in range(4):
                        out_ref[0, pl.ds(toks[u], 1), :] = rows[u] + y_ref[g, u0 + u:u0 + u + 1, :]

            @pl.when(jnp.logical_not(whole))
            def _():
                for u in range(8):
                    @pl.when((base + u >= off) & (base + u < end))
                    def _():
                        tk = tok_ref[p0 + base + u]
                        out_ref[0, pl.ds(tk, 1), :] = out_ref[0, pl.ds(tk, 1), :] + y_ref[g, u:u + 1, :]

            return carry

        lax.fori_loop(off // 8, (end + 7) // 8, scatter, 0)


def _moe(hx, tables, wp, tc, tm):
    b, s, dx = hx.shape
    d = dx - LANES
    nc = (b * s) // tc
    tile_e = tables[0]
    nt = tile_e.shape[0] // nc
    f = wp["w1g"].shape[2]
    hx = hx.reshape(nc, tc, dx)
    chunk = lambda c, i, *prefetch: (c, 0, 0)
    expert = lambda c, i, te, *prefetch: (te[c * nt + i], 0, 0)
    out = pl.pallas_call(
        functools.partial(_moe_kernel, nt=nt, tm=tm, d=d),
        grid_spec=pltpu.PrefetchScalarGridSpec(
            num_scalar_prefetch=len(tables),
            grid=(nc, nt),
            in_specs=[
                pl.BlockSpec((1, tc, dx), chunk, pipeline_mode=pl.Buffered(1)),
                pl.BlockSpec((1, d, f), expert), pl.BlockSpec((1, d, f), expert),
                pl.BlockSpec((1, 1, f), expert), pl.BlockSpec((1, 1, f), expert),
                pl.BlockSpec((1, f, d), expert), pl.BlockSpec((1, 1, d), expert),
            ],
            out_specs=pl.BlockSpec((1, tc, d), chunk, pipeline_mode=pl.Buffered(1)),
            scratch_shapes=[pltpu.VMEM((tm // 8, 8, dx), F32), pltpu.VMEM((tm // 8, 8, d), F32)],
        ),
        out_shape=jax.ShapeDtypeStruct((nc, tc, d), F32),
        compiler_params=_params(("parallel", "arbitrary")),
    )(*tables, hx, wp["w1g"], wp["w1l"], wp["b1g"], wp["b1l"], wp["w2"], wp["b2"])
    return out.reshape(b, s, d)


def _routing_tables(route, n_experts, tc, tm):
    b, s, _ = route.shape
    t = b * s
    nc = t // tc
    idx = route[:, :, :TOP_K].astype(jnp.int32).reshape(nc, tc, TOP_K)
    key = idx * tc + jnp.arange(tc, dtype=jnp.int32)[None, :, None]
    key = jnp.sort(key.reshape(nc, tc * TOP_K), axis=1)
    tok = key % tc
    e_sorted = key // tc
    experts = jnp.arange(n_experts, dtype=jnp.int32)
    cstart = jnp.sum(e_sorted[:, :, None] < experts[None, None, :], axis=1).astype(jnp.int32)
    cend = jnp.concatenate([cstart[:, 1:], jnp.full((nc, 1), tc * TOP_K, jnp.int32)], axis=1)
    count = cend - cstart
    tiles = (count + tm - 1) // tm
    tend = jnp.cumsum(tiles, axis=1)
    tstart = tend - tiles
    nt = (tc * TOP_K) // tm + n_experts
    slot = jnp.arange(nt, dtype=jnp.int32)
    total = tend[:, -1:]
    live = slot[None, :] < total
    slot_c = jnp.minimum(slot[None, :], total - 1)
    e_of = jnp.sum(slot_c[:, :, None] >= tend[:, None, :], axis=2).astype(jnp.int32)
    pick = lambda tbl: jnp.sum(jnp.where(e_of[:, :, None] == experts[None, None, :], tbl[:, None, :], 0), axis=2)
    jj = slot_c - pick(tstart)
    p0 = pick(cstart) + jj * tm
    n = jnp.where(live, jnp.clip(pick(count) - jj * tm, 0, tm), 0)
    p0 = p0 + (jnp.arange(nc, dtype=jnp.int32) * (tc * TOP_K))[:, None]
    first = jnp.minimum(p0, nc * tc * TOP_K - tm)
    flat = lambda a: a.reshape(-1).astype(jnp.int32)
    return flat(e_of), flat(first), flat(p0 - first), flat(n), flat(tok)


def _residual_kernel(x_ref, moe_ref, g_ref, o_ref):
    o_ref[0] = x_ref[0] + g_ref[0] * moe_ref[0]


def _residual(x, moe, g, tm):
    b, s, d = x.shape
    tok = lambda bi, j: (bi, j, 0)
    return pl.pallas_call(
        _residual_kernel,
        grid=(b, s // tm),
        in_specs=[pl.BlockSpec((1, tm, d), tok), pl.BlockSpec((1, tm, d), tok),
                  pl.BlockSpec((1, 1, d), lambda bi, j: (bi, 0, 0))],
        out_specs=pl.BlockSpec((1, tm, d), tok),
        out_shape=jax.ShapeDtypeStruct((b, s, d), F32),
        compiler_params=_params(("parallel", "parallel")),
    )(x, moe, g)


def _split_kernel(w_ref, pe_ref, po_ref, g_ref, l_ref):
    w = w_ref[0].astype(BF16)
    g_ref[0] = jnp.dot(w, pe_ref[...], preferred_element_type=F32).astype(BF16)
    l_ref[0] = jnp.dot(w, po_ref[...], preferred_element_type=F32).astype(BF16)


def _split_glu_weights(w):
    n, d, f2 = w.shape
    cw = 2 * MXU_DIM
    k = np.arange(MXU_DIM)
    pe = np.zeros((cw, MXU_DIM), np.float32)
    po = np.zeros((cw, MXU_DIM), np.float32)
    pe[2 * k, k] = 1.0
    po[2 * k + 1, k] = 1.0
    const2 = lambda i, j: (0, 0)
    out = jax.ShapeDtypeStruct((n, d, f2 // 2), BF16)
    return pl.pallas_call(
        _split_kernel,
        grid=(n, f2 // cw),
        in_specs=[pl.BlockSpec((1, d, cw), lambda i, j: (i, 0, j)),
                  pl.BlockSpec((cw, MXU_DIM), const2), pl.BlockSpec((cw, MXU_DIM), const2)],
        out_specs=[pl.BlockSpec((1, d, MXU_DIM), lambda i, j: (i, 0, j))] * 2,
        out_shape=[out, out],
        compiler_params=_params(("parallel", "parallel")),
    )(w, jnp.asarray(pe, BF16), jnp.asarray(po, BF16))


def _pad_heads(w, n_heads):
    lead = w.shape[:-1]
    w = w.reshape(lead + (n_heads, HEAD_DIM))
    w = jnp.concatenate([w, jnp.zeros_like(w)], axis=-1)
    return w.reshape(lead + (n_heads * LANES,))


def _layer_weights(l, d, tm, slopes, w_in, b_fgate, qk_gain, w_br_fox, w_br_swa, w_br_moba, w_out,
                   w_router, b_router, w1g, w1l, b_exp1, w_exp2, b_exp2):
    splits = (W_FOX, W_FOX, W_FOX, FOX_HEADS, W_SWA, W_SWA_KV, W_SWA_KV, W_MOBA, W_MOBA, W_MOBA, 3 * d)
    pts = np.cumsum(splits)[:-1].tolist()
    qa, ka, va, fa, qb, kb, vb, qc, kc, vc, gt = jnp.split(w_in[l], pts, axis=1)
    order = np.asarray(SWA_HEAD_ORDER)
    qb = qb.reshape(d, SWA_HEADS, HEAD_DIM)[:, order].reshape(d, W_SWA)
    fa = jnp.zeros((d, LANES), F32).at[:, :FOX_HEADS].set(fa)
    w_cat = jnp.concatenate([_pad_heads(ka, FOX_HEADS), fa, qb, kb, vb, _pad_heads(kc, MOBA_HEADS), gt],
                            axis=1).astype(BF16)
    w_t = jnp.concatenate([qa, va, qc, vc], axis=1).T.astype(BF16)
    b_f = jnp.zeros((1, LANES), F32).at[0, :FOX_HEADS].set(b_fgate[l])
    g = qk_gain[l].astype(F32)
    qscale = HEAD_DIM ** -0.5
    gvec = jnp.concatenate([
        _pad_heads(jnp.tile(g[1], FOX_HEADS), FOX_HEADS), jnp.ones((LANES,), F32),
        jnp.tile(g[2], SWA_HEADS) * qscale, jnp.tile(g[3], SWA_KV_HEADS), jnp.ones((W_SWA_KV,), F32),
        _pad_heads(jnp.tile(g[5], MOBA_HEADS), MOBA_HEADS),
    ])[None, :]
    gain_t = jnp.concatenate([jnp.tile(g[0], FOX_HEADS), jnp.tile(g[4], MOBA_HEADS)]) * (qscale * LOG2E)
    gain_t = jnp.broadcast_to(gain_t[:, None], (W_FOX + W_MOBA, tm))
    heads = MXU_DIM // HEAD_DIM
    gsum = jnp.asarray(np.kron(np.eye(heads), np.ones((HEAD_DIM, HEAD_DIM))), BF16)
    tril = jnp.asarray(np.tril(np.ones((tm, tm))), F32)
    perm = np.zeros((N_PIECES * LANES, W_FOX_AUG), np.float32)
    for h in range(FOX_HEADS):
        for p in range(N_PIECES):
            perm[p * LANES + h, h * LANES + FOX_CUM_LANE + p] = 1.0
    slope_lanes = jnp.repeat(slopes[SWA_HEADS:] * LOG2E, LANES)[None, :]
    n_exp = w_router.shape[2]
    w_r = jnp.zeros((d, LANES), F32).at[:, :n_exp].set(w_router[l])
    b_r = jnp.full((1, LANES), NEG, F32).at[0, :n_exp].set(b_router[l])
    w_b = w_br_swa[l].reshape(SWA_HEADS, HEAD_DIM, d)[order].reshape(W_SWA, d)
    return dict(
        w_cat=w_cat, w_t=w_t, b_f=b_f, gvec=gvec, gain_t=gain_t, gsum=gsum, tril=tril,
        perm=jnp.asarray(perm, BF16), slope_lanes=slope_lanes,
        w_a=w_br_fox[l].astype(BF16), w_b=w_b.astype(BF16), w_c=w_br_moba[l].astype(BF16),
        w_o=w_out[l].astype(BF16), w_r=w_r, b_r=b_r,
        w1g=w1g[l], w1l=w1l[l],
        b1g=b_exp1[l, :, None, 0::2], b1l=b_exp1[l, :, None, 1::2],
        w2=w_exp2[l].astype(BF16), b2=b_exp2[l][:, None, :],
    )


def kernel(x, c, w_ada, b_ada, norm_gain, w_in, b_fgate, qk_gain, attn_sinks, w_br_fox, w_br_swa, w_br_moba, w_out, w_router, b_router, w_exp1, b_exp1, w_exp2, b_exp2):
    b, s, d = x.shape
    depth = w_ada.shape[0]
    n_experts = w_router.shape[2]
    tm = min(s, 512)
    tq_fox = min(s, 512)
    tq_swa = min(s, 256)
    tc = min(s, 4096)
    tm_moe = 256
    n_alibi = SWA_HEADS + MOBA_HEADS
    slopes = jnp.exp2(-8.0 * jnp.arange(1, n_alibi + 1, dtype=F32) / n_alibi)

    mod = _modulation(c, w_ada, b_ada)
    mod = mod.reshape(depth, b, 6, 1, d)
    w1g, w1l = _split_glu_weights(w_exp1.reshape((depth * n_experts,) + w_exp1.shape[2:]))
    w1g = w1g.reshape((depth, n_experts) + w1g.shape[1:])
    w1l = w1l.reshape((depth, n_experts) + w1l.shape[1:])
    prev = None
    for l in range(depth):
        sh1, sc1, g1, sh2, sc2, g2 = (mod[l, :, k] for k in range(6))
        wp = _layer_weights(l, d, tm, slopes, w_in, b_fgate, qk_gain, w_br_fox, w_br_swa, w_br_moba, w_out,
                            w_router, b_router, w1g, w1l, b_exp1, w_exp2, b_exp2)
        outs = _in_proj(x, prev, sh1, sc1, norm_gain[l, 0][None, :], wp, tm)
        qft, kfa, vft, qs, ks, vs, qmt, kma, vmt, kmean, gates = outs[:11]
        if prev is not None:
            x = outs[11]
        o_a = _fox_attention(qft, kfa, vft, tq_fox)
        o_b = _swa_attention(qs, ks, vs, attn_sinks[l].astype(F32), slopes, tq_swa)
        nblk = s // MOBA_BLOCK
        km4 = kmean.reshape(b, nblk, MOBA_HEADS, LANES)[..., :HEAD_DIM]
        kmat = jnp.zeros((b, MOBA_HEADS, LANES, HEAD_DIM), F32)
        kmat = kmat.at[:, :, :nblk].set(jnp.transpose(km4, (0, 2, 1, 3)))
        o_c = _moba_attention(qmt, kma, vmt, kmat, tq_fox)
        x, hx, route = _out_proj(o_a, o_b, o_c, gates, x, g1, sh2, sc2, norm_gain[l, 1][None, :], wp, tm)
        tables = _routing_tables(route, n_experts, tc, tm_moe)
        moe = _moe(hx, tables, wp, tc, tm_moe)
        prev = (moe, g2)
    return _residual(x, prev[0], prev[1], tm)
```

```python
import functools

import numpy as np
import jax
import jax.numpy as jnp
from jax import lax
from jax.experimental import pallas as pl
from jax.experimental.pallas import tpu as pltpu

HEAD_DIM = 64
FOX_HEADS = 4
SWA_HEADS = 8
SWA_KV_HEADS = 2
SWA_WINDOW = 128
MOBA_HEADS = 4
MOBA_BLOCK = 256
MOBA_TOPK = 3
TOP_K = 4
SWIGLU_ALPHA = 1.702
SWIGLU_LIMIT = 7.0
EPS = 1e-6
NEG = -1e30
LOG2E = 1.4426950408889634
MOBA_MASK_BIAS = -float(2 ** 30)

LANES = 128
MXU_DIM = 256
VMEM_LIMIT = 60 * 1024 * 1024

W_FOX = FOX_HEADS * HEAD_DIM
W_SWA = SWA_HEADS * HEAD_DIM
W_SWA_KV = SWA_KV_HEADS * HEAD_DIM
W_MOBA = MOBA_HEADS * HEAD_DIM
W_FOX_AUG = FOX_HEADS * LANES
FOX_CUM_LANE = HEAD_DIM
W_MOBA_AUG = MOBA_HEADS * LANES
MOBA_BLOCK_LANE = HEAD_DIM
MOBA_MAX_BLOCKS = 16
MOBA_ALIBI_LANE = MOBA_BLOCK_LANE + MOBA_MAX_BLOCKS
N_PIECES = 3
SWA_HEAD_ORDER = (0, 4, 1, 5, 2, 6, 3, 7)

F32 = jnp.float32
BF16 = jnp.bfloat16
HIGHEST = lax.Precision.HIGHEST
NT_DIMS = (((1,), (1,)), ((), ()))


def _params(semantics):
    return pltpu.CompilerParams(dimension_semantics=semantics, vmem_limit_bytes=VMEM_LIMIT)


def _iota(shape, dim, dtype=jnp.int32):
    return lax.broadcasted_iota(dtype, shape, dim)


def _pieces(x):
    hi = x.astype(BF16).astype(F32)
    r = x - hi
    mid = r.astype(BF16).astype(F32)
    return hi, mid, r - mid


def _mod_kernel(c_ref, w_ref, b_ref, o_ref):
    c = c_ref[...]
    s = c * jax.nn.sigmoid(c)
    o_ref[0] = jnp.dot(s, w_ref[0], precision=HIGHEST, preferred_element_type=F32) + b_ref[0]


def _modulation(c, w_ada, b_ada):
    depth, d, n = w_ada.shape
    b = c.shape[0]
    tn = d
    return pl.pallas_call(
        _mod_kernel,
        grid=(depth, n // tn),
        in_specs=[
            pl.BlockSpec((b, d), lambda l, j: (0, 0)),
            pl.BlockSpec((1, d, tn), lambda l, j: (l, 0, j)),
            pl.BlockSpec((1, 1, tn), lambda l, j: (l, 0, j)),
        ],
        out_specs=pl.BlockSpec((1, b, tn), lambda l, j: (l, 0, j)),
        out_shape=jax.ShapeDtypeStruct((depth, b, n), F32),
        compiler_params=_params(("parallel", "parallel")),
    )(c, w_ada, b_ada.reshape(depth, 1, n))


def _head_norm(z, g_ref, gain):
    w = z.shape[1]
    cw = min(w, MXU_DIM)
    outs = []
    for c0 in range(0, w, cw):
        zz = z[:, c0:c0 + cw]
        ss = jnp.dot((zz * zz).astype(BF16), g_ref[:cw, :cw], preferred_element_type=F32)
        outs.append(zz * lax.rsqrt(ss * (1.0 / HEAD_DIM) + EPS))
    zn = outs[0] if len(outs) == 1 else jnp.concatenate(outs, axis=1)
    return zn * gain


def _head_norm_t(zt, gain_t):
    rows, cols = zt.shape
    z3 = zt.reshape(rows // HEAD_DIM, HEAD_DIM, cols)
    ms = jnp.mean(z3 * z3, axis=1, keepdims=True)
    return (z3 * lax.rsqrt(ms + EPS)).reshape(rows, cols) * gain_t


def _in_kernel(*refs, has_prev, tm, d):
    if has_prev:
        x_ref, moe_ref, g2_ref = refs[:3]
        refs = refs[3:]
    else:
        x_ref = refs[0]
        refs = refs[1:]
    (sh_ref, sc_ref, gain_ref, w_ref, wt_ref, bf_ref, gvec_ref, gt_ref, g_ref, tril_ref, perm_ref, slope_ref,
     qft_ref, kfa_ref, vft_ref, qs_ref, ks_ref, vs_ref, qmt_ref, kma_ref, vmt_ref,
     kmean_ref, gates_ref) = refs[:23]
    refs = refs[23:]
    if has_prev:
        xn_ref, carry_ref = refs
    else:
        (carry_ref,) = refs
    j = pl.program_id(1)

    x = x_ref[0]
    if has_prev:
        x = x + g2_ref[0] * moe_ref[0]
        xn_ref[0] = x
    ms = jnp.mean(x * x, axis=-1, keepdims=True)
    h = x * lax.rsqrt(ms + EPS) * gain_ref[...]
    h = h * (1.0 + sc_ref[0]) + sh_ref[0]
    hb = h.astype(BF16)

    def proj(c0, c1):
        return jnp.dot(hb, w_ref[:, c0:c1], preferred_element_type=F32)

    def gain(c0, c1):
        return gvec_ref[:, c0:c1]

    zt = lax.dot_general(wt_ref[...], hb, NT_DIMS, preferred_element_type=F32)
    qft_ref[0] = _head_norm_t(zt[:W_FOX], gt_ref[:W_FOX]).astype(BF16)
    vft_ref[0] = zt[W_FOX:2 * W_FOX].astype(BF16)
    o = 2 * W_FOX
    qmt_ref[0] = _head_norm_t(zt[o:o + W_MOBA], gt_ref[W_FOX:W_FOX + W_MOBA])
    vmt_ref[0] = zt[o + W_MOBA:o + 2 * W_MOBA].astype(BF16)

    @pl.when(j == 0)
    def _():
        carry_ref[...] = jnp.zeros_like(carry_ref)

    o = 0
    z = proj(o, o + W_FOX_AUG + LANES)
    kf = _head_norm(z[:, :W_FOX_AUG], g_ref, gain(o, o + W_FOX_AUG))
    fa = z[:, W_FOX_AUG:] + bf_ref[...]
    logf = -(jnp.maximum(-fa, 0.0) + jnp.log(1.0 + jnp.exp(-jnp.abs(fa))))
    cum = jnp.dot(tril_ref[...], logf, precision=HIGHEST, preferred_element_type=F32) + carry_ref[0:1, :]
    carry_ref[...] = jnp.broadcast_to(cum[tm - 1:tm, :], carry_ref.shape)
    parts = jnp.concatenate(_pieces(cum * (-LOG2E)), axis=1).astype(BF16)
    kfa_ref[0] = (kf + jnp.dot(parts, perm_ref[...], preferred_element_type=F32)).astype(BF16)
    o += W_FOX_AUG + LANES
    z = proj(o, o + W_SWA + 2 * W_SWA_KV)
    qs_ref[0] = _head_norm(z[:, :W_SWA], g_ref, gain(o, o + W_SWA)).astype(BF16)
    ks_ref[0] = _head_norm(z[:, W_SWA:W_SWA + W_SWA_KV], g_ref,
                           gain(o + W_SWA, o + W_SWA + W_SWA_KV)).astype(BF16)
    vs_ref[0] = z[:, W_SWA + W_SWA_KV:].astype(BF16)
    o += W_SWA + 2 * W_SWA_KV
    z = proj(o, o + W_MOBA_AUG)
    kn = _head_norm(z, g_ref, gain(o, o + W_MOBA_AUG))
    nb = tm // MOBA_BLOCK
    kmean_ref[0] = jnp.mean(kn.reshape(nb, MOBA_BLOCK, W_MOBA_AUG), axis=1).reshape(nb, 1, W_MOBA_AUG)
    lane = _iota((tm, W_MOBA_AUG), 1) % LANES
    pos = j * tm + _iota((tm, W_MOBA_AUG), 0)
    a_hi, a_mid, a_lo = _pieces(slope_ref[...] * pos.astype(F32))
    ka = jnp.where(lane == MOBA_BLOCK_LANE + pos // MOBA_BLOCK, 1.0, kn)
    ka = jnp.where(lane == MOBA_ALIBI_LANE, a_hi, ka)
    ka = jnp.where(lane == MOBA_ALIBI_LANE + 1, a_mid, ka)
    ka = jnp.where(lane == MOBA_ALIBI_LANE + 2, a_lo, ka)
    kma_ref[0] = ka.astype(BF16)
    o += W_MOBA_AUG
    for br in range(3):
        zg = proj(o + br * d, o + (br + 1) * d)
        gates_ref[0, :, br * d:(br + 1) * d] = jax.nn.sigmoid(zg).astype(BF16)


def _in_proj(x, prev, sh, sc, gain, wp, tm):
    b, s, d = x.shape
    has_prev = prev is not None
    tok = lambda bi, j: (bi, j, 0)
    tok_t = lambda bi, j: (bi, 0, j)
    row = lambda bi, j: (bi, 0, 0)
    const2 = lambda bi, j: (0, 0)
    full2 = lambda a: pl.BlockSpec(a.shape, const2)
    in_specs = [pl.BlockSpec((1, tm, d), tok)]
    args = [x]
    if has_prev:
        in_specs += [pl.BlockSpec((1, tm, d), tok), pl.BlockSpec((1, 1, d), row)]
        args += [prev[0], prev[1]]
    consts = [gain, wp["w_cat"], wp["w_t"], wp["b_f"], wp["gvec"], wp["gain_t"], wp["gsum"], wp["tril"],
              wp["perm"], wp["slope_lanes"]]
    in_specs += [pl.BlockSpec((1, 1, d), row), pl.BlockSpec((1, 1, d), row)] + [full2(a) for a in consts]
    args += [sh, sc] + consts
    nblk = s // MOBA_BLOCK
    out_shape = [
        jax.ShapeDtypeStruct((b, W_FOX, s), BF16), jax.ShapeDtypeStruct((b, s, W_FOX_AUG), BF16),
        jax.ShapeDtypeStruct((b, W_FOX, s), BF16),
        jax.ShapeDtypeStruct((b, s, W_SWA), BF16), jax.ShapeDtypeStruct((b, s, W_SWA_KV), BF16),
        jax.ShapeDtypeStruct((b, s, W_SWA_KV), BF16),
        jax.ShapeDtypeStruct((b, W_MOBA, s), F32), jax.ShapeDtypeStruct((b, s, W_MOBA_AUG), BF16),
        jax.ShapeDtypeStruct((b, W_MOBA, s), BF16),
        jax.ShapeDtypeStruct((b, nblk, 1, W_MOBA_AUG), F32),
        jax.ShapeDtypeStruct((b, s, 3 * d), BF16),
    ]
    out_specs = [
        pl.BlockSpec((1, W_FOX, tm), tok_t), pl.BlockSpec((1, tm, W_FOX_AUG), tok),
        pl.BlockSpec((1, W_FOX, tm), tok_t),
        pl.BlockSpec((1, tm, W_SWA), tok), pl.BlockSpec((1, tm, W_SWA_KV), tok),
        pl.BlockSpec((1, tm, W_SWA_KV), tok),
        pl.BlockSpec((1, W_MOBA, tm), tok_t), pl.BlockSpec((1, tm, W_MOBA_AUG), tok),
        pl.BlockSpec((1, W_MOBA, tm), tok_t),
        pl.BlockSpec((1, tm // MOBA_BLOCK, 1, W_MOBA_AUG), lambda bi, j: (bi, j, 0, 0)),
        pl.BlockSpec((1, tm, 3 * d), tok),
    ]
    if has_prev:
        out_shape.append(jax.ShapeDtypeStruct((b, s, d), F32))
        out_specs.append(pl.BlockSpec((1, tm, d), tok))
    return pl.pallas_call(
        functools.partial(_in_kernel, has_prev=has_prev, tm=tm, d=d),
        grid=(b, s // tm),
        in_specs=in_specs,
        out_specs=out_specs,
        out_shape=out_shape,
        scratch_shapes=[pltpu.VMEM((8, LANES), F32)],
        compiler_params=_params(("parallel", "arbitrary")),
    )(*args)


def _flash_t(i, tq, n_heads, scores, values, o_ref, m_ref, l_ref, acc_ref):
    m_ref[...] = jnp.full_like(m_ref, NEG)
    l_ref[...] = jnp.zeros_like(l_ref)
    acc_ref[...] = jnp.zeros_like(acc_ref)
    causal = _iota((tq, tq), 0) <= _iota((tq, tq), 1)

    def block(j, diagonal):
        for h in range(n_heads):
            s = scores(j, h)
            if diagonal:
                s = jnp.where(causal, s, NEG)
            m_old = m_ref[h]
            m_new = jnp.maximum(m_old, jnp.max(s, axis=0, keepdims=True))
            alpha = jnp.exp2(m_old - m_new)
            p = jnp.exp2(s - m_new)
            l_ref[h] = alpha * l_ref[h] + jnp.sum(p, axis=0, keepdims=True)
            m_ref[h] = m_new
            pv = jnp.dot(values(j, h), p.astype(BF16), preferred_element_type=F32)
            rows = slice(h * HEAD_DIM, (h + 1) * HEAD_DIM)
            acc_ref[rows, :] = acc_ref[rows, :] * alpha + pv

    def body(j, carry):
        block(j, False)
        return carry

    lax.fori_loop(0, i, body, 0)
    block(i, True)
    for h in range(n_heads):
        rows = slice(h * HEAD_DIM, (h + 1) * HEAD_DIM)
        acc_ref[rows, :] = acc_ref[rows, :] * (1.0 / l_ref[h])
    o_ref[0] = jnp.transpose(acc_ref[...]).astype(BF16)


def _fox_kernel(qt_ref, k_ref, vt_ref, o_ref, m_ref, l_ref, acc_ref, *, tq):
    i = pl.program_id(1)
    tail = jnp.where(_iota((LANES - HEAD_DIM, tq), 0) < N_PIECES, 1.0, 0.0).astype(BF16)
    qa = [jnp.concatenate([qt_ref[0, h * HEAD_DIM:(h + 1) * HEAD_DIM, :], tail], axis=0)
          for h in range(FOX_HEADS)]

    def scores(j, h):
        start = pl.multiple_of(j * tq, tq)
        return jnp.dot(k_ref[0, pl.ds(start, tq), h * LANES:(h + 1) * LANES], qa[h],
                       preferred_element_type=F32)

    def values(j, h):
        start = pl.multiple_of(j * tq, tq)
        return vt_ref[0, h * HEAD_DIM:(h + 1) * HEAD_DIM, pl.ds(start, tq)]

    _flash_t(i, tq, FOX_HEADS, scores, values, o_ref, m_ref, l_ref, acc_ref)


def _fox_attention(qt, k_aug, vt, tq):
    b, w, s = qt.shape
    return pl.pallas_call(
        functools.partial(_fox_kernel, tq=tq),
        grid=(b, s // tq),
        in_specs=[
            pl.BlockSpec((1, w, tq), lambda bi, i: (bi, 0, i)),
            pl.BlockSpec((1, s, k_aug.shape[2]), lambda bi, i: (bi, 0, 0)),
            pl.BlockSpec((1, w, s), lambda bi, i: (bi, 0, 0)),
        ],
        out_specs=pl.BlockSpec((1, tq, w), lambda bi, i: (bi, i, 0)),
        out_shape=jax.ShapeDtypeStruct((b, s, w), BF16),
        scratch_shapes=[pltpu.VMEM((FOX_HEADS, 1, tq), F32), pltpu.VMEM((FOX_HEADS, 1, tq), F32),
                        pltpu.VMEM((w, tq), F32)],
        compiler_params=_params(("parallel", "parallel")),
    )(qt, k_aug, vt)


def _moba_kernel(qt_ref, k_ref, vt_ref, kmean_ref, o_ref, m_ref, l_ref, acc_ref, *, tq):
    i = pl.program_id(1)
    row = _iota((LANES, tq), 0)
    rowf = row.astype(F32)
    own = (i * tq + _iota((LANES, tq), 1)) // MOBA_BLOCK
    past = row < own
    tail = jnp.where(_iota((LANES - MOBA_ALIBI_LANE, tq), 0) < N_PIECES, 1.0, 0.0)
    qa = []
    for h in range(MOBA_HEADS):
        qh = qt_ref[0, h * HEAD_DIM:(h + 1) * HEAD_DIM, :]
        gate = jnp.dot(kmean_ref[0, h], qh, precision=HIGHEST, preferred_element_type=F32)
        g = jnp.where(past, gate, NEG)
        chosen = jnp.zeros((LANES, tq), jnp.bool_)
        for _ in range(MOBA_TOPK):
            mx = jnp.max(g, axis=0, keepdims=True)
            first = jnp.min(jnp.where(g == mx, rowf, float(LANES)), axis=0, keepdims=True)
            pick = (rowf == first) & (mx > 0.5 * NEG)
            chosen = chosen | pick
            g = jnp.where(pick, NEG, g)
        bias = jnp.where(past & jnp.logical_not(chosen), MOBA_MASK_BIAS, 0.0)
        qa.append(jnp.concatenate([qh, bias[:MOBA_MAX_BLOCKS], tail], axis=0).astype(BF16))

    def scores(j, h):
        start = pl.multiple_of(j * tq, tq)
        return jnp.dot(k_ref[0, pl.ds(start, tq), h * LANES:(h + 1) * LANES], qa[h],
                       preferred_element_type=F32)

    def values(j, h):
        start = pl.multiple_of(j * tq, tq)
        return vt_ref[0, h * HEAD_DIM:(h + 1) * HEAD_DIM, pl.ds(start, tq)]

    _flash_t(i, tq, MOBA_HEADS, scores, values, o_ref, m_ref, l_ref, acc_ref)


def _moba_attention(qt, k_aug, vt, kmean_mat, tq):
    b, w, s = qt.shape
    assert s // MOBA_BLOCK <= MOBA_MAX_BLOCKS and tq % MOBA_BLOCK == 0
    return pl.pallas_call(
        functools.partial(_moba_kernel, tq=tq),
        grid=(b, s // tq),
        in_specs=[
            pl.BlockSpec((1, w, tq), lambda bi, i: (bi, 0, i)),
            pl.BlockSpec((1, s, k_aug.shape[2]), lambda bi, i: (bi, 0, 0)),
            pl.BlockSpec((1, w, s), lambda bi, i: (bi, 0, 0)),
            pl.BlockSpec((1, MOBA_HEADS, LANES, HEAD_DIM), lambda bi, i: (bi, 0, 0, 0)),
        ],
        out_specs=pl.BlockSpec((1, tq, w), lambda bi, i: (bi, i, 0)),
        out_shape=jax.ShapeDtypeStruct((b, s, w), BF16),
        scratch_shapes=[pltpu.VMEM((MOBA_HEADS, 1, tq), F32), pltpu.VMEM((MOBA_HEADS, 1, tq), F32),
                        pltpu.VMEM((w, tq), F32)],
        compiler_params=_params(("parallel", "parallel")),
    )(qt, k_aug, vt, kmean_mat)


def _swa_kernel(sinks_ref, slopes_ref, q_ref, k_ref, v_ref, o_ref, *, tq):
    i = pl.program_id(1)
    tk = tq + SWA_WINDOW
    start = pl.multiple_of(jnp.maximum(i * tq - SWA_WINDOW, 0), SWA_WINDOW)
    kw = k_ref[0, pl.ds(start, tk), :]
    vw = v_ref[0, pl.ds(start, tk), :]
    dist = ((i * tq + _iota((tq, tk), 0)) - (start + _iota((tq, tk), 1))).astype(F32)
    valid = (dist >= 0.0) & (dist < float(SWA_WINDOW))
    low_lanes = _iota((1, LANES), 1) < HEAD_DIM
    pairs = SWA_HEADS // SWA_KV_HEADS
    for p in range(pairs):
        qg = q_ref[0, :, p * LANES:(p + 1) * LANES]
        outs = []
        for half in range(SWA_KV_HEADS):
            head = p + pairs * half
            mine = low_lanes if half == 0 else jnp.logical_not(low_lanes)
            qm = jnp.where(mine, qg, jnp.zeros_like(qg))
            s = lax.dot_general(qm, kw, NT_DIMS, preferred_element_type=F32)
            s = jnp.where(valid, s - slopes_ref[head] * dist, NEG)
            sink = sinks_ref[head]
            m = jnp.maximum(jnp.max(s, axis=-1, keepdims=True), sink)
            pr = jnp.exp(s - m)
            den = jnp.sum(pr, axis=-1, keepdims=True) + jnp.exp(sink - m)
            outs.append(jnp.dot(pr.astype(BF16), vw, preferred_element_type=F32) / den)
        o_ref[0, :, p * LANES:(p + 1) * LANES] = jnp.where(low_lanes, outs[0], outs[1]).astype(BF16)


def _swa_attention(q, k, v, sinks, slopes, tq):
    b, s, w = q.shape
    wk = k.shape[2]
    smem = pl.BlockSpec(memory_space=pltpu.SMEM)
    return pl.pallas_call(
        functools.partial(_swa_kernel, tq=tq),
        grid=(b, s // tq),
        in_specs=[
            smem, smem,
            pl.BlockSpec((1, tq, w), lambda bi, i: (bi, i, 0)),
            pl.BlockSpec((1, s, wk), lambda bi, i: (bi, 0, 0)),
            pl.BlockSpec((1, s, wk), lambda bi, i: (bi, 0, 0)),
        ],
        out_specs=pl.BlockSpec((1, tq, w), lambda bi, i: (bi, i, 0)),
        out_shape=jax.ShapeDtypeStruct((b, s, w), BF16),
        compiler_params=_params(("parallel", "parallel")),
    )(sinks, slopes, q, k, v)


def _out_kernel(oa_ref, ob_ref, oc_ref, gates_ref, x_ref, g1_ref, sh_ref, sc_ref, gain_ref,
                wa_ref, wb_ref, wc_ref, wo_ref, wr_ref, br_ref,
                x1_ref, hx_ref, route_ref, *, d):
    ya = jnp.dot(oa_ref[0], wa_ref[...], preferred_element_type=F32)
    yb = jnp.dot(ob_ref[0], wb_ref[...], preferred_element_type=F32)
    yc = jnp.dot(oc_ref[0], wc_ref[...], preferred_element_type=F32)
    g = gates_ref[0]
    mix = (g[:, :d].astype(F32) * ya + g[:, d:2 * d].astype(F32) * yb + g[:, 2 * d:].astype(F32) * yc)
    y = jnp.dot(mix.astype(BF16), wo_ref[...], preferred_element_type=F32)
    x1 = x_ref[0] + g1_ref[0] * y
    x1_ref[0] = x1
    ms = jnp.mean(x1 * x1, axis=-1, keepdims=True)
    h = x1 * lax.rsqrt(ms + EPS) * gain_ref[...]
    h = h * (1.0 + sc_ref[0]) + sh_ref[0]
    hx_ref[0, :, :d] = h
    logits = jnp.dot(h, wr_ref[...], precision=HIGHEST, preferred_element_type=F32) + br_ref[...]
    tm = logits.shape[0]
    lanef = _iota((tm, LANES), 1).astype(F32)
    vals, idxs = [], []
    for _ in range(TOP_K):
        mx = jnp.max(logits, axis=-1, keepdims=True)
        first = jnp.min(jnp.where(logits == mx, lanef, float(LANES)), axis=-1, keepdims=True)
        vals.append(mx)
        idxs.append(first)
        logits = jnp.where(lanef == first, -jnp.inf, logits)
    exps = [jnp.exp(v - vals[0]) for v in vals]
    den = exps[0]
    for e in exps[1:]:
        den = den + e
    gates = jnp.zeros((tm, LANES), F32)
    info = jnp.zeros((tm, LANES), F32)
    for k in range(TOP_K):
        wk = exps[k] / den
        gates = jnp.where(lanef == idxs[k], wk, gates)
        info = jnp.where(lanef == float(k), idxs[k], info)
    hx_ref[0, :, d:] = gates
    route_ref[0] = info


def _out_proj(oa, ob, oc, gates, x, g1, sh, sc, gain, wp, tm):
    b, s, d = x.shape
    tok = lambda bi, j: (bi, j, 0)
    row = lambda bi, j: (bi, 0, 0)
    const2 = lambda bi, j: (0, 0)
    return pl.pallas_call(
        functools.partial(_out_kernel, d=d),
        grid=(b, s // tm),
        in_specs=[
            pl.BlockSpec((1, tm, W_FOX), tok), pl.BlockSpec((1, tm, W_SWA), tok),
            pl.BlockSpec((1, tm, W_MOBA), tok), pl.BlockSpec((1, tm, 3 * d), tok),
            pl.BlockSpec((1, tm, d), tok),
            pl.BlockSpec((1, 1, d), row), pl.BlockSpec((1, 1, d), row), pl.BlockSpec((1, 1, d), row),
            pl.BlockSpec((1, d), const2),
            pl.BlockSpec((W_FOX, d), const2), pl.BlockSpec((W_SWA, d), const2),
            pl.BlockSpec((W_MOBA, d), const2), pl.BlockSpec((d, d), const2),
            pl.BlockSpec((d, LANES), const2), pl.BlockSpec((1, LANES), const2),
        ],
        out_specs=[pl.BlockSpec((1, tm, d), tok), pl.BlockSpec((1, tm, d + LANES), tok),
                   pl.BlockSpec((1, tm, LANES), tok)],
        out_shape=[jax.ShapeDtypeStruct((b, s, d), F32), jax.ShapeDtypeStruct((b, s, d + LANES), F32),
                   jax.ShapeDtypeStruct((b, s, LANES), F32)],
        compiler_params=_params(("parallel", "parallel")),
    )(oa, ob, oc, gates, x, g1, sh, sc, gain, wp["w_a"], wp["w_b"], wp["w_c"], wp["w_o"],
      wp["w_r"], wp["b_r"])


MOE_PAD_ROWS = 8


def _moe_kernel(tile_e_ref, tile_p0_ref, tile_off_ref, tile_n_ref, tok_ref,
                hx_ref, w1g_ref, w1l_ref, b1g_ref, b1l_ref, w2_ref, b2_ref,
                out_ref, xa_ref, xb_ref, ya_ref, yb_ref, *, slots, tm, tc, d):
    c = pl.program_id(0)
    i = pl.program_id(1)
    k = c * slots + i + 1

    def gather(kk, x_ref):
        p0 = tile_p0_ref[kk]
        for r in range(tm):
            tk = tok_ref[p0 + r]
            x_ref[r // 8, r % 8:r % 8 + 1, :] = hx_ref[0, pl.ds(tk, 1), :]

    def expert(x_ref, y_ref):
        xg = x_ref[...].reshape(tm, hx_ref.shape[2])
        xb = xg[:, :d].astype(BF16)
        lane = _iota((tm, LANES), 1)
        wcol = jnp.sum(jnp.where(lane == tile_e_ref[k], xg[:, d:], 0.0), axis=-1, keepdims=True)
        ug = jnp.dot(xb, w1g_ref[0], preferred_element_type=F32) + b1g_ref[0]
        ul = jnp.dot(xb, w1l_ref[0], preferred_element_type=F32) + b1l_ref[0]
        ug = jnp.minimum(ug, SWIGLU_LIMIT)
        ul = jnp.clip(ul, -SWIGLU_LIMIT, SWIGLU_LIMIT)
        act = ug * jax.nn.sigmoid(SWIGLU_ALPHA * ug) * (ul + 1.0)
        y = jnp.dot(act.astype(BF16), w2_ref[0], preferred_element_type=F32) + b2_ref[0]
        y_ref[...] = (y * wcol).reshape(y_ref.shape)

    def scatter(kk, y_ref):
        p0 = tile_p0_ref[kk]
        off = tile_off_ref[kk]
        end = off + tile_n_ref[kk]
        for r0 in range(0, tm, 4):
            toks = [jnp.where((r0 + u >= off) & (r0 + u < end), tok_ref[p0 + r0 + u], tc + u) for u in range(4)]
            rows = [out_ref[0, pl.ds(tk, 1), :] for tk in toks]
            for u in range(4):
                r = r0 + u
                out_ref[0, pl.ds(toks[u], 1), :] = rows[u] + y_ref[r // 8, r % 8:r % 8 + 1, :]

    @pl.when(i == 0)
    def _():
        out_ref[...] = jnp.zeros_like(out_ref)
        ya_ref[...] = jnp.zeros_like(ya_ref)
        yb_ref[...] = jnp.zeros_like(yb_ref)
        gather(k, xa_ref)

    busy = (tile_n_ref[k] > 0) | (tile_n_ref[k - 1] > 0)

    @pl.when(busy & (i % 2 == 0))
    def _():
        gather(k + 1, xb_ref)
        expert(xa_ref, ya_ref)
        scatter(k - 1, yb_ref)

    @pl.when(busy & (i % 2 == 1))
    def _():
        gather(k + 1, xa_ref)
        expert(xb_ref, yb_ref)
        scatter(k - 1, ya_ref)


def _moe(hx, tables, wp, tc, tm):
    b, s, dx = hx.shape
    d = dx - LANES
    nc = (b * s) // tc
    tile_e = tables[0]
    slots = tile_e.shape[0] // nc
    f = wp["w1g"].shape[2]
    hx = hx.reshape(nc, tc, dx)
    chunk = lambda c, i, *prefetch: (c, 0, 0)
    expert = lambda c, i, te, *prefetch: (te[c * slots + i + 1], 0, 0)
    x_tile = pltpu.VMEM((tm // 8, 8, dx), F32)
    y_tile = pltpu.VMEM((tm // 8, 8, d), F32)
    out = pl.pallas_call(
        functools.partial(_moe_kernel, slots=slots, tm=tm, tc=tc, d=d),
        grid_spec=pltpu.PrefetchScalarGridSpec(
            num_scalar_prefetch=len(tables),
            grid=(nc, slots - 2),
            in_specs=[
                pl.BlockSpec((1, tc, dx), chunk, pipeline_mode=pl.Buffered(1)),
                pl.BlockSpec((1, d, f), expert), pl.BlockSpec((1, d, f), expert),
                pl.BlockSpec((1, 1, f), expert), pl.BlockSpec((1, 1, f), expert),
                pl.BlockSpec((1, f, d), expert), pl.BlockSpec((1, 1, d), expert),
            ],
            out_specs=pl.BlockSpec((1, tc + MOE_PAD_ROWS, d), chunk, pipeline_mode=pl.Buffered(1)),
            scratch_shapes=[x_tile, x_tile, y_tile, y_tile],
        ),
        out_shape=jax.ShapeDtypeStruct((nc, tc + MOE_PAD_ROWS, d), F32),
        compiler_params=_params(("parallel", "arbitrary")),
    )(*tables, hx, wp["w1g"], wp["w1l"], wp["b1g"], wp["b1l"], wp["w2"], wp["b2"])
    if tc == s:
        return out
    return out[:, :tc].reshape(b, s, d)


def _routing_tables(route, n_experts, tc, tm):
    b, s, _ = route.shape
    t = b * s
    nc = t // tc
    idx = route[:, :, :TOP_K].astype(jnp.int32).reshape(nc, tc, TOP_K)
    key = idx * tc + jnp.arange(tc, dtype=jnp.int32)[None, :, None]
    key = jnp.sort(key.reshape(nc, tc * TOP_K), axis=1)
    tok = key % tc
    e_sorted = key // tc
    experts = jnp.arange(n_experts, dtype=jnp.int32)
    cstart = jnp.sum(e_sorted[:, :, None] < experts[None, None, :], axis=1).astype(jnp.int32)
    cend = jnp.concatenate([cstart[:, 1:], jnp.full((nc, 1), tc * TOP_K, jnp.int32)], axis=1)
    count = cend - cstart
    tiles = (count + tm - 1) // tm
    tend = jnp.cumsum(tiles, axis=1)
    tstart = tend - tiles
    nt = (tc * TOP_K) // tm + n_experts
    slot = jnp.arange(-1, nt + 2, dtype=jnp.int32)
    total = tend[:, -1:]
    live = (slot[None, :] >= 0) & (slot[None, :] < total)
    slot_c = jnp.clip(slot[None, :], 0, total - 1)
    e_of = jnp.sum(slot_c[:, :, None] >= tend[:, None, :], axis=2).astype(jnp.int32)
    pick = lambda tbl: jnp.sum(jnp.where(e_of[:, :, None] == experts[None, None, :], tbl[:, None, :], 0), axis=2)
    jj = slot_c - pick(tstart)
    p0 = pick(cstart) + jj * tm
    n = jnp.where(live, jnp.clip(pick(count) - jj * tm, 0, tm), 0)
    p0 = p0 + (jnp.arange(nc, dtype=jnp.int32) * (tc * TOP_K))[:, None]
    first = jnp.minimum(p0, nc * tc * TOP_K - tm)
    flat = lambda a: a.reshape(-1).astype(jnp.int32)
    return flat(e_of), flat(first), flat(p0 - first), flat(n), flat(tok)


def _residual_kernel(x_ref, moe_ref, g_ref, o_ref):
    o_ref[0] = x_ref[0] + g_ref[0] * moe_ref[0]


def _residual(x, moe, g, tm):
    b, s, d = x.shape
    tok = lambda bi, j: (bi, j, 0)
    return pl.pallas_call(
        _residual_kernel,
        grid=(b, s // tm),
        in_specs=[pl.BlockSpec((1, tm, d), tok), pl.BlockSpec((1, tm, d), tok),
                  pl.BlockSpec((1, 1, d), lambda bi, j: (bi, 0, 0))],
        out_specs=pl.BlockSpec((1, tm, d), tok),
        out_shape=jax.ShapeDtypeStruct((b, s, d), F32),
        compiler_params=_params(("parallel", "parallel")),
    )(x, moe, g)


def _split_kernel(w_ref, p_ref, g_ref, l_ref):
    half = MXU_DIM // 2
    for c in range(w_ref.shape[2] // MXU_DIM):
        w = w_ref[0, :, c * MXU_DIM:(c + 1) * MXU_DIM].astype(BF16)
        z = jnp.dot(w, p_ref[...], preferred_element_type=F32)
        g_ref[0, :, c * half:(c + 1) * half] = z[:, :half].astype(BF16)
        l_ref[0, :, c * half:(c + 1) * half] = z[:, half:].astype(BF16)


def _split_glu_weights(w):
    n, d, f2 = w.shape
    cw = min(f2, 4 * MXU_DIM)
    half = MXU_DIM // 2
    k = np.arange(half)
    perm = np.zeros((MXU_DIM, MXU_DIM), np.float32)
    perm[2 * k, k] = 1.0
    perm[2 * k + 1, half + k] = 1.0
    out = jax.ShapeDtypeStruct((n, d, f2 // 2), BF16)
    return pl.pallas_call(
        _split_kernel,
        grid=(n, f2 // cw),
        in_specs=[pl.BlockSpec((1, d, cw), lambda i, j: (i, 0, j)),
                  pl.BlockSpec((MXU_DIM, MXU_DIM), lambda i, j: (0, 0))],
        out_specs=[pl.BlockSpec((1, d, cw // 2), lambda i, j: (i, 0, j))] * 2,
        out_shape=[out, out],
        compiler_params=_params(("parallel", "parallel")),
    )(w, jnp.asarray(perm, BF16))


def _pad_heads(w, n_heads):
    lead = w.shape[:-1]
    w = w.reshape(lead + (n_heads, HEAD_DIM))
    w = jnp.concatenate([w, jnp.zeros_like(w)], axis=-1)
    return w.reshape(lead + (n_heads * LANES,))


def _layer_weights(l, d, tm, slopes, w_in, b_fgate, qk_gain, w_br_fox, w_br_swa, w_br_moba, w_out,
                   w_router, b_router, w1g, w1l, b_exp1, w_exp2, b_exp2):
    splits = (W_FOX, W_FOX, W_FOX, FOX_HEADS, W_SWA, W_SWA_KV, W_SWA_KV, W_MOBA, W_MOBA, W_MOBA, 3 * d)
    pts = np.cumsum(splits)[:-1].tolist()
    qa, ka, va, fa, qb, kb, vb, qc, kc, vc, gt = jnp.split(w_in[l], pts, axis=1)
    order = np.asarray(SWA_HEAD_ORDER)
    qb = qb.reshape(d, SWA_HEADS, HEAD_DIM)[:, order].reshape(d, W_SWA)
    fa = jnp.zeros((d, LANES), F32).at[:, :FOX_HEADS].set(fa)
    w_cat = jnp.concatenate([_pad_heads(ka, FOX_HEADS), fa, qb, kb, vb, _pad_heads(kc, MOBA_HEADS), gt],
                            axis=1).astype(BF16)
    w_t = jnp.concatenate([qa, va, qc, vc], axis=1).T.astype(BF16)
    b_f = jnp.zeros((1, LANES), F32).at[0, :FOX_HEADS].set(b_fgate[l])
    g = qk_gain[l].astype(F32)
    qscale = HEAD_DIM ** -0.5
    gvec = jnp.concatenate([
        _pad_heads(jnp.tile(g[1], FOX_HEADS), FOX_HEADS), jnp.ones((LANES,), F32),
        jnp.tile(g[2], SWA_HEADS) * qscale, jnp.tile(g[3], SWA_KV_HEADS), jnp.ones((W_SWA_KV,), F32),
        _pad_heads(jnp.tile(g[5], MOBA_HEADS), MOBA_HEADS),
    ])[None, :]
    gain_t = jnp.concatenate([jnp.tile(g[0], FOX_HEADS), jnp.tile(g[4], MOBA_HEADS)]) * (qscale * LOG2E)
    gain_t = jnp.broadcast_to(gain_t[:, None], (W_FOX + W_MOBA, tm))
    heads = MXU_DIM // HEAD_DIM
    gsum = jnp.asarray(np.kron(np.eye(heads), np.ones((HEAD_DIM, HEAD_DIM))), BF16)
    tril = jnp.asarray(np.tril(np.ones((tm, tm))), F32)
    perm = np.zeros((N_PIECES * LANES, W_FOX_AUG), np.float32)
    for h in range(FOX_HEADS):
        for p in range(N_PIECES):
            perm[p * LANES + h, h * LANES + FOX_CUM_LANE + p] = 1.0
    slope_lanes = jnp.repeat(slopes[SWA_HEADS:] * LOG2E, LANES)[None, :]
    n_exp = w_router.shape[2]
    w_r = jnp.zeros((d, LANES), F32).at[:, :n_exp].set(w_router[l])
    b_r = jnp.full((1, LANES), NEG, F32).at[0, :n_exp].set(b_router[l])
    w_b = w_br_swa[l].reshape(SWA_HEADS, HEAD_DIM, d)[order].reshape(W_SWA, d)
    return dict(
        w_cat=w_cat, w_t=w_t, b_f=b_f, gvec=gvec, gain_t=gain_t, gsum=gsum, tril=tril,
        perm=jnp.asarray(perm, BF16), slope_lanes=slope_lanes,
        w_a=w_br_fox[l].astype(BF16), w_b=w_b.astype(BF16), w_c=w_br_moba[l].astype(BF16),
        w_o=w_out[l].astype(BF16), w_r=w_r, b_r=b_r,
        w1g=w1g[l], w1l=w1l[l],
        b1g=b_exp1[l, :, None, 0::2], b1l=b_exp1[l, :, None, 1::2],
        w2=w_exp2[l].astype(BF16), b2=b_exp2[l][:, None, :],
    )


def kernel(x, c, w_ada, b_ada, norm_gain, w_in, b_fgate, qk_gain, attn_sinks, w_br_fox, w_br_swa, w_br_moba, w_out, w_router, b_router, w_exp1, b_exp1, w_exp2, b_exp2):
    b, s, d = x.shape
    depth = w_ada.shape[0]
    n_experts = w_router.shape[2]
    tm = min(s, 512)
    tq_fox = min(s, 512)
    tq_swa = min(s, 256)
    tc = min(s, 4096)
    tm_moe = 256
    n_alibi = SWA_HEADS + MOBA_HEADS
    slopes = jnp.exp2(-8.0 * jnp.arange(1, n_alibi + 1, dtype=F32) / n_alibi)

    mod = _modulation(c, w_ada, b_ada)
    mod = mod.reshape(depth, b, 6, 1, d)
    w1g, w1l = _split_glu_weights(w_exp1.reshape((depth * n_experts,) + w_exp1.shape[2:]))
    w1g = w1g.reshape((depth, n_experts) + w1g.shape[1:])
    w1l = w1l.reshape((depth, n_experts) + w1l.shape[1:])
    prev = None
    for l in range(depth):
        sh1, sc1, g1, sh2, sc2, g2 = (mod[l, :, k] for k in range(6))
        wp = _layer_weights(l, d, tm, slopes, w_in, b_fgate, qk_gain, w_br_fox, w_br_swa, w_br_moba, w_out,
                            w_router, b_router, w1g, w1l, b_exp1, w_exp2, b_exp2)
        outs = _in_proj(x, prev, sh1, sc1, norm_gain[l, 0][None, :], wp, tm)
        qft, kfa, vft, qs, ks, vs, qmt, kma, vmt, kmean, gates = outs[:11]
        if prev is not None:
            x = outs[11]
        o_a = _fox_attention(qft, kfa, vft, tq_fox)
        o_b = _swa_attention(qs, ks, vs, attn_sinks[l].astype(F32), slopes, tq_swa)
        nblk = s // MOBA_BLOCK
        km4 = kmean.reshape(b, nblk, MOBA_HEADS, LANES)[..., :HEAD_DIM]
        kmat = jnp.zeros((b, MOBA_HEADS, LANES, HEAD_DIM), F32)
        kmat = kmat.at[:, :, :nblk].set(jnp.transpose(km4, (0, 2, 1, 3)))
        o_c = _moba_attention(qmt, kma, vmt, kmat, tq_fox)
        x, hx, route = _out_proj(o_a, o_b, o_c, gates, x, g1, sh2, sc2, norm_gain[l, 1][None, :], wp, tm)
        tables = _routing_tables(route, n_experts, tc, tm_moe)
        moe = _moe(hx, tables, wp, tc, tm_moe)
        prev = (moe, g2)
    return _residual(x, prev[0], prev[1], tm)
```

```python
import functools

import numpy as np
import jax
import jax.numpy as jnp
from jax import lax
from jax.experimental import pallas as pl
from jax.experimental.pallas import tpu as pltpu

HEAD_DIM = 64
FOX_HEADS = 4
SWA_HEADS = 8
SWA_KV_HEADS = 2
SWA_WINDOW = 128
MOBA_HEADS = 4
MOBA_BLOCK = 256
MOBA_TOPK = 3
TOP_K = 4
SWIGLU_ALPHA = 1.702
SWIGLU_LIMIT = 7.0
EPS = 1e-6
NEG = -1e30
LOG2E = 1.4426950408889634
MOBA_MASK_BIAS = -float(2 ** 30)

LANES = 128
MXU_DIM = 256
VMEM_LIMIT = 60 * 1024 * 1024

W_FOX = FOX_HEADS * HEAD_DIM
W_SWA = SWA_HEADS * HEAD_DIM
W_SWA_KV = SWA_KV_HEADS * HEAD_DIM
W_MOBA = MOBA_HEADS * HEAD_DIM
W_FOX_AUG = FOX_HEADS * LANES
FOX_CUM_LANE = HEAD_DIM
W_MOBA_AUG = MOBA_HEADS * LANES
MOBA_BLOCK_LANE = HEAD_DIM
MOBA_MAX_BLOCKS = 16
MOBA_ALIBI_LANE = MOBA_BLOCK_LANE + MOBA_MAX_BLOCKS
N_PIECES = 3
SWA_HEAD_ORDER = (0, 4, 1, 5, 2, 6, 3, 7)

F32 = jnp.float32
BF16 = jnp.bfloat16
HIGHEST = lax.Precision.HIGHEST
NT_DIMS = (((1,), (1,)), ((), ()))


def _params(semantics):
    return pltpu.CompilerParams(dimension_semantics=semantics, vmem_limit_bytes=VMEM_LIMIT)


def _iota(shape, dim, dtype=jnp.int32):
    return lax.broadcasted_iota(dtype, shape, dim)


def _pieces(x):
    hi = x.astype(BF16).astype(F32)
    r = x - hi
    mid = r.astype(BF16).astype(F32)
    return hi, mid, r - mid


def _mod_kernel(c_ref, w_ref, b_ref, o_ref):
    c = c_ref[...]
    s = c * jax.nn.sigmoid(c)
    o_ref[0] = jnp.dot(s, w_ref[0], precision=HIGHEST, preferred_element_type=F32) + b_ref[0]


def _modulation(c, w_ada, b_ada):
    depth, d, n = w_ada.shape
    b = c.shape[0]
    tn = d
    return pl.pallas_call(
        _mod_kernel,
        grid=(depth, n // tn),
        in_specs=[
            pl.BlockSpec((b, d), lambda l, j: (0, 0)),
            pl.BlockSpec((1, d, tn), lambda l, j: (l, 0, j)),
            pl.BlockSpec((1, 1, tn), lambda l, j: (l, 0, j)),
        ],
        out_specs=pl.BlockSpec((1, b, tn), lambda l, j: (l, 0, j)),
        out_shape=jax.ShapeDtypeStruct((depth, b, n), F32),
        compiler_params=_params(("parallel", "parallel")),
    )(c, w_ada, b_ada.reshape(depth, 1, n))


def _head_norm(z, g_ref, gain):
    w = z.shape[1]
    cw = min(w, MXU_DIM)
    outs = []
    for c0 in range(0, w, cw):
        zz = z[:, c0:c0 + cw]
        ss = jnp.dot((zz * zz).astype(BF16), g_ref[:cw, :cw], preferred_element_type=F32)
        outs.append(zz * lax.rsqrt(ss * (1.0 / HEAD_DIM) + EPS))
    zn = outs[0] if len(outs) == 1 else jnp.concatenate(outs, axis=1)
    return zn * gain


def _head_norm_t(zt, gain_t):
    rows, cols = zt.shape
    z3 = zt.reshape(rows // HEAD_DIM, HEAD_DIM, cols)
    ms = jnp.mean(z3 * z3, axis=1, keepdims=True)
    return (z3 * lax.rsqrt(ms + EPS)).reshape(rows, cols) * gain_t


def _in_kernel(*refs, has_prev, tm, d):
    if has_prev:
        x_ref, moe_ref, g2_ref = refs[:3]
        refs = refs[3:]
    else:
        x_ref = refs[0]
        refs = refs[1:]
    (sh_ref, sc_ref, gain_ref, w_ref, wt_ref, bf_ref, gvec_ref, gt_ref, g_ref, tril_ref, perm_ref, slope_ref,
     qft_ref, kfa_ref, vft_ref, qs_ref, ks_ref, vs_ref, qmt_ref, kma_ref, vmt_ref,
     kmean_ref, gates_ref) = refs[:23]
    refs = refs[23:]
    if has_prev:
        xn_ref, carry_ref = refs
    else:
        (carry_ref,) = refs
    j = pl.program_id(1)

    x = x_ref[0]
    if has_prev:
        x = x + g2_ref[0] * moe_ref[0]
        xn_ref[0] = x
    ms = jnp.mean(x * x, axis=-1, keepdims=True)
    h = x * lax.rsqrt(ms + EPS) * gain_ref[...]
    h = h * (1.0 + sc_ref[0]) + sh_ref[0]
    hb = h.astype(BF16)

    def proj(c0, c1):
        return jnp.dot(hb, w_ref[:, c0:c1], preferred_element_type=F32)

    def gain(c0, c1):
        return gvec_ref[:, c0:c1]

    zt = lax.dot_general(wt_ref[...], hb, NT_DIMS, preferred_element_type=F32)
    qft_ref[0] = _head_norm_t(zt[:W_FOX], gt_ref[:W_FOX]).astype(BF16)
    vft_ref[0] = zt[W_FOX:2 * W_FOX].astype(BF16)
    o = 2 * W_FOX
    qmt_ref[0] = _head_norm_t(zt[o:o + W_MOBA], gt_ref[W_FOX:W_FOX + W_MOBA])
    vmt_ref[0] = zt[o + W_MOBA:o + 2 * W_MOBA].astype(BF16)

    @pl.when(j == 0)
    def _():
        carry_ref[...] = jnp.zeros_like(carry_ref)

    o = 0
    z = proj(o, o + W_FOX_AUG + LANES)
    kf = _head_norm(z[:, :W_FOX_AUG], g_ref, gain(o, o + W_FOX_AUG))
    fa = z[:, W_FOX_AUG:] + bf_ref[...]
    logf = -(jnp.maximum(-fa, 0.0) + jnp.log(1.0 + jnp.exp(-jnp.abs(fa))))
    cum = jnp.dot(tril_ref[...], logf, precision=HIGHEST, preferred_element_type=F32) + carry_ref[0:1, :]
    carry_ref[...] = jnp.broadcast_to(cum[tm - 1:tm, :], carry_ref.shape)
    parts = jnp.concatenate(_pieces(cum * (-LOG2E)), axis=1).astype(BF16)
    kfa_ref[0] = (kf + jnp.dot(parts, perm_ref[...], preferred_element_type=F32)).astype(BF16)
    o += W_FOX_AUG + LANES
    z = proj(o, o + W_SWA + 2 * W_SWA_KV)
    qs_ref[0] = _head_norm(z[:, :W_SWA], g_ref, gain(o, o + W_SWA)).astype(BF16)
    ks_ref[0] = _head_norm(z[:, W_SWA:W_SWA + W_SWA_KV], g_ref,
                           gain(o + W_SWA, o + W_SWA + W_SWA_KV)).astype(BF16)
    vs_ref[0] = z[:, W_SWA + W_SWA_KV:].astype(BF16)
    o += W_SWA + 2 * W_SWA_KV
    z = proj(o, o + W_MOBA_AUG)
    kn = _head_norm(z, g_ref, gain(o, o + W_MOBA_AUG))
    nb = tm // MOBA_BLOCK
    kmean_ref[0] = jnp.mean(kn.reshape(nb, MOBA_BLOCK, W_MOBA_AUG), axis=1).reshape(nb, 1, W_MOBA_AUG)
    lane = _iota((tm, W_MOBA_AUG), 1) % LANES
    pos = j * tm + _iota((tm, W_MOBA_AUG), 0)
    a_hi, a_mid, a_lo = _pieces(slope_ref[...] * pos.astype(F32))
    ka = jnp.where(lane == MOBA_BLOCK_LANE + pos // MOBA_BLOCK, 1.0, kn)
    ka = jnp.where(lane == MOBA_ALIBI_LANE, a_hi, ka)
    ka = jnp.where(lane == MOBA_ALIBI_LANE + 1, a_mid, ka)
    ka = jnp.where(lane == MOBA_ALIBI_LANE + 2, a_lo, ka)
    kma_ref[0] = ka.astype(BF16)
    o += W_MOBA_AUG
    for br in range(3):
        zg = proj(o + br * d, o + (br + 1) * d)
        gates_ref[0, :, br * d:(br + 1) * d] = jax.nn.sigmoid(zg).astype(BF16)


def _layer_spec(a, l):
    return pl.BlockSpec((None,) + a.shape[1:], lambda *_: (l,) + (0,) * (a.ndim - 1))


def _const_spec(a):
    return pl.BlockSpec(a.shape, lambda *_: (0,) * a.ndim)


def _mod_spec(mod, l, k):
    d = mod.shape[3] // 6
    return pl.BlockSpec((None, 1, 1, d), lambda bi, j: (l, bi, 0, k))


def _in_proj(x, moe, mod, norm_gain, l, wp, tm):
    b, s, d = x.shape
    has_prev = moe is not None
    tok = lambda bi, j: (bi, j, 0)
    tok_t = lambda bi, j: (bi, 0, j)
    in_specs = [pl.BlockSpec((1, tm, d), tok)]
    args = [x]
    if has_prev:
        in_specs += [pl.BlockSpec((1, tm, d), tok), _mod_spec(mod, l - 1, 5)]
        args += [moe, mod]
    stacked = [wp["w_cat"], wp["w_t"], wp["b_f"], wp["gvec"], wp["gain_t"]]
    shared = [wp["gsum"], wp["tril"], wp["perm"], wp["slope_lanes"]]
    in_specs += [_mod_spec(mod, l, 0), _mod_spec(mod, l, 1),
                 pl.BlockSpec((None, None, 1, d), lambda bi, j: (l, 0, 0, 0))]
    in_specs += [_layer_spec(a, l) for a in stacked] + [_const_spec(a) for a in shared]
    args += [mod, mod, norm_gain] + stacked + shared
    nblk = s // MOBA_BLOCK
    out_shape = [
        jax.ShapeDtypeStruct((b, W_FOX, s), BF16), jax.ShapeDtypeStruct((b, s, W_FOX_AUG), BF16),
        jax.ShapeDtypeStruct((b, W_FOX, s), BF16),
        jax.ShapeDtypeStruct((b, s, W_SWA), BF16), jax.ShapeDtypeStruct((b, s, W_SWA_KV), BF16),
        jax.ShapeDtypeStruct((b, s, W_SWA_KV), BF16),
        jax.ShapeDtypeStruct((b, W_MOBA, s), F32), jax.ShapeDtypeStruct((b, s, W_MOBA_AUG), BF16),
        jax.ShapeDtypeStruct((b, W_MOBA, s), BF16),
        jax.ShapeDtypeStruct((b, nblk, 1, W_MOBA_AUG), F32),
        jax.ShapeDtypeStruct((b, s, 3 * d), BF16),
    ]
    out_specs = [
        pl.BlockSpec((1, W_FOX, tm), tok_t), pl.BlockSpec((1, tm, W_FOX_AUG), tok),
        pl.BlockSpec((1, W_FOX, tm), tok_t),
        pl.BlockSpec((1, tm, W_SWA), tok), pl.BlockSpec((1, tm, W_SWA_KV), tok),
        pl.BlockSpec((1, tm, W_SWA_KV), tok),
        pl.BlockSpec((1, W_MOBA, tm), tok_t), pl.BlockSpec((1, tm, W_MOBA_AUG), tok),
        pl.BlockSpec((1, W_MOBA, tm), tok_t),
        pl.BlockSpec((1, tm // MOBA_BLOCK, 1, W_MOBA_AUG), lambda bi, j: (bi, j, 0, 0)),
        pl.BlockSpec((1, tm, 3 * d), tok),
    ]
    if has_prev:
        out_shape.append(jax.ShapeDtypeStruct((b, s, d), F32))
        out_specs.append(pl.BlockSpec((1, tm, d), tok))
    return pl.pallas_call(
        functools.partial(_in_kernel, has_prev=has_prev, tm=tm, d=d),
        grid=(b, s // tm),
        in_specs=in_specs,
        out_specs=out_specs,
        out_shape=out_shape,
        scratch_shapes=[pltpu.VMEM((8, LANES), F32)],
        compiler_params=_params(("parallel", "arbitrary")),
    )(*args)


def _flash_t(i, tq, n_heads, scores, values, o_ref, m_ref, l_ref, acc_ref):
    m_ref[...] = jnp.full_like(m_ref, NEG)
    l_ref[...] = jnp.zeros_like(l_ref)
    acc_ref[...] = jnp.zeros_like(acc_ref)
    causal = _iota((tq, tq), 0) <= _iota((tq, tq), 1)

    def block(j, diagonal):
        for h in range(n_heads):
            s = scores(j, h)
            if diagonal:
                s = jnp.where(causal, s, NEG)
            m_old = m_ref[h]
            m_new = jnp.maximum(m_old, jnp.max(s, axis=0, keepdims=True))
            alpha = jnp.exp2(m_old - m_new)
            p = jnp.exp2(s - m_new)
            l_ref[h] = alpha * l_ref[h] + jnp.sum(p, axis=0, keepdims=True)
            m_ref[h] = m_new
            pv = jnp.dot(values(j, h), p.astype(BF16), preferred_element_type=F32)
            rows = slice(h * HEAD_DIM, (h + 1) * HEAD_DIM)
            acc_ref[rows, :] = acc_ref[rows, :] * alpha + pv

    def body(j, carry):
        block(j, False)
        return carry

    lax.fori_loop(0, i, body, 0)
    block(i, True)
    for h in range(n_heads):
        rows = slice(h * HEAD_DIM, (h + 1) * HEAD_DIM)
        acc_ref[rows, :] = acc_ref[rows, :] * (1.0 / l_ref[h])
    o_ref[0] = jnp.transpose(acc_ref[...]).astype(BF16)


def _fox_kernel(qt_ref, k_ref, vt_ref, o_ref, m_ref, l_ref, acc_ref, *, tq):
    i = pl.program_id(1)
    tail = jnp.where(_iota((LANES - HEAD_DIM, tq), 0) < N_PIECES, 1.0, 0.0).astype(BF16)
    qa = [jnp.concatenate([qt_ref[0, h * HEAD_DIM:(h + 1) * HEAD_DIM, :], tail], axis=0)
          for h in range(FOX_HEADS)]

    def scores(j, h):
        start = pl.multiple_of(j * tq, tq)
        return jnp.dot(k_ref[0, pl.ds(start, tq), h * LANES:(h + 1) * LANES], qa[h],
                       preferred_element_type=F32)

    def values(j, h):
        start = pl.multiple_of(j * tq, tq)
        return vt_ref[0, h * HEAD_DIM:(h + 1) * HEAD_DIM, pl.ds(start, tq)]

    _flash_t(i, tq, FOX_HEADS, scores, values, o_ref, m_ref, l_ref, acc_ref)


def _fox_attention(qt, k_aug, vt, tq):
    b, w, s = qt.shape
    return pl.pallas_call(
        functools.partial(_fox_kernel, tq=tq),
        grid=(b, s // tq),
        in_specs=[
            pl.BlockSpec((1, w, tq), lambda bi, i: (bi, 0, i)),
            pl.BlockSpec((1, s, k_aug.shape[2]), lambda bi, i: (bi, 0, 0)),
            pl.BlockSpec((1, w, s), lambda bi, i: (bi, 0, 0)),
        ],
        out_specs=pl.BlockSpec((1, tq, w), lambda bi, i: (bi, i, 0)),
        out_shape=jax.ShapeDtypeStruct((b, s, w), BF16),
        scratch_shapes=[pltpu.VMEM((FOX_HEADS, 1, tq), F32), pltpu.VMEM((FOX_HEADS, 1, tq), F32),
                        pltpu.VMEM((w, tq), F32)],
        compiler_params=_params(("parallel", "parallel")),
    )(qt, k_aug, vt)


def _moba_kernel(qt_ref, k_ref, vt_ref, kmean_ref, o_ref, m_ref, l_ref, acc_ref, *, tq):
    i = pl.program_id(1)
    row = _iota((LANES, tq), 0)
    rowf = row.astype(F32)
    own = (i * tq + _iota((LANES, tq), 1)) // MOBA_BLOCK
    past = row < own
    tail = jnp.where(_iota((LANES - MOBA_ALIBI_LANE, tq), 0) < N_PIECES, 1.0, 0.0)
    qa = []
    for h in range(MOBA_HEADS):
        qh = qt_ref[0, h * HEAD_DIM:(h + 1) * HEAD_DIM, :]
        gate = jnp.dot(kmean_ref[0, h], qh, precision=HIGHEST, preferred_element_type=F32)
        g = jnp.where(past, gate, NEG)
        chosen = jnp.zeros((LANES, tq), jnp.bool_)
        for _ in range(MOBA_TOPK):
            mx = jnp.max(g, axis=0, keepdims=True)
            first = jnp.min(jnp.where(g == mx, rowf, float(LANES)), axis=0, keepdims=True)
            pick = (rowf == first) & (mx > 0.5 * NEG)
            chosen = chosen | pick
            g = jnp.where(pick, NEG, g)
        bias = jnp.where(past & jnp.logical_not(chosen), MOBA_MASK_BIAS, 0.0)
        qa.append(jnp.concatenate([qh, bias[:MOBA_MAX_BLOCKS], tail], axis=0).astype(BF16))

    def scores(j, h):
        start = pl.multiple_of(j * tq, tq)
        return jnp.dot(k_ref[0, pl.ds(start, tq), h * LANES:(h + 1) * LANES], qa[h],
                       preferred_element_type=F32)

    def values(j, h):
        start = pl.multiple_of(j * tq, tq)
        return vt_ref[0, h * HEAD_DIM:(h + 1) * HEAD_DIM, pl.ds(start, tq)]

    _flash_t(i, tq, MOBA_HEADS, scores, values, o_ref, m_ref, l_ref, acc_ref)


def _moba_attention(qt, k_aug, vt, kmean_mat, tq):
    b, w, s = qt.shape
    assert s // MOBA_BLOCK <= MOBA_MAX_BLOCKS and tq % MOBA_BLOCK == 0
    return pl.pallas_call(
        functools.partial(_moba_kernel, tq=tq),
        grid=(b, s // tq),
        in_specs=[
            pl.BlockSpec((1, w, tq), lambda bi, i: (bi, 0, i)),
            pl.BlockSpec((1, s, k_aug.shape[2]), lambda bi, i: (bi, 0, 0)),
            pl.BlockSpec((1, w, s), lambda bi, i: (bi, 0, 0)),
            pl.BlockSpec((1, MOBA_HEADS, LANES, HEAD_DIM), lambda bi, i: (bi, 0, 0, 0)),
        ],
        out_specs=pl.BlockSpec((1, tq, w), lambda bi, i: (bi, i, 0)),
        out_shape=jax.ShapeDtypeStruct((b, s, w), BF16),
        scratch_shapes=[pltpu.VMEM((MOBA_HEADS, 1, tq), F32), pltpu.VMEM((MOBA_HEADS, 1, tq), F32),
                        pltpu.VMEM((w, tq), F32)],
        compiler_params=_params(("parallel", "parallel")),
    )(qt, k_aug, vt, kmean_mat)


def _swa_kernel(sinks_ref, slopes_ref, q_ref, k_ref, v_ref, o_ref, *, tq):
    i = pl.program_id(1)
    tk = tq + SWA_WINDOW
    start = pl.multiple_of(jnp.maximum(i * tq - SWA_WINDOW, 0), SWA_WINDOW)
    kw = k_ref[0, pl.ds(start, tk), :]
    vw = v_ref[0, pl.ds(start, tk), :]
    dist = ((i * tq + _iota((tq, tk), 0)) - (start + _iota((tq, tk), 1))).astype(F32)
    valid = (dist >= 0.0) & (dist < float(SWA_WINDOW))
    low_lanes = _iota((1, LANES), 1) < HEAD_DIM
    pairs = SWA_HEADS // SWA_KV_HEADS
    for p in range(pairs):
        qg = q_ref[0, :, p * LANES:(p + 1) * LANES]
        outs = []
        for half in range(SWA_KV_HEADS):
            head = p + pairs * half
            mine = low_lanes if half == 0 else jnp.logical_not(low_lanes)
            qm = jnp.where(mine, qg, jnp.zeros_like(qg))
            s = lax.dot_general(qm, kw, NT_DIMS, preferred_element_type=F32)
            s = jnp.where(valid, s - slopes_ref[head] * dist, NEG)
            sink = sinks_ref[head]
            m = jnp.maximum(jnp.max(s, axis=-1, keepdims=True), sink)
            pr = jnp.exp(s - m)
            den = jnp.sum(pr, axis=-1, keepdims=True) + jnp.exp(sink - m)
            outs.append(jnp.dot(pr.astype(BF16), vw, preferred_element_type=F32) / den)
        o_ref[0, :, p * LANES:(p + 1) * LANES] = jnp.where(low_lanes, outs[0], outs[1]).astype(BF16)


def _swa_attention(q, k, v, sinks, slopes, tq):
    b, s, w = q.shape
    wk = k.shape[2]
    smem = pl.BlockSpec(memory_space=pltpu.SMEM)
    return pl.pallas_call(
        functools.partial(_swa_kernel, tq=tq),
        grid=(b, s // tq),
        in_specs=[
            smem, smem,
            pl.BlockSpec((1, tq, w), lambda bi, i: (bi, i, 0)),
            pl.BlockSpec((1, s, wk), lambda bi, i: (bi, 0, 0)),
            pl.BlockSpec((1, s, wk), lambda bi, i: (bi, 0, 0)),
        ],
        out_specs=pl.BlockSpec((1, tq, w), lambda bi, i: (bi, i, 0)),
        out_shape=jax.ShapeDtypeStruct((b, s, w), BF16),
        compiler_params=_params(("parallel", "parallel")),
    )(sinks, slopes, q, k, v)


def _out_kernel(oa_ref, ob_ref, oc_ref, gates_ref, x_ref, g1_ref, sh_ref, sc_ref, gain_ref,
                wa_ref, wb_ref, wc_ref, wo_ref, wr_ref, br_ref,
                x1_ref, hx_ref, route_ref, *, d):
    ya = jnp.dot(oa_ref[0], wa_ref[...], preferred_element_type=F32)
    yb = jnp.dot(ob_ref[0], wb_ref[...], preferred_element_type=F32)
    yc = jnp.dot(oc_ref[0], wc_ref[...], preferred_element_type=F32)
    g = gates_ref[0]
    mix = (g[:, :d].astype(F32) * ya + g[:, d:2 * d].astype(F32) * yb + g[:, 2 * d:].astype(F32) * yc)
    y = jnp.dot(mix.astype(BF16), wo_ref[...], preferred_element_type=F32)
    x1 = x_ref[0] + g1_ref[0] * y
    x1_ref[0] = x1
    ms = jnp.mean(x1 * x1, axis=-1, keepdims=True)
    h = x1 * lax.rsqrt(ms + EPS) * gain_ref[...]
    h = h * (1.0 + sc_ref[0]) + sh_ref[0]
    hx_ref[0, :, :d] = h
    logits = jnp.dot(h, wr_ref[...], precision=HIGHEST, preferred_element_type=F32) + br_ref[...]
    tm = logits.shape[0]
    lanef = _iota((tm, LANES), 1).astype(F32)
    vals, idxs = [], []
    for _ in range(TOP_K):
        mx = jnp.max(logits, axis=-1, keepdims=True)
        first = jnp.min(jnp.where(logits == mx, lanef, float(LANES)), axis=-1, keepdims=True)
        vals.append(mx)
        idxs.append(first)
        logits = jnp.where(lanef == first, -jnp.inf, logits)
    exps = [jnp.exp(v - vals[0]) for v in vals]
    den = exps[0]
    for e in exps[1:]:
        den = den + e
    gates = jnp.zeros((tm, LANES), F32)
    info = jnp.zeros((tm, LANES), F32)
    for k in range(TOP_K):
        wk = exps[k] / den
        gates = jnp.where(lanef == idxs[k], wk, gates)
        info = jnp.where(lanef == float(k), idxs[k], info)
    hx_ref[0, :, d:] = gates
    route_ref[0] = info


def _out_proj(oa, ob, oc, gates, x, mod, norm_gain, l, wp, tm):
    b, s, d = x.shape
    tok = lambda bi, j: (bi, j, 0)
    stacked = [wp["w_a"], wp["w_b"], wp["w_c"], wp["w_o"], wp["w_r"], wp["b_r"]]
    return pl.pallas_call(
        functools.partial(_out_kernel, d=d),
        grid=(b, s // tm),
        in_specs=[
            pl.BlockSpec((1, tm, W_FOX), tok), pl.BlockSpec((1, tm, W_SWA), tok),
            pl.BlockSpec((1, tm, W_MOBA), tok), pl.BlockSpec((1, tm, 3 * d), tok),
            pl.BlockSpec((1, tm, d), tok),
            _mod_spec(mod, l, 2), _mod_spec(mod, l, 3), _mod_spec(mod, l, 4),
            pl.BlockSpec((None, None, 1, d), lambda bi, j: (l, 1, 0, 0)),
        ] + [_layer_spec(a, l) for a in stacked],
        out_specs=[pl.BlockSpec((1, tm, d), tok), pl.BlockSpec((1, tm, d + LANES), tok),
                   pl.BlockSpec((1, tm, LANES), tok)],
        out_shape=[jax.ShapeDtypeStruct((b, s, d), F32), jax.ShapeDtypeStruct((b, s, d + LANES), F32),
                   jax.ShapeDtypeStruct((b, s, LANES), F32)],
        compiler_params=_params(("parallel", "parallel")),
    )(oa, ob, oc, gates, x, mod, mod, mod, norm_gain, *stacked)


MOE_PAD_ROWS = 8


def _moe_kernel(tile_e_ref, tile_p0_ref, tile_off_ref, tile_n_ref, tok_ref,
                hx_ref, w1g_ref, w1l_ref, b1g_ref, b1l_ref, w2_ref, b2_ref,
                out_ref, xa_ref, xb_ref, ya_ref, yb_ref, *, slots, tm, tc, d):
    c = pl.program_id(0)
    i = pl.program_id(1)
    k = c * slots + i + 1

    def gather(kk, x_ref):
        p0 = tile_p0_ref[kk]
        for r in range(tm):
            tk = tok_ref[p0 + r]
            x_ref[r // 8, r % 8:r % 8 + 1, :] = hx_ref[0, pl.ds(tk, 1), :]

    def expert(x_ref, y_ref):
        xg = x_ref[...].reshape(tm, hx_ref.shape[2])
        xb = xg[:, :d].astype(BF16)
        lane = _iota((tm, LANES), 1)
        wcol = jnp.sum(jnp.where(lane == tile_e_ref[k], xg[:, d:], 0.0), axis=-1, keepdims=True)
        ug = jnp.dot(xb, w1g_ref[0], preferred_element_type=F32) + b1g_ref[0]
        ul = jnp.dot(xb, w1l_ref[0], preferred_element_type=F32) + b1l_ref[0]
        ug = jnp.minimum(ug, SWIGLU_LIMIT)
        ul = jnp.clip(ul, -SWIGLU_LIMIT, SWIGLU_LIMIT)
        act = ug * jax.nn.sigmoid(SWIGLU_ALPHA * ug) * (ul + 1.0)
        y = jnp.dot(act.astype(BF16), w2_ref[0], preferred_element_type=F32) + b2_ref[0]
        y_ref[...] = (y * wcol).reshape(y_ref.shape)

    def scatter(kk, y_ref):
        p0 = tile_p0_ref[kk]
        off = tile_off_ref[kk]
        end = off + tile_n_ref[kk]
        for r0 in range(0, tm, 4):
            toks = [jnp.where((r0 + u >= off) & (r0 + u < end), tok_ref[p0 + r0 + u], tc + u) for u in range(4)]
            rows = [out_ref[0, pl.ds(tk, 1), :] for tk in toks]
            for u in range(4):
                r = r0 + u
                out_ref[0, pl.ds(toks[u], 1), :] = rows[u] + y_ref[r // 8, r % 8:r % 8 + 1, :]

    @pl.when(i == 0)
    def _():
        out_ref[...] = jnp.zeros_like(out_ref)
        ya_ref[...] = jnp.zeros_like(ya_ref)
        yb_ref[...] = jnp.zeros_like(yb_ref)
        gather(k, xa_ref)

    busy = (tile_n_ref[k] > 0) | (tile_n_ref[k - 1] > 0)

    @pl.when(busy & (i % 2 == 0))
    def _():
        gather(k + 1, xb_ref)
        expert(xa_ref, ya_ref)
        scatter(k - 1, yb_ref)

    @pl.when(busy & (i % 2 == 1))
    def _():
        gather(k + 1, xa_ref)
        expert(xb_ref, yb_ref)
        scatter(k - 1, ya_ref)


def _moe(hx, tables, wp, e0, tc, tm):
    b, s, dx = hx.shape
    d = dx - LANES
    nc = (b * s) // tc
    tile_e = tables[0]
    slots = tile_e.shape[0] // nc
    f = wp["w1g"].shape[2]
    hx = hx.reshape(nc, tc, dx)
    chunk = lambda c, i, *prefetch: (c, 0, 0)
    expert = lambda c, i, te, *prefetch: (e0 + te[c * slots + i + 1], 0, 0)
    x_tile = pltpu.VMEM((tm // 8, 8, dx), F32)
    y_tile = pltpu.VMEM((tm // 8, 8, d), F32)
    out = pl.pallas_call(
        functools.partial(_moe_kernel, slots=slots, tm=tm, tc=tc, d=d),
        grid_spec=pltpu.PrefetchScalarGridSpec(
            num_scalar_prefetch=len(tables),
            grid=(nc, slots - 2),
            in_specs=[
                pl.BlockSpec((1, tc, dx), chunk, pipeline_mode=pl.Buffered(1)),
                pl.BlockSpec((1, d, f), expert), pl.BlockSpec((1, d, f), expert),
                pl.BlockSpec((1, 1, f), expert), pl.BlockSpec((1, 1, f), expert),
                pl.BlockSpec((1, f, d), expert), pl.BlockSpec((1, 1, d), expert),
            ],
            out_specs=pl.BlockSpec((1, tc + MOE_PAD_ROWS, d), chunk, pipeline_mode=pl.Buffered(1)),
            scratch_shapes=[x_tile, x_tile, y_tile, y_tile],
        ),
        out_shape=jax.ShapeDtypeStruct((nc, tc + MOE_PAD_ROWS, d), F32),
        compiler_params=_params(("parallel", "arbitrary")),
    )(*tables, hx, wp["w1g"], wp["w1l"], wp["b1g"], wp["b1l"], wp["w2"], wp["b2"])
    if tc == s:
        return out
    return out[:, :tc].reshape(b, s, d)


def _routing_tables(route, n_experts, tc, tm):
    b, s, _ = route.shape
    t = b * s
    nc = t // tc
    idx = route[:, :, :TOP_K].astype(jnp.int32).reshape(nc, tc, TOP_K)
    key = idx * tc + jnp.arange(tc, dtype=jnp.int32)[None, :, None]
    key = jnp.sort(key.reshape(nc, tc * TOP_K), axis=1)
    tok = key % tc
    e_sorted = key // tc
    experts = jnp.arange(n_experts, dtype=jnp.int32)
    cstart = jnp.sum(e_sorted[:, :, None] < experts[None, None, :], axis=1).astype(jnp.int32)
    cend = jnp.concatenate([cstart[:, 1:], jnp.full((nc, 1), tc * TOP_K, jnp.int32)], axis=1)
    count = cend - cstart
    tiles = (count + tm - 1) // tm
    tend = jnp.cumsum(tiles, axis=1)
    tstart = tend - tiles
    nt = (tc * TOP_K) // tm + n_experts
    slot = jnp.arange(-1, nt + 2, dtype=jnp.int32)
    total = tend[:, -1:]
    live = (slot[None, :] >= 0) & (slot[None, :] < total)
    slot_c = jnp.clip(slot[None, :], 0, total - 1)
    e_of = jnp.sum(slot_c[:, :, None] >= tend[:, None, :], axis=2).astype(jnp.int32)
    pick = lambda tbl: jnp.sum(jnp.where(e_of[:, :, None] == experts[None, None, :], tbl[:, None, :], 0), axis=2)
    jj = slot_c - pick(tstart)
    p0 = pick(cstart) + jj * tm
    n = jnp.where(live, jnp.clip(pick(count) - jj * tm, 0, tm), 0)
    p0 = p0 + (jnp.arange(nc, dtype=jnp.int32) * (tc * TOP_K))[:, None]
    first = jnp.minimum(p0, nc * tc * TOP_K - tm)
    flat = lambda a: a.reshape(-1).astype(jnp.int32)
    return flat(e_of), flat(first), flat(p0 - first), flat(n), flat(tok)


def _residual_kernel(x_ref, moe_ref, g_ref, o_ref):
    o_ref[0] = x_ref[0] + g_ref[0] * moe_ref[0]


def _residual(x, moe, mod, l, tm):
    b, s, d = x.shape
    tok = lambda bi, j: (bi, j, 0)
    return pl.pallas_call(
        _residual_kernel,
        grid=(b, s // tm),
        in_specs=[pl.BlockSpec((1, tm, d), tok), pl.BlockSpec((1, tm, d), tok), _mod_spec(mod, l, 5)],
        out_specs=pl.BlockSpec((1, tm, d), tok),
        out_shape=jax.ShapeDtypeStruct((b, s, d), F32),
        compiler_params=_params(("parallel", "parallel")),
    )(x, moe, mod)


def _split_kernel(w_ref, p_ref, g_ref, l_ref):
    half = MXU_DIM // 2
    for c in range(w_ref.shape[2] // MXU_DIM):
        w = w_ref[0, :, c * MXU_DIM:(c + 1) * MXU_DIM].astype(BF16)
        z = jnp.dot(w, p_ref[...], preferred_element_type=F32)
        g_ref[0, :, c * half:(c + 1) * half] = z[:, :half].astype(BF16)
        l_ref[0, :, c * half:(c + 1) * half] = z[:, half:].astype(BF16)


def _split_glu_weights(w):
    n, d, f2 = w.shape
    cw = min(f2, 4 * MXU_DIM)
    half = MXU_DIM // 2
    k = np.arange(half)
    perm = np.zeros((MXU_DIM, MXU_DIM), np.float32)
    perm[2 * k, k] = 1.0
    perm[2 * k + 1, half + k] = 1.0
    out = jax.ShapeDtypeStruct((n, d, f2 // 2), BF16)
    return pl.pallas_call(
        _split_kernel,
        grid=(n, f2 // cw),
        in_specs=[pl.BlockSpec((1, d, cw), lambda i, j: (i, 0, j)),
                  pl.BlockSpec((MXU_DIM, MXU_DIM), lambda i, j: (0, 0))],
        out_specs=[pl.BlockSpec((1, d, cw // 2), lambda i, j: (i, 0, j))] * 2,
        out_shape=[out, out],
        compiler_params=_params(("parallel", "parallel")),
    )(w, jnp.asarray(perm, BF16))


def _pad_heads(w, n_heads):
    lead = w.shape[:-1]
    w = w.reshape(lead + (n_heads, HEAD_DIM))
    w = jnp.concatenate([w, jnp.zeros_like(w)], axis=-1)
    return w.reshape(lead + (n_heads * LANES,))


def _prepare_weights(d, tm, slopes, w_in, b_fgate, qk_gain, w_br_fox, w_br_swa, w_br_moba, w_out,
                     w_router, b_router, w_exp1, b_exp1, w_exp2, b_exp2):
    depth = w_in.shape[0]
    n_exp = w_router.shape[2]
    splits = (W_FOX, W_FOX, W_FOX, FOX_HEADS, W_SWA, W_SWA_KV, W_SWA_KV, W_MOBA, W_MOBA, W_MOBA, 3 * d)
    pts = np.cumsum(splits)[:-1].tolist()
    qa, ka, va, fa, qb, kb, vb, qc, kc, vc, gt = jnp.split(w_in, pts, axis=2)
    order = np.asarray(SWA_HEAD_ORDER)
    qb = qb.reshape(depth, d, SWA_HEADS, HEAD_DIM)[:, :, order].reshape(depth, d, W_SWA)
    fa = jnp.pad(fa, ((0, 0), (0, 0), (0, LANES - FOX_HEADS)))
    w_cat = jnp.concatenate([_pad_heads(ka, FOX_HEADS), fa, qb, kb, vb, _pad_heads(kc, MOBA_HEADS), gt],
                            axis=2).astype(BF16)
    w_t = jnp.swapaxes(jnp.concatenate([qa, va, qc, vc], axis=2), 1, 2).astype(BF16)
    b_f = jnp.pad(b_fgate.astype(F32), ((0, 0), (0, LANES - FOX_HEADS)))[:, None, :]
    g = qk_gain.astype(F32)
    tile = lambda v, n: jnp.tile(v, (1, n))
    ones = lambda n: jnp.ones((depth, n), F32)
    qscale = HEAD_DIM ** -0.5
    gvec = jnp.concatenate([
        _pad_heads(tile(g[:, 1], FOX_HEADS), FOX_HEADS), ones(LANES),
        tile(g[:, 2], SWA_HEADS) * qscale, tile(g[:, 3], SWA_KV_HEADS), ones(W_SWA_KV),
        _pad_heads(tile(g[:, 5], MOBA_HEADS), MOBA_HEADS),
    ], axis=1)[:, None, :]
    gain_t = jnp.concatenate([tile(g[:, 0], FOX_HEADS), tile(g[:, 4], MOBA_HEADS)], axis=1) * (qscale * LOG2E)
    gain_t = jnp.broadcast_to(gain_t[:, :, None], (depth, W_FOX + W_MOBA, tm))
    heads = MXU_DIM // HEAD_DIM
    gsum = jnp.asarray(np.kron(np.eye(heads), np.ones((HEAD_DIM, HEAD_DIM))), BF16)
    tril = jnp.asarray(np.tril(np.ones((tm, tm))), F32)
    perm = np.zeros((N_PIECES * LANES, W_FOX_AUG), np.float32)
    for h in range(FOX_HEADS):
        for p in range(N_PIECES):
            perm[p * LANES + h, h * LANES + FOX_CUM_LANE + p] = 1.0
    slope_lanes = jnp.repeat(slopes[SWA_HEADS:] * LOG2E, LANES)[None, :]
    w_r = jnp.pad(w_router.astype(F32), ((0, 0), (0, 0), (0, LANES - n_exp)))
    b_r = jnp.pad(b_router.astype(F32), ((0, 0), (0, LANES - n_exp)), constant_values=NEG)[:, None, :]
    w_b = w_br_swa.reshape(depth, SWA_HEADS, HEAD_DIM, d)[:, order].reshape(depth, W_SWA, d)
    w1g, w1l = _split_glu_weights(w_exp1.reshape((depth * n_exp,) + w_exp1.shape[2:]))
    f = w_exp2.shape[2]
    b1 = b_exp1.reshape(depth * n_exp, 1, f, 2)
    return dict(
        w_cat=w_cat, w_t=w_t, b_f=b_f, gvec=gvec, gain_t=gain_t, gsum=gsum, tril=tril,
        perm=jnp.asarray(perm, BF16), slope_lanes=slope_lanes,
        w_a=w_br_fox.astype(BF16), w_b=w_b.astype(BF16), w_c=w_br_moba.astype(BF16),
        w_o=w_out.astype(BF16), w_r=w_r, b_r=b_r,
        w1g=w1g, w1l=w1l, b1g=b1[..., 0], b1l=b1[..., 1],
        w2=w_exp2.reshape(depth * n_exp, f, d).astype(BF16), b2=b_exp2.reshape(depth * n_exp, 1, d),
    )


def kernel(x, c, w_ada, b_ada, norm_gain, w_in, b_fgate, qk_gain, attn_sinks, w_br_fox, w_br_swa, w_br_moba, w_out, w_router, b_router, w_exp1, b_exp1, w_exp2, b_exp2):
    b, s, d = x.shape
    depth = w_ada.shape[0]
    n_experts = w_router.shape[2]
    tm = min(s, 512)
    tq_fox = min(s, 512)
    tq_swa = min(s, 256)
    tc = min(s, 4096)
    tm_moe = 256
    n_alibi = SWA_HEADS + MOBA_HEADS
    slopes = jnp.exp2(-8.0 * jnp.arange(1, n_alibi + 1, dtype=F32) / n_alibi)

    mod = _modulation(c, w_ada, b_ada).reshape(depth, b, 1, 6 * d)
    ng = norm_gain.astype(F32).reshape(depth, 2, 1, d)
    sinks = attn_sinks.astype(F32)
    wp = _prepare_weights(d, tm, slopes, w_in, b_fgate, qk_gain, w_br_fox, w_br_swa, w_br_moba, w_out,
                          w_router, b_router, w_exp1, b_exp1, w_exp2, b_exp2)
    moe = None
    for l in range(depth):
        outs = _in_proj(x, moe, mod, ng, l, wp, tm)
        qft, kfa, vft, qs, ks, vs, qmt, kma, vmt, kmean, gates = outs[:11]
        if moe is not None:
            x = outs[11]
        o_a = _fox_attention(qft, kfa, vft, tq_fox)
        o_b = _swa_attention(qs, ks, vs, sinks[l], slopes, tq_swa)
        nblk = s // MOBA_BLOCK
        km4 = kmean.reshape(b, nblk, MOBA_HEADS, LANES)[..., :HEAD_DIM]
        kmat = jnp.zeros((b, MOBA_HEADS, LANES, HEAD_DIM), F32)
        kmat = kmat.at[:, :, :nblk].set(jnp.transpose(km4, (0, 2, 1, 3)))
        o_c = _moba_attention(qmt, kma, vmt, kmat, tq_fox)
        x, hx, route = _out_proj(o_a, o_b, o_c, gates, x, mod, ng, l, wp, tm)
        tables = _routing_tables(route, n_experts, tc, tm_moe)
        moe = _moe(hx, tables, wp, l * n_experts, tc, tm_moe)
    return _residual(x, moe, mod, depth - 1, tm)
```

```python
import functools

import numpy as np
import jax
import jax.numpy as jnp
from jax import lax
from jax.experimental import pallas as pl
from jax.experimental.pallas import tpu as pltpu

HEAD_DIM = 64
FOX_HEADS = 4
SWA_HEADS = 8
SWA_KV_HEADS = 2
SWA_WINDOW = 128
MOBA_HEADS = 4
MOBA_BLOCK = 256
MOBA_TOPK = 3
TOP_K = 4
SWIGLU_ALPHA = 1.702
SWIGLU_LIMIT = 7.0
EPS = 1e-6
NEG = -1e30
LOG2E = 1.4426950408889634
MOBA_MASK_BIAS = -float(2 ** 30)

LANES = 128
MXU_DIM = 256
VMEM_LIMIT = 60 * 1024 * 1024

W_FOX = FOX_HEADS * HEAD_DIM
W_SWA = SWA_HEADS * HEAD_DIM
W_SWA_KV = SWA_KV_HEADS * HEAD_DIM
W_MOBA = MOBA_HEADS * HEAD_DIM
W_FOX_AUG = FOX_HEADS * LANES
FOX_CUM_LANE = HEAD_DIM
W_MOBA_AUG = MOBA_HEADS * LANES
MOBA_BLOCK_LANE = HEAD_DIM
MOBA_MAX_BLOCKS = 16
MOBA_ALIBI_LANE = MOBA_BLOCK_LANE + MOBA_MAX_BLOCKS
N_PIECES = 3

F32 = jnp.float32
BF16 = jnp.bfloat16
HIGHEST = lax.Precision.HIGHEST
NT_DIMS = (((1,), (1,)), ((), ()))


def _params(semantics):
    return pltpu.CompilerParams(dimension_semantics=semantics, vmem_limit_bytes=VMEM_LIMIT)


def _iota(shape, dim, dtype=jnp.int32):
    return lax.broadcasted_iota(dtype, shape, dim)


def _pieces(x):
    hi = x.astype(BF16).astype(F32)
    r = x - hi
    mid = r.astype(BF16).astype(F32)
    return hi, mid, r - mid


def _mod_kernel(c_ref, w_ref, b_ref, o_ref):
    c = c_ref[...]
    s = c * jax.nn.sigmoid(c)
    o_ref[0] = jnp.dot(s, w_ref[0], precision=HIGHEST, preferred_element_type=F32) + b_ref[0]


def _modulation(c, w_ada, b_ada):
    depth, d, n = w_ada.shape
    b = c.shape[0]
    tn = d
    return pl.pallas_call(
        _mod_kernel,
        grid=(depth, n // tn),
        in_specs=[
            pl.BlockSpec((b, d), lambda l, j: (0, 0)),
            pl.BlockSpec((1, d, tn), lambda l, j: (l, 0, j)),
            pl.BlockSpec((1, 1, tn), lambda l, j: (l, 0, j)),
        ],
        out_specs=pl.BlockSpec((1, b, tn), lambda l, j: (l, 0, j)),
        out_shape=jax.ShapeDtypeStruct((depth, b, n), F32),
        compiler_params=_params(("parallel", "parallel")),
    )(c, w_ada, b_ada.reshape(depth, 1, n))


def _head_norm(z, g_ref, gain):
    w = z.shape[1]
    cw = min(w, MXU_DIM)
    outs = []
    for c0 in range(0, w, cw):
        zz = z[:, c0:c0 + cw]
        ss = jnp.dot((zz * zz).astype(BF16), g_ref[:cw, :cw], preferred_element_type=F32)
        outs.append(zz * lax.rsqrt(ss * (1.0 / HEAD_DIM) + EPS))
    zn = outs[0] if len(outs) == 1 else jnp.concatenate(outs, axis=1)
    return zn * gain


def _head_norm_t(zt, gain_t):
    rows, cols = zt.shape
    z3 = zt.reshape(rows // HEAD_DIM, HEAD_DIM, cols)
    ms = jnp.mean(z3 * z3, axis=1, keepdims=True)
    return (z3 * lax.rsqrt(ms + EPS)).reshape(rows, cols) * gain_t


def _in_kernel(*refs, has_prev, tm, d):
    if has_prev:
        x_ref, moe_ref, g2_ref = refs[:3]
        refs = refs[3:]
    else:
        x_ref = refs[0]
        refs = refs[1:]
    (sh_ref, sc_ref, gain_ref, w_ref, wt_ref, bf_ref, gvec_ref, gt_ref, g_ref, tril_ref, perm_ref, slope_ref,
     sslope_ref,
     qft_ref, kfa_ref, vft_ref, qst_ref, ksa_ref, vst_ref, qmt_ref, kma_ref, vmt_ref,
     kmean_ref, gates_ref) = refs[:24]
    refs = refs[24:]
    if has_prev:
        xn_ref, carry_ref = refs
    else:
        (carry_ref,) = refs
    j = pl.program_id(1)

    x = x_ref[0]
    if has_prev:
        x = x + g2_ref[0] * moe_ref[0]
        xn_ref[0] = x
    ms = jnp.mean(x * x, axis=-1, keepdims=True)
    h = x * lax.rsqrt(ms + EPS) * gain_ref[...]
    h = h * (1.0 + sc_ref[0]) + sh_ref[0]
    hb = h.astype(BF16)

    def proj(c0, c1):
        return jnp.dot(hb, w_ref[:, c0:c1], preferred_element_type=F32)

    def gain(c0, c1):
        return gvec_ref[:, c0:c1]

    zt = lax.dot_general(wt_ref[...], hb, NT_DIMS, preferred_element_type=F32)
    qft_ref[0] = _head_norm_t(zt[:W_FOX], gt_ref[:W_FOX]).astype(BF16)
    vft_ref[0] = zt[W_FOX:2 * W_FOX].astype(BF16)
    o = 2 * W_FOX
    qmt_ref[0] = _head_norm_t(zt[o:o + W_MOBA], gt_ref[W_FOX:W_FOX + W_MOBA])
    vmt_ref[0] = zt[o + W_MOBA:o + 2 * W_MOBA].astype(BF16)
    o += 2 * W_MOBA
    qst_ref[0] = _head_norm_t(zt[o:o + W_SWA], gt_ref[W_FOX + W_MOBA:]).astype(BF16)
    vst_ref[0] = zt[o + W_SWA:o + W_SWA + W_SWA_KV].astype(BF16)

    @pl.when(j == 0)
    def _():
        carry_ref[...] = jnp.zeros_like(carry_ref)

    o = 0
    z = proj(o, o + W_FOX_AUG + LANES)
    kf = _head_norm(z[:, :W_FOX_AUG], g_ref, gain(o, o + W_FOX_AUG))
    fa = z[:, W_FOX_AUG:] + bf_ref[...]
    logf = -(jnp.maximum(-fa, 0.0) + jnp.log(1.0 + jnp.exp(-jnp.abs(fa))))
    cum = jnp.dot(tril_ref[...], logf, precision=HIGHEST, preferred_element_type=F32) + carry_ref[0:1, :]
    carry_ref[...] = jnp.broadcast_to(cum[tm - 1:tm, :], carry_ref.shape)
    parts = jnp.concatenate(_pieces(cum * (-LOG2E)), axis=1).astype(BF16)
    kfa_ref[0] = (kf + jnp.dot(parts, perm_ref[...], preferred_element_type=F32)).astype(BF16)
    o += W_FOX_AUG + LANES
    ksw = _head_norm(proj(o, o + W_SWA_KV), g_ref, gain(o, o + W_SWA_KV))
    lane = _iota((tm, LANES), 1)
    pos = (j * tm + _iota((tm, LANES), 0)).astype(F32)
    s_hi, s_mid, s_lo = _pieces(sslope_ref[...] * pos)
    piece = lane % N_PIECES
    alibi = jnp.where(piece == 0, s_hi, jnp.where(piece == 1, s_mid, s_lo))
    ksa_ref[0] = jnp.concatenate([ksw, alibi], axis=1).astype(BF16)
    o += W_SWA_KV
    z = proj(o, o + W_MOBA_AUG)
    kn = _head_norm(z, g_ref, gain(o, o + W_MOBA_AUG))
    nb = tm // MOBA_BLOCK
    kmean_ref[0] = jnp.mean(kn.reshape(nb, MOBA_BLOCK, W_MOBA_AUG), axis=1).reshape(nb, 1, W_MOBA_AUG)
    lane = _iota((tm, W_MOBA_AUG), 1) % LANES
    pos = j * tm + _iota((tm, W_MOBA_AUG), 0)
    a_hi, a_mid, a_lo = _pieces(slope_ref[...] * pos.astype(F32))
    ka = jnp.where(lane == MOBA_BLOCK_LANE + pos // MOBA_BLOCK, 1.0, kn)
    ka = jnp.where(lane == MOBA_ALIBI_LANE, a_hi, ka)
    ka = jnp.where(lane == MOBA_ALIBI_LANE + 1, a_mid, ka)
    ka = jnp.where(lane == MOBA_ALIBI_LANE + 2, a_lo, ka)
    kma_ref[0] = ka.astype(BF16)
    o += W_MOBA_AUG
    for br in range(3):
        zg = proj(o + br * d, o + (br + 1) * d)
        gates_ref[0, :, br * d:(br + 1) * d] = jax.nn.sigmoid(zg).astype(BF16)


def _layer_spec(a, l):
    return pl.BlockSpec((None,) + a.shape[1:], lambda *_: (l,) + (0,) * (a.ndim - 1))


def _const_spec(a):
    return pl.BlockSpec(a.shape, lambda *_: (0,) * a.ndim)


def _mod_spec(mod, l, k):
    d = mod.shape[3] // 6
    return pl.BlockSpec((None, 1, 1, d), lambda bi, j: (l, bi, 0, k))


def _in_proj(x, moe, mod, norm_gain, l, wp, tm):
    b, s, d = x.shape
    has_prev = moe is not None
    tok = lambda bi, j: (bi, j, 0)
    tok_t = lambda bi, j: (bi, 0, j)
    in_specs = [pl.BlockSpec((1, tm, d), tok)]
    args = [x]
    if has_prev:
        in_specs += [pl.BlockSpec((1, tm, d), tok), _mod_spec(mod, l - 1, 5)]
        args += [moe, mod]
    stacked = [wp["w_cat"], wp["w_t"], wp["b_f"], wp["gvec"], wp["gain_t"]]
    shared = [wp["gsum"], wp["tril"], wp["perm"], wp["slope_lanes"], wp["swa_slope_lanes"]]
    in_specs += [_mod_spec(mod, l, 0), _mod_spec(mod, l, 1),
                 pl.BlockSpec((None, None, 1, d), lambda bi, j: (l, 0, 0, 0))]
    in_specs += [_layer_spec(a, l) for a in stacked] + [_const_spec(a) for a in shared]
    args += [mod, mod, norm_gain] + stacked + shared
    nblk = s // MOBA_BLOCK
    out_shape = [
        jax.ShapeDtypeStruct((b, W_FOX, s), BF16), jax.ShapeDtypeStruct((b, s, W_FOX_AUG), BF16),
        jax.ShapeDtypeStruct((b, W_FOX, s), BF16),
        jax.ShapeDtypeStruct((b, W_SWA, s), BF16), jax.ShapeDtypeStruct((b, s, 2 * LANES), BF16),
        jax.ShapeDtypeStruct((b, W_SWA_KV, s), BF16),
        jax.ShapeDtypeStruct((b, W_MOBA, s), F32), jax.ShapeDtypeStruct((b, s, W_MOBA_AUG), BF16),
        jax.ShapeDtypeStruct((b, W_MOBA, s), BF16),
        jax.ShapeDtypeStruct((b, nblk, 1, W_MOBA_AUG), F32),
        jax.ShapeDtypeStruct((b, s, 3 * d), BF16),
    ]
    out_specs = [
        pl.BlockSpec((1, W_FOX, tm), tok_t), pl.BlockSpec((1, tm, W_FOX_AUG), tok),
        pl.BlockSpec((1, W_FOX, tm), tok_t),
        pl.BlockSpec((1, W_SWA, tm), tok_t), pl.BlockSpec((1, tm, 2 * LANES), tok),
        pl.BlockSpec((1, W_SWA_KV, tm), tok_t),
        pl.BlockSpec((1, W_MOBA, tm), tok_t), pl.BlockSpec((1, tm, W_MOBA_AUG), tok),
        pl.BlockSpec((1, W_MOBA, tm), tok_t),
        pl.BlockSpec((1, tm // MOBA_BLOCK, 1, W_MOBA_AUG), lambda bi, j: (bi, j, 0, 0)),
        pl.BlockSpec((1, tm, 3 * d), tok),
    ]
    if has_prev:
        out_shape.append(jax.ShapeDtypeStruct((b, s, d), F32))
        out_specs.append(pl.BlockSpec((1, tm, d), tok))
    return pl.pallas_call(
        functools.partial(_in_kernel, has_prev=has_prev, tm=tm, d=d),
        grid=(b, s // tm),
        in_specs=in_specs,
        out_specs=out_specs,
        out_shape=out_shape,
        scratch_shapes=[pltpu.VMEM((8, LANES), F32)],
        compiler_params=_params(("parallel", "arbitrary")),
    )(*args)


def _flash_t(i, tq, n_heads, scores, values, o_ref, m_ref, l_ref, acc_ref):
    m_ref[...] = jnp.full_like(m_ref, NEG)
    l_ref[...] = jnp.zeros_like(l_ref)
    acc_ref[...] = jnp.zeros_like(acc_ref)
    causal = (_iota((tq, tq), 0) <= _iota((tq, tq), 1))[None]

    def block(j, diagonal):
        s = jnp.stack([scores(j, h) for h in range(n_heads)])
        if diagonal:
            s = jnp.where(causal, s, NEG)
        m_old = m_ref[...]
        m_new = jnp.maximum(m_old, jnp.max(s, axis=1, keepdims=True))
        alpha = jnp.exp2(m_old - m_new)
        p = jnp.exp2(s - m_new)
        l_ref[...] = alpha * l_ref[...] + jnp.sum(p, axis=1, keepdims=True)
        m_ref[...] = m_new
        pb = p.astype(BF16)
        pv = jnp.stack([jnp.dot(values(j, h), pb[h], preferred_element_type=F32) for h in range(n_heads)])
        acc_ref[...] = acc_ref[...] * alpha + pv

    def body(j, carry):
        block(j, False)
        return carry

    lax.fori_loop(0, i, body, 0)
    block(i, True)
    out = acc_ref[...] * (1.0 / l_ref[...])
    o_ref[0] = jnp.transpose(out.reshape(n_heads * HEAD_DIM, tq)).astype(BF16)


def _fox_kernel(qt_ref, k_ref, vt_ref, o_ref, m_ref, l_ref, acc_ref, *, tq):
    i = pl.program_id(1)
    tail = jnp.where(_iota((LANES - HEAD_DIM, tq), 0) < N_PIECES, 1.0, 0.0).astype(BF16)
    qa = [jnp.concatenate([qt_ref[0, h * HEAD_DIM:(h + 1) * HEAD_DIM, :], tail], axis=0)
          for h in range(FOX_HEADS)]

    def scores(j, h):
        start = pl.multiple_of(j * tq, tq)
        return jnp.dot(k_ref[0, pl.ds(start, tq), h * LANES:(h + 1) * LANES], qa[h],
                       preferred_element_type=F32)

    def values(j, h):
        start = pl.multiple_of(j * tq, tq)
        return vt_ref[0, h * HEAD_DIM:(h + 1) * HEAD_DIM, pl.ds(start, tq)]

    _flash_t(i, tq, FOX_HEADS, scores, values, o_ref, m_ref, l_ref, acc_ref)


def _fox_attention(qt, k_aug, vt, tq):
    b, w, s = qt.shape
    return pl.pallas_call(
        functools.partial(_fox_kernel, tq=tq),
        grid=(b, s // tq),
        in_specs=[
            pl.BlockSpec((1, w, tq), lambda bi, i: (bi, 0, i)),
            pl.BlockSpec((1, s, k_aug.shape[2]), lambda bi, i: (bi, 0, 0)),
            pl.BlockSpec((1, w, s), lambda bi, i: (bi, 0, 0)),
        ],
        out_specs=pl.BlockSpec((1, tq, w), lambda bi, i: (bi, i, 0)),
        out_shape=jax.ShapeDtypeStruct((b, s, w), BF16),
        scratch_shapes=[pltpu.VMEM((FOX_HEADS, 1, tq), F32), pltpu.VMEM((FOX_HEADS, 1, tq), F32),
                        pltpu.VMEM((FOX_HEADS, HEAD_DIM, tq), F32)],
        compiler_params=_params(("parallel", "parallel")),
    )(qt, k_aug, vt)


def _moba_kernel(qt_ref, k_ref, vt_ref, kmean_ref, o_ref, m_ref, l_ref, acc_ref, *, tq):
    i = pl.program_id(1)
    row = _iota((LANES, tq), 0)
    rowf = row.astype(F32)
    own = (i * tq + _iota((LANES, tq), 1)) // MOBA_BLOCK
    past = row < own
    tail = jnp.where(_iota((LANES - MOBA_ALIBI_LANE, tq), 0) < N_PIECES, 1.0, 0.0)
    qa = []
    for h in range(MOBA_HEADS):
        qh = qt_ref[0, h * HEAD_DIM:(h + 1) * HEAD_DIM, :]
        gate = jnp.dot(kmean_ref[0, h], qh, precision=HIGHEST, preferred_element_type=F32)
        g = jnp.where(past, gate, NEG)
        chosen = jnp.zeros((LANES, tq), jnp.bool_)
        for _ in range(MOBA_TOPK):
            mx = jnp.max(g, axis=0, keepdims=True)
            first = jnp.min(jnp.where(g == mx, rowf, float(LANES)), axis=0, keepdims=True)
            pick = (rowf == first) & (mx > 0.5 * NEG)
            chosen = chosen | pick
            g = jnp.where(pick, NEG, g)
        bias = jnp.where(past & jnp.logical_not(chosen), MOBA_MASK_BIAS, 0.0)
        qa.append(jnp.concatenate([qh, bias[:MOBA_MAX_BLOCKS], tail], axis=0).astype(BF16))

    def scores(j, h):
        start = pl.multiple_of(j * tq, tq)
        return jnp.dot(k_ref[0, pl.ds(start, tq), h * LANES:(h + 1) * LANES], qa[h],
                       preferred_element_type=F32)

    def values(j, h):
        start = pl.multiple_of(j * tq, tq)
        return vt_ref[0, h * HEAD_DIM:(h + 1) * HEAD_DIM, pl.ds(start, tq)]

    _flash_t(i, tq, MOBA_HEADS, scores, values, o_ref, m_ref, l_ref, acc_ref)


def _moba_attention(qt, k_aug, vt, kmean_mat, tq):
    b, w, s = qt.shape
    assert s // MOBA_BLOCK <= MOBA_MAX_BLOCKS and tq % MOBA_BLOCK == 0
    return pl.pallas_call(
        functools.partial(_moba_kernel, tq=tq),
        grid=(b, s // tq),
        in_specs=[
            pl.BlockSpec((1, w, tq), lambda bi, i: (bi, 0, i)),
            pl.BlockSpec((1, s, k_aug.shape[2]), lambda bi, i: (bi, 0, 0)),
            pl.BlockSpec((1, w, s), lambda bi, i: (bi, 0, 0)),
            pl.BlockSpec((1, MOBA_HEADS, LANES, HEAD_DIM), lambda bi, i: (bi, 0, 0, 0)),
        ],
        out_specs=pl.BlockSpec((1, tq, w), lambda bi, i: (bi, i, 0)),
        out_shape=jax.ShapeDtypeStruct((b, s, w), BF16),
        scratch_shapes=[pltpu.VMEM((MOBA_HEADS, 1, tq), F32), pltpu.VMEM((MOBA_HEADS, 1, tq), F32),
                        pltpu.VMEM((MOBA_HEADS, HEAD_DIM, tq), F32)],
        compiler_params=_params(("parallel", "parallel")),
    )(qt, k_aug, vt, kmean_mat)


def _swa_kernel(sinks_ref, slopes_ref, qt_ref, k_ref, vt_ref, o_ref, *, tq):
    i = pl.program_id(1)
    tk = tq + SWA_WINDOW
    start = pl.multiple_of(jnp.maximum(i * tq - SWA_WINDOW, 0), SWA_WINDOW)
    kw = k_ref[0, pl.ds(start, tk), :]
    dist = (i * tq + _iota((tk, tq), 1)) - (start + _iota((tk, tq), 0))
    valid = (dist >= 0) & (dist < SWA_WINDOW)
    qpos = (i * tq + _iota((1, tq), 1)).astype(F32)
    group = SWA_HEADS // SWA_KV_HEADS
    aug_row = _iota((LANES, tq), 0)
    blank = jnp.zeros((HEAD_DIM, tq), BF16)
    scores, sinks = [], []
    for h in range(SWA_HEADS):
        qh = qt_ref[0, h * HEAD_DIM:(h + 1) * HEAD_DIM, :]
        ones = jnp.where((aug_row >= N_PIECES * h) & (aug_row < N_PIECES * (h + 1)), 1.0, 0.0).astype(BF16)
        qa = jnp.concatenate([qh, blank, ones] if h < group else [blank, qh, ones], axis=0)
        scores.append(jnp.dot(kw, qa, preferred_element_type=F32))
        sinks.append((sinks_ref[h] + slopes_ref[h] * qpos) * LOG2E)
    s = jnp.where(valid[None], jnp.stack(scores), NEG)
    sink = jnp.stack(sinks)
    m = jnp.maximum(jnp.max(s, axis=1, keepdims=True), sink)
    pr = jnp.exp2(s - m)
    den = jnp.sum(pr, axis=1, keepdims=True) + jnp.exp2(sink - m)
    pb = pr.astype(BF16)
    outs = [jnp.dot(vt_ref[0, (h // group) * HEAD_DIM:(h // group + 1) * HEAD_DIM, pl.ds(start, tk)], pb[h],
                    preferred_element_type=F32) for h in range(SWA_HEADS)]
    out = jnp.stack(outs) / den
    o_ref[0] = jnp.transpose(out.reshape(SWA_HEADS * HEAD_DIM, tq)).astype(BF16)


def _swa_attention(qt, k_aug, vt, sinks, slopes, tq):
    b, w, s = qt.shape
    smem = pl.BlockSpec(memory_space=pltpu.SMEM)
    return pl.pallas_call(
        functools.partial(_swa_kernel, tq=tq),
        grid=(b, s // tq),
        in_specs=[
            smem, smem,
            pl.BlockSpec((1, w, tq), lambda bi, i: (bi, 0, i)),
            pl.BlockSpec((1, s, k_aug.shape[2]), lambda bi, i: (bi, 0, 0)),
            pl.BlockSpec((1, vt.shape[1], s), lambda bi, i: (bi, 0, 0)),
        ],
        out_specs=pl.BlockSpec((1, tq, w), lambda bi, i: (bi, i, 0)),
        out_shape=jax.ShapeDtypeStruct((b, s, w), BF16),
        compiler_params=_params(("parallel", "parallel")),
    )(sinks, slopes, qt, k_aug, vt)


def _out_kernel(oa_ref, ob_ref, oc_ref, gates_ref, x_ref, g1_ref, sh_ref, sc_ref, gain_ref,
                wa_ref, wb_ref, wc_ref, wo_ref, wr_ref, br_ref,
                x1_ref, hx_ref, route_ref, *, d, parts):
    tm = x_ref.shape[1] // parts
    for part in range(parts):
        _out_rows(slice(part * tm, (part + 1) * tm), tm, d, oa_ref, ob_ref, oc_ref, gates_ref, x_ref, g1_ref,
                  sh_ref, sc_ref, gain_ref, wa_ref, wb_ref, wc_ref, wo_ref, wr_ref, br_ref,
                  x1_ref, hx_ref, route_ref)


def _out_rows(rows, tm, d, oa_ref, ob_ref, oc_ref, gates_ref, x_ref, g1_ref, sh_ref, sc_ref, gain_ref,
              wa_ref, wb_ref, wc_ref, wo_ref, wr_ref, br_ref, x1_ref, hx_ref, route_ref):
    ya = jnp.dot(oa_ref[0, rows, :], wa_ref[...], preferred_element_type=F32)
    yb = jnp.dot(ob_ref[0, rows, :], wb_ref[...], preferred_element_type=F32)
    yc = jnp.dot(oc_ref[0, rows, :], wc_ref[...], preferred_element_type=F32)
    g = gates_ref[0, rows, :]
    mix = (g[:, :d].astype(F32) * ya + g[:, d:2 * d].astype(F32) * yb + g[:, 2 * d:].astype(F32) * yc)
    y = jnp.dot(mix.astype(BF16), wo_ref[...], preferred_element_type=F32)
    x1 = x_ref[0, rows, :] + g1_ref[0] * y
    x1_ref[0, rows, :] = x1
    ms = jnp.mean(x1 * x1, axis=-1, keepdims=True)
    h = x1 * lax.rsqrt(ms + EPS) * gain_ref[...]
    h = h * (1.0 + sc_ref[0]) + sh_ref[0]
    hx_ref[0, rows, :d] = h
    h_hi = h.astype(BF16)
    h_lo = (h - h_hi.astype(F32)).astype(BF16)
    logits = (jnp.dot(h_hi, wr_ref[0], preferred_element_type=F32)
              + jnp.dot(h_lo, wr_ref[0], preferred_element_type=F32)
              + jnp.dot(h_hi, wr_ref[1], preferred_element_type=F32)) + br_ref[...]
    lanef = _iota((tm, LANES), 1).astype(F32)
    vals, idxs = [], []
    for _ in range(TOP_K):
        mx = jnp.max(logits, axis=-1, keepdims=True)
        first = jnp.min(jnp.where(logits == mx, lanef, float(LANES)), axis=-1, keepdims=True)
        vals.append(mx)
        idxs.append(first)
        logits = jnp.where(lanef == first, -jnp.inf, logits)
    exps = [jnp.exp(v - vals[0]) for v in vals]
    den = exps[0]
    for e in exps[1:]:
        den = den + e
    gates = jnp.zeros((tm, LANES), F32)
    info = jnp.zeros((tm, LANES), F32)
    for k in range(TOP_K):
        wk = exps[k] / den
        gates = jnp.where(lanef == idxs[k], wk, gates)
        info = jnp.where(lanef == float(k), idxs[k], info)
    hx_ref[0, rows, d:] = gates
    route_ref[0, rows, :] = info


def _out_proj(oa, ob, oc, gates, x, mod, norm_gain, l, wp, tm):
    b, s, d = x.shape
    tok = lambda bi, j: (bi, j, 0)
    stacked = [wp["w_a"], wp["w_b"], wp["w_c"], wp["w_o"], wp["w_r"], wp["b_r"]]
    return pl.pallas_call(
        functools.partial(_out_kernel, d=d, parts=4),
        grid=(b, s // tm),
        in_specs=[
            pl.BlockSpec((1, tm, W_FOX), tok), pl.BlockSpec((1, tm, W_SWA), tok),
            pl.BlockSpec((1, tm, W_MOBA), tok), pl.BlockSpec((1, tm, 3 * d), tok),
            pl.BlockSpec((1, tm, d), tok),
            _mod_spec(mod, l, 2), _mod_spec(mod, l, 3), _mod_spec(mod, l, 4),
            pl.BlockSpec((None, None, 1, d), lambda bi, j: (l, 1, 0, 0)),
        ] + [_layer_spec(a, l) for a in stacked],
        out_specs=[pl.BlockSpec((1, tm, d), tok), pl.BlockSpec((1, tm, d + LANES), tok),
                   pl.BlockSpec((1, tm, LANES), tok)],
        out_shape=[jax.ShapeDtypeStruct((b, s, d), F32), jax.ShapeDtypeStruct((b, s, d + LANES), F32),
                   jax.ShapeDtypeStruct((b, s, LANES), F32)],
        compiler_params=_params(("parallel", "parallel")),
    )(oa, ob, oc, gates, x, mod, mod, mod, norm_gain, *stacked)


MOE_PAD_ROWS = 8


def _moe_kernel(tile_e_ref, tile_p0_ref, tile_off_ref, tile_n_ref, tok_ref,
                hx_ref, w1g_ref, w1l_ref, b1g_ref, b1l_ref, w2_ref, b2_ref,
                out_ref, xa_ref, xb_ref, ya_ref, yb_ref, *, slots, tm, tc, d):
    c = pl.program_id(0)
    i = pl.program_id(1)
    k = c * slots + i + 1

    def gather(kk, x_ref):
        p0 = tile_p0_ref[kk]
        for r in range(tm):
            tk = tok_ref[p0 + r]
            x_ref[r // 8, r % 8:r % 8 + 1, :] = hx_ref[0, pl.ds(tk, 1), :]

    def expert(x_ref, y_ref):
        xg = x_ref[...].reshape(tm, hx_ref.shape[2])
        xb = xg[:, :d].astype(BF16)
        lane = _iota((tm, LANES), 1)
        wcol = jnp.sum(jnp.where(lane == tile_e_ref[k], xg[:, d:], 0.0), axis=-1, keepdims=True)
        ug = jnp.dot(xb, w1g_ref[0], preferred_element_type=F32) + b1g_ref[0]
        ul = jnp.dot(xb, w1l_ref[0], preferred_element_type=F32) + b1l_ref[0]
        ug = jnp.minimum(ug, SWIGLU_LIMIT)
        ul = jnp.clip(ul, -SWIGLU_LIMIT, SWIGLU_LIMIT)
        act = ug * jax.nn.sigmoid(SWIGLU_ALPHA * ug) * (ul + 1.0)
        y = jnp.dot(act.astype(BF16), w2_ref[0], preferred_element_type=F32) + b2_ref[0]
        y_ref[...] = (y * wcol).reshape(y_ref.shape)

    def scatter(kk, y_ref):
        p0 = tile_p0_ref[kk]
        off = tile_off_ref[kk]
        end = off + tile_n_ref[kk]
        for r0 in range(0, tm, 4):
            toks = [jnp.where((r0 + u >= off) & (r0 + u < end), tok_ref[p0 + r0 + u], tc + u) for u in range(4)]
            rows = [out_ref[0, pl.ds(tk, 1), :] for tk in toks]
            for u in range(4):
                r = r0 + u
                out_ref[0, pl.ds(toks[u], 1), :] = rows[u] + y_ref[r // 8, r % 8:r % 8 + 1, :]

    @pl.when(i == 0)
    def _():
        out_ref[...] = jnp.zeros_like(out_ref)
        ya_ref[...] = jnp.zeros_like(ya_ref)
        yb_ref[...] = jnp.zeros_like(yb_ref)
        gather(k, xa_ref)

    busy = (tile_n_ref[k] > 0) | (tile_n_ref[k - 1] > 0)

    @pl.when(busy & (i % 2 == 0))
    def _():
        gather(k + 1, xb_ref)
        expert(xa_ref, ya_ref)
        scatter(k - 1, yb_ref)

    @pl.when(busy & (i % 2 == 1))
    def _():
        gather(k + 1, xa_ref)
        expert(xb_ref, yb_ref)
        scatter(k - 1, ya_ref)


def _moe(hx, tables, wp, e0, tc, tm):
    b, s, dx = hx.shape
    d = dx - LANES
    nc = (b * s) // tc
    tile_e = tables[0]
    slots = tile_e.shape[0] // nc
    f = wp["w1g"].shape[2]
    hx = hx.reshape(nc, tc, dx)
    chunk = lambda c, i, *prefetch: (c, 0, 0)
    expert = lambda c, i, te, *prefetch: (e0 + te[c * slots + i + 1], 0, 0)
    x_tile = pltpu.VMEM((tm // 8, 8, dx), F32)
    y_tile = pltpu.VMEM((tm // 8, 8, d), F32)
    out = pl.pallas_call(
        functools.partial(_moe_kernel, slots=slots, tm=tm, tc=tc, d=d),
        grid_spec=pltpu.PrefetchScalarGridSpec(
            num_scalar_prefetch=len(tables),
            grid=(nc, slots - 2),
            in_specs=[
                pl.BlockSpec((1, tc, dx), chunk, pipeline_mode=pl.Buffered(1)),
                pl.BlockSpec((1, d, f), expert), pl.BlockSpec((1, d, f), expert),
                pl.BlockSpec((1, 1, f), expert), pl.BlockSpec((1, 1, f), expert),
                pl.BlockSpec((1, f, d), expert), pl.BlockSpec((1, 1, d), expert),
            ],
            out_specs=pl.BlockSpec((1, tc + MOE_PAD_ROWS, d), chunk, pipeline_mode=pl.Buffered(1)),
            scratch_shapes=[x_tile, x_tile, y_tile, y_tile],
        ),
        out_shape=jax.ShapeDtypeStruct((nc, tc + MOE_PAD_ROWS, d), F32),
        compiler_params=_params(("parallel", "arbitrary")),
    )(*tables, hx, wp["w1g"], wp["w1l"], wp["b1g"], wp["b1l"], wp["w2"], wp["b2"])
    if tc == s:
        return out
    return out[:, :tc].reshape(b, s, d)


def _routing_tables(route, n_experts, tc, tm):
    b, s, _ = route.shape
    t = b * s
    nc = t // tc
    idx = route[:, :, :TOP_K].astype(jnp.int32).reshape(nc, tc, TOP_K)
    key = idx * tc + jnp.arange(tc, dtype=jnp.int32)[None, :, None]
    key = jnp.sort(key.reshape(nc, tc * TOP_K), axis=1)
    tok = key % tc
    e_sorted = key // tc
    experts = jnp.arange(n_experts, dtype=jnp.int32)
    cstart = jnp.sum(e_sorted[:, :, None] < experts[None, None, :], axis=1).astype(jnp.int32)
    cend = jnp.concatenate([cstart[:, 1:], jnp.full((nc, 1), tc * TOP_K, jnp.int32)], axis=1)
    count = cend - cstart
    tiles = (count + tm - 1) // tm
    tend = jnp.cumsum(tiles, axis=1)
    tstart = tend - tiles
    nt = (tc * TOP_K) // tm + n_experts
    slot = jnp.arange(-1, nt + 2, dtype=jnp.int32)
    total = tend[:, -1:]
    live = (slot[None, :] >= 0) & (slot[None, :] < total)
    slot_c = jnp.clip(slot[None, :], 0, total - 1)
    e_of = jnp.sum(slot_c[:, :, None] >= tend[:, None, :], axis=2).astype(jnp.int32)
    pick = lambda tbl: jnp.sum(jnp.where(e_of[:, :, None] == experts[None, None, :], tbl[:, None, :], 0), axis=2)
    jj = slot_c - pick(tstart)
    p0 = pick(cstart) + jj * tm
    n = jnp.where(live, jnp.clip(pick(count) - jj * tm, 0, tm), 0)
    p0 = p0 + (jnp.arange(nc, dtype=jnp.int32) * (tc * TOP_K))[:, None]
    first = jnp.minimum(p0, nc * tc * TOP_K - tm)
    flat = lambda a: a.reshape(-1).astype(jnp.int32)
    return flat(e_of), flat(first), flat(p0 - first), flat(n), flat(tok)


def _residual_kernel(x_ref, moe_ref, g_ref, o_ref):
    o_ref[0] = x_ref[0] + g_ref[0] * moe_ref[0]


def _residual(x, moe, mod, l, tm):
    b, s, d = x.shape
    tok = lambda bi, j: (bi, j, 0)
    return pl.pallas_call(
        _residual_kernel,
        grid=(b, s // tm),
        in_specs=[pl.BlockSpec((1, tm, d), tok), pl.BlockSpec((1, tm, d), tok), _mod_spec(mod, l, 5)],
        out_specs=pl.BlockSpec((1, tm, d), tok),
        out_shape=jax.ShapeDtypeStruct((b, s, d), F32),
        compiler_params=_params(("parallel", "parallel")),
    )(x, moe, mod)


def _split_kernel(w_ref, p_ref, g_ref, l_ref):
    half = MXU_DIM // 2
    for c in range(w_ref.shape[2] // MXU_DIM):
        w = w_ref[0, :, c * MXU_DIM:(c + 1) * MXU_DIM].astype(BF16)
        z = jnp.dot(w, p_ref[...], preferred_element_type=F32)
        g_ref[0, :, c * half:(c + 1) * half] = z[:, :half].astype(BF16)
        l_ref[0, :, c * half:(c + 1) * half] = z[:, half:].astype(BF16)


def _split_glu_weights(w):
    n, d, f2 = w.shape
    cw = min(f2, 4 * MXU_DIM)
    half = MXU_DIM // 2
    k = np.arange(half)
    perm = np.zeros((MXU_DIM, MXU_DIM), np.float32)
    perm[2 * k, k] = 1.0
    perm[2 * k + 1, half + k] = 1.0
    out = jax.ShapeDtypeStruct((n, d, f2 // 2), BF16)
    return pl.pallas_call(
        _split_kernel,
        grid=(n, f2 // cw),
        in_specs=[pl.BlockSpec((1, d, cw), lambda i, j: (i, 0, j)),
                  pl.BlockSpec((MXU_DIM, MXU_DIM), lambda i, j: (0, 0))],
        out_specs=[pl.BlockSpec((1, d, cw // 2), lambda i, j: (i, 0, j))] * 2,
        out_shape=[out, out],
        compiler_params=_params(("parallel", "parallel")),
    )(w, jnp.asarray(perm, BF16))


def _pad_heads(w, n_heads):
    lead = w.shape[:-1]
    w = w.reshape(lead + (n_heads, HEAD_DIM))
    w = jnp.concatenate([w, jnp.zeros_like(w)], axis=-1)
    return w.reshape(lead + (n_heads * LANES,))


def _prepare_weights(d, tm, slopes, w_in, b_fgate, qk_gain, w_br_fox, w_br_swa, w_br_moba, w_out,
                     w_router, b_router, w_exp1, b_exp1, w_exp2, b_exp2):
    depth = w_in.shape[0]
    n_exp = w_router.shape[2]
    splits = (W_FOX, W_FOX, W_FOX, FOX_HEADS, W_SWA, W_SWA_KV, W_SWA_KV, W_MOBA, W_MOBA, W_MOBA, 3 * d)
    pts = np.cumsum(splits)[:-1].tolist()
    qa, ka, va, fa, qb, kb, vb, qc, kc, vc, gt = jnp.split(w_in, pts, axis=2)
    fa = jnp.pad(fa, ((0, 0), (0, 0), (0, LANES - FOX_HEADS)))
    w_cat = jnp.concatenate([_pad_heads(ka, FOX_HEADS), fa, kb, _pad_heads(kc, MOBA_HEADS), gt],
                            axis=2).astype(BF16)
    w_t = jnp.swapaxes(jnp.concatenate([qa, va, qc, vc, qb, vb], axis=2), 1, 2).astype(BF16)
    b_f = jnp.pad(b_fgate.astype(F32), ((0, 0), (0, LANES - FOX_HEADS)))[:, None, :]
    g = qk_gain.astype(F32)
    tile = lambda v, n: jnp.tile(v, (1, n))
    ones = lambda n: jnp.ones((depth, n), F32)
    qscale = HEAD_DIM ** -0.5
    gvec = jnp.concatenate([
        _pad_heads(tile(g[:, 1], FOX_HEADS), FOX_HEADS), ones(LANES),
        tile(g[:, 3], SWA_KV_HEADS),
        _pad_heads(tile(g[:, 5], MOBA_HEADS), MOBA_HEADS),
    ], axis=1)[:, None, :]
    gain_t = jnp.concatenate([tile(g[:, 0], FOX_HEADS), tile(g[:, 4], MOBA_HEADS), tile(g[:, 2], SWA_HEADS)],
                             axis=1) * (qscale * LOG2E)
    gain_t = jnp.broadcast_to(gain_t[:, :, None], (depth, W_FOX + W_MOBA + W_SWA, tm))
    heads = MXU_DIM // HEAD_DIM
    gsum = jnp.asarray(np.kron(np.eye(heads), np.ones((HEAD_DIM, HEAD_DIM))), BF16)
    tril = jnp.asarray(np.tril(np.ones((tm, tm))), F32)
    perm = np.zeros((N_PIECES * LANES, W_FOX_AUG), np.float32)
    for h in range(FOX_HEADS):
        for p in range(N_PIECES):
            perm[p * LANES + h, h * LANES + FOX_CUM_LANE + p] = 1.0
    slope_lanes = jnp.repeat(slopes[SWA_HEADS:] * LOG2E, LANES)[None, :]
    swa_slope_lanes = jnp.pad(jnp.repeat(slopes[:SWA_HEADS] * LOG2E, N_PIECES),
                              (0, LANES - N_PIECES * SWA_HEADS))[None, :]
    w_r = jnp.pad(w_router.astype(F32), ((0, 0), (0, 0), (0, LANES - n_exp)))
    w_r_hi = w_r.astype(BF16)
    w_r = jnp.stack([w_r_hi, (w_r - w_r_hi.astype(F32)).astype(BF16)], axis=1)
    b_r = jnp.pad(b_router.astype(F32), ((0, 0), (0, LANES - n_exp)), constant_values=NEG)[:, None, :]
    w1g, w1l = _split_glu_weights(w_exp1.reshape((depth * n_exp,) + w_exp1.shape[2:]))
    f = w_exp2.shape[2]
    b1 = b_exp1.reshape(depth * n_exp, 1, f, 2)
    return dict(
        w_cat=w_cat, w_t=w_t, b_f=b_f, gvec=gvec, gain_t=gain_t, gsum=gsum, tril=tril,
        perm=jnp.asarray(perm, BF16), slope_lanes=slope_lanes, swa_slope_lanes=swa_slope_lanes,
        w_a=w_br_fox.astype(BF16), w_b=w_br_swa.astype(BF16), w_c=w_br_moba.astype(BF16),
        w_o=w_out.astype(BF16), w_r=w_r, b_r=b_r,
        w1g=w1g, w1l=w1l, b1g=b1[..., 0], b1l=b1[..., 1],
        w2=w_exp2.reshape(depth * n_exp, f, d).astype(BF16), b2=b_exp2.reshape(depth * n_exp, 1, d),
    )


def kernel(x, c, w_ada, b_ada, norm_gain, w_in, b_fgate, qk_gain, attn_sinks, w_br_fox, w_br_swa, w_br_moba, w_out, w_router, b_router, w_exp1, b_exp1, w_exp2, b_exp2):
    b, s, d = x.shape
    depth = w_ada.shape[0]
    n_experts = w_router.shape[2]
    tm = min(s, 512)
    tq_fox = min(s, 512)
    tq_swa = min(s, 256)
    tc = min(s, 4096)
    tm_moe = 256
    n_alibi = SWA_HEADS + MOBA_HEADS
    slopes = jnp.exp2(-8.0 * jnp.arange(1, n_alibi + 1, dtype=F32) / n_alibi)

    mod = _modulation(c, w_ada, b_ada).reshape(depth, b, 1, 6 * d)
    ng = norm_gain.astype(F32).reshape(depth, 2, 1, d)
    sinks = attn_sinks.astype(F32)
    wp = _prepare_weights(d, tm, slopes, w_in, b_fgate, qk_gain, w_br_fox, w_br_swa, w_br_moba, w_out,
                          w_router, b_router, w_exp1, b_exp1, w_exp2, b_exp2)
    moe = None
    for l in range(depth):
        outs = _in_proj(x, moe, mod, ng, l, wp, tm)
        qft, kfa, vft, qst, ksa, vst, qmt, kma, vmt, kmean, gates = outs[:11]
        if moe is not None:
            x = outs[11]
        o_a = _fox_attention(qft, kfa, vft, tq_fox)
        o_b = _swa_attention(qst, ksa, vst, sinks[l], slopes, tq_swa)
        nblk = s // MOBA_BLOCK
        km4 = kmean.reshape(b, nblk, MOBA_HEADS, LANES)[..., :HEAD_DIM]
        kmat = jnp.zeros((b, MOBA_HEADS, LANES, HEAD_DIM), F32)
        kmat = kmat.at[:, :, :nblk].set(jnp.transpose(km4, (0, 2, 1, 3)))
        o_c = _moba_attention(qmt, kma, vmt, kmat, tq_fox)
        x, hx, route = _out_proj(o_a, o_b, o_c, gates, x, mod, ng, l, wp, tm)
        tables = _routing_tables(route, n_experts, tc, tm_moe)
        moe = _moe(hx, tables, wp, l * n_experts, tc, tm_moe)
    return _residual(x, moe, mod, depth - 1, tm)
```

```python
import functools

import numpy as np
import jax
import jax.numpy as jnp
from jax import lax
from jax.experimental import pallas as pl
from jax.experimental.pallas import tpu as pltpu

HEAD_DIM = 64
FOX_HEADS = 4
SWA_HEADS = 8
SWA_KV_HEADS = 2
SWA_WINDOW = 128
MOBA_HEADS = 4
MOBA_BLOCK = 256
MOBA_TOPK = 3
TOP_K = 4
SWIGLU_ALPHA = 1.702
SWIGLU_LIMIT = 7.0
EPS = 1e-6
NEG = -1e30
LOG2E = 1.4426950408889634
MOBA_MASK_BIAS = -float(2 ** 30)

LANES = 128
MXU_DIM = 256
VMEM_LIMIT = 60 * 1024 * 1024

W_FOX = FOX_HEADS * HEAD_DIM
W_SWA = SWA_HEADS * HEAD_DIM
W_SWA_KV = SWA_KV_HEADS * HEAD_DIM
W_MOBA = MOBA_HEADS * HEAD_DIM
W_FOX_AUG = FOX_HEADS * LANES
FOX_CUM_LANE = HEAD_DIM
W_MOBA_AUG = MOBA_HEADS * LANES
MOBA_BLOCK_LANE = HEAD_DIM
MOBA_MAX_BLOCKS = 16
MOBA_ALIBI_LANE = MOBA_BLOCK_LANE + MOBA_MAX_BLOCKS
N_PIECES = 3
SUM_ROWS = 16

F32 = jnp.float32
BF16 = jnp.bfloat16
HIGHEST = lax.Precision.HIGHEST
NT_DIMS = (((1,), (1,)), ((), ()))


def _params(semantics):
    return pltpu.CompilerParams(dimension_semantics=semantics, vmem_limit_bytes=VMEM_LIMIT)


def _iota(shape, dim, dtype=jnp.int32):
    return lax.broadcasted_iota(dtype, shape, dim)


def _pieces(x):
    hi = x.astype(BF16).astype(F32)
    r = x - hi
    mid = r.astype(BF16).astype(F32)
    return hi, mid, r - mid


def _mod_kernel(c_ref, w_ref, b_ref, o_ref):
    c = c_ref[...]
    s = c * jax.nn.sigmoid(c)
    o_ref[0] = jnp.dot(s, w_ref[0], precision=HIGHEST, preferred_element_type=F32) + b_ref[0]


def _modulation(c, w_ada, b_ada):
    depth, d, n = w_ada.shape
    b = c.shape[0]
    tn = d
    return pl.pallas_call(
        _mod_kernel,
        grid=(depth, n // tn),
        in_specs=[
            pl.BlockSpec((b, d), lambda l, j: (0, 0)),
            pl.BlockSpec((1, d, tn), lambda l, j: (l, 0, j)),
            pl.BlockSpec((1, 1, tn), lambda l, j: (l, 0, j)),
        ],
        out_specs=pl.BlockSpec((1, b, tn), lambda l, j: (l, 0, j)),
        out_shape=jax.ShapeDtypeStruct((depth, b, n), F32),
        compiler_params=_params(("parallel", "parallel")),
    )(c, w_ada, b_ada.reshape(depth, 1, n))


def _head_norm(z, g_ref, gain):
    w = z.shape[1]
    cw = min(w, MXU_DIM)
    outs = []
    for c0 in range(0, w, cw):
        zz = z[:, c0:c0 + cw]
        ss = jnp.dot((zz * zz).astype(BF16), g_ref[:cw, :cw], preferred_element_type=F32)
        outs.append(zz * lax.rsqrt(ss * (1.0 / HEAD_DIM) + EPS))
    zn = outs[0] if len(outs) == 1 else jnp.concatenate(outs, axis=1)
    return zn * gain


def _head_norm_t(zt, gain_t):
    rows, cols = zt.shape
    z3 = zt.reshape(rows // HEAD_DIM, HEAD_DIM, cols)
    ms = jnp.mean(z3 * z3, axis=1, keepdims=True)
    return (z3 * lax.rsqrt(ms + EPS)).reshape(rows, cols) * gain_t


def _in_kernel(*refs, has_prev, tm, d):
    if has_prev:
        x_ref, moe_ref, g2_ref = refs[:3]
        refs = refs[3:]
    else:
        x_ref = refs[0]
        refs = refs[1:]
    (sh_ref, sc_ref, gain_ref, w_ref, wt_ref, bf_ref, gvec_ref, gt_ref, g_ref, tril_ref, perm_ref, slope_ref,
     sslope_ref,
     qft_ref, kfa_ref, vft_ref, qst_ref, ksa_ref, vst_ref, qmt_ref, kma_ref, vmt_ref,
     kmean_ref, gates_ref) = refs[:24]
    refs = refs[24:]
    if has_prev:
        xn_ref, carry_ref = refs
    else:
        (carry_ref,) = refs
    j = pl.program_id(1)

    x = x_ref[0]
    if has_prev:
        x = x + g2_ref[0] * moe_ref[0]
        xn_ref[0] = x
    ms = jnp.mean(x * x, axis=-1, keepdims=True)
    h = x * lax.rsqrt(ms + EPS) * gain_ref[...]
    h = h * (1.0 + sc_ref[0]) + sh_ref[0]
    hb = h.astype(BF16)

    def proj(c0, c1):
        return jnp.dot(hb, w_ref[:, c0:c1], preferred_element_type=F32)

    def gain(c0, c1):
        return gvec_ref[:, c0:c1]

    zt = lax.dot_general(wt_ref[...], hb, NT_DIMS, preferred_element_type=F32)
    qft_ref[0] = _head_norm_t(zt[:W_FOX], gt_ref[:W_FOX]).astype(BF16)
    vft_ref[0] = zt[W_FOX:2 * W_FOX].astype(BF16)
    o = 2 * W_FOX
    qmt_ref[0] = _head_norm_t(zt[o:o + W_MOBA], gt_ref[W_FOX:W_FOX + W_MOBA])
    vmt_ref[0] = zt[o + W_MOBA:o + 2 * W_MOBA].astype(BF16)
    o += 2 * W_MOBA
    qst_ref[0] = _head_norm_t(zt[o:o + W_SWA], gt_ref[W_FOX + W_MOBA:]).astype(BF16)
    vst_ref[0] = zt[o + W_SWA:o + W_SWA + W_SWA_KV].astype(BF16)

    @pl.when(j == 0)
    def _():
        carry_ref[...] = jnp.zeros_like(carry_ref)

    o = 0
    z = proj(o, o + W_FOX_AUG + LANES)
    kf = _head_norm(z[:, :W_FOX_AUG], g_ref, gain(o, o + W_FOX_AUG))
    fa = z[:, W_FOX_AUG:] + bf_ref[...]
    logf = -(jnp.maximum(-fa, 0.0) + jnp.log(1.0 + jnp.exp(-jnp.abs(fa))))
    cum = jnp.dot(tril_ref[...], logf, precision=HIGHEST, preferred_element_type=F32) + carry_ref[0:1, :]
    carry_ref[...] = jnp.broadcast_to(cum[tm - 1:tm, :], carry_ref.shape)
    parts = jnp.concatenate(_pieces(cum * (-LOG2E)), axis=1).astype(BF16)
    kfa_ref[0] = (kf + jnp.dot(parts, perm_ref[...], preferred_element_type=F32)).astype(BF16)
    o += W_FOX_AUG + LANES
    ksw = _head_norm(proj(o, o + W_SWA_KV), g_ref, gain(o, o + W_SWA_KV))
    lane = _iota((tm, LANES), 1)
    pos = (j * tm + _iota((tm, LANES), 0)).astype(F32)
    s_hi, s_mid, s_lo = _pieces(sslope_ref[...] * pos)
    piece = lane % N_PIECES
    alibi = jnp.where(piece == 0, s_hi, jnp.where(piece == 1, s_mid, s_lo))
    ksa_ref[0] = jnp.concatenate([ksw, alibi], axis=1).astype(BF16)
    o += W_SWA_KV
    z = proj(o, o + W_MOBA_AUG)
    kn = _head_norm(z, g_ref, gain(o, o + W_MOBA_AUG))
    nb = tm // MOBA_BLOCK
    kmean_ref[0] = jnp.mean(kn.reshape(nb, MOBA_BLOCK, W_MOBA_AUG), axis=1).reshape(nb, 1, W_MOBA_AUG)
    lane = _iota((tm, W_MOBA_AUG), 1) % LANES
    pos = j * tm + _iota((tm, W_MOBA_AUG), 0)
    a_hi, a_mid, a_lo = _pieces(slope_ref[...] * pos.astype(F32))
    ka = jnp.where(lane == MOBA_BLOCK_LANE + pos // MOBA_BLOCK, 1.0, kn)
    ka = jnp.where(lane == MOBA_ALIBI_LANE, a_hi, ka)
    ka = jnp.where(lane == MOBA_ALIBI_LANE + 1, a_mid, ka)
    ka = jnp.where(lane == MOBA_ALIBI_LANE + 2, a_lo, ka)
    kma_ref[0] = ka.astype(BF16)
    o += W_MOBA_AUG
    for br in range(3):
        zg = proj(o + br * d, o + (br + 1) * d)
        gates_ref[0, :, br * d:(br + 1) * d] = jax.nn.sigmoid(zg).astype(BF16)


def _layer_spec(a, l):
    return pl.BlockSpec((None,) + a.shape[1:], lambda *_: (l,) + (0,) * (a.ndim - 1))


def _const_spec(a):
    return pl.BlockSpec(a.shape, lambda *_: (0,) * a.ndim)


def _mod_spec(mod, l, k):
    d = mod.shape[3] // 6
    return pl.BlockSpec((None, 1, 1, d), lambda bi, j: (l, bi, 0, k))


def _in_proj(x, moe, mod, norm_gain, l, wp, tm):
    b, s, d = x.shape
    has_prev = moe is not None
    tok = lambda bi, j: (bi, j, 0)
    tok_t = lambda bi, j: (bi, 0, j)
    in_specs = [pl.BlockSpec((1, tm, d), tok)]
    args = [x]
    if has_prev:
        in_specs += [pl.BlockSpec((1, tm, d), tok), _mod_spec(mod, l - 1, 5)]
        args += [moe, mod]
    stacked = [wp["w_cat"], wp["w_t"], wp["b_f"], wp["gvec"], wp["gain_t"]]
    shared = [wp["gsum"], wp["tril"], wp["perm"], wp["slope_lanes"], wp["swa_slope_lanes"]]
    in_specs += [_mod_spec(mod, l, 0), _mod_spec(mod, l, 1),
                 pl.BlockSpec((None, None, 1, d), lambda bi, j: (l, 0, 0, 0))]
    in_specs += [_layer_spec(a, l) for a in stacked] + [_const_spec(a) for a in shared]
    args += [mod, mod, norm_gain] + stacked + shared
    nblk = s // MOBA_BLOCK
    out_shape = [
        jax.ShapeDtypeStruct((b, W_FOX, s), BF16), jax.ShapeDtypeStruct((b, s, W_FOX_AUG), BF16),
        jax.ShapeDtypeStruct((b, W_FOX, s), BF16),
        jax.ShapeDtypeStruct((b, W_SWA, s), BF16), jax.ShapeDtypeStruct((b, s, 2 * LANES), BF16),
        jax.ShapeDtypeStruct((b, W_SWA_KV, s), BF16),
        jax.ShapeDtypeStruct((b, W_MOBA, s), F32), jax.ShapeDtypeStruct((b, s, W_MOBA_AUG), BF16),
        jax.ShapeDtypeStruct((b, W_MOBA, s), BF16),
        jax.ShapeDtypeStruct((b, nblk, 1, W_MOBA_AUG), F32),
        jax.ShapeDtypeStruct((b, s, 3 * d), BF16),
    ]
    out_specs = [
        pl.BlockSpec((1, W_FOX, tm), tok_t), pl.BlockSpec((1, tm, W_FOX_AUG), tok),
        pl.BlockSpec((1, W_FOX, tm), tok_t),
        pl.BlockSpec((1, W_SWA, tm), tok_t), pl.BlockSpec((1, tm, 2 * LANES), tok),
        pl.BlockSpec((1, W_SWA_KV, tm), tok_t),
        pl.BlockSpec((1, W_MOBA, tm), tok_t), pl.BlockSpec((1, tm, W_MOBA_AUG), tok),
        pl.BlockSpec((1, W_MOBA, tm), tok_t),
        pl.BlockSpec((1, tm // MOBA_BLOCK, 1, W_MOBA_AUG), lambda bi, j: (bi, j, 0, 0)),
        pl.BlockSpec((1, tm, 3 * d), tok),
    ]
    if has_prev:
        out_shape.append(jax.ShapeDtypeStruct((b, s, d), F32))
        out_specs.append(pl.BlockSpec((1, tm, d), tok))
    return pl.pallas_call(
        functools.partial(_in_kernel, has_prev=has_prev, tm=tm, d=d),
        grid=(b, s // tm),
        in_specs=in_specs,
        out_specs=out_specs,
        out_shape=out_shape,
        scratch_shapes=[pltpu.VMEM((8, LANES), F32)],
        compiler_params=_params(("parallel", "arbitrary")),
    )(*args)


def _flash_t(i, tq, n_heads, scores, values, o_ref, m_ref, acc_ref):
    m_ref[...] = jnp.full_like(m_ref, NEG)
    acc_ref[...] = jnp.zeros_like(acc_ref)
    causal = (_iota((tq, tq), 0) <= _iota((tq, tq), 1))[None]
    one_row = jnp.where(_iota((SUM_ROWS, tq), 0) == 0, 1.0, 0.0).astype(BF16)

    def block(j, diagonal):
        s = jnp.stack([scores(j, h) for h in range(n_heads)])
        if diagonal:
            s = jnp.where(causal, s, NEG)
        m_old = m_ref[...]
        m_new = jnp.maximum(m_old, jnp.max(s, axis=1, keepdims=True))
        alpha = jnp.exp2(m_old - m_new)
        pb = jnp.exp2(s - m_new).astype(BF16)
        m_ref[...] = m_new
        pv = jnp.stack([jnp.dot(jnp.concatenate([values(j, h), one_row], axis=0), pb[h],
                                preferred_element_type=F32) for h in range(n_heads)])
        acc_ref[...] = acc_ref[...] * alpha + pv

    def body(j, carry):
        block(j, False)
        return carry

    lax.fori_loop(0, i, body, 0)
    block(i, True)
    acc = acc_ref[...]
    out = acc[:, :HEAD_DIM] * (1.0 / acc[:, HEAD_DIM:HEAD_DIM + 1])
    o_ref[0] = jnp.transpose(out.reshape(n_heads * HEAD_DIM, tq)).astype(BF16)


def _fox_kernel(qt_ref, k_ref, vt_ref, o_ref, m_ref, acc_ref, *, tq):
    i = pl.program_id(1)
    tail = jnp.where(_iota((LANES - HEAD_DIM, tq), 0) < N_PIECES, 1.0, 0.0).astype(BF16)
    qa = [jnp.concatenate([qt_ref[0, h * HEAD_DIM:(h + 1) * HEAD_DIM, :], tail], axis=0)
          for h in range(FOX_HEADS)]

    def scores(j, h):
        start = pl.multiple_of(j * tq, tq)
        return jnp.dot(k_ref[0, pl.ds(start, tq), h * LANES:(h + 1) * LANES], qa[h],
                       preferred_element_type=F32)

    def values(j, h):
        start = pl.multiple_of(j * tq, tq)
        return vt_ref[0, h * HEAD_DIM:(h + 1) * HEAD_DIM, pl.ds(start, tq)]

    _flash_t(i, tq, FOX_HEADS, scores, values, o_ref, m_ref, acc_ref)


def _fox_attention(qt, k_aug, vt, tq):
    b, w, s = qt.shape
    return pl.pallas_call(
        functools.partial(_fox_kernel, tq=tq),
        grid=(b, s // tq),
        in_specs=[
            pl.BlockSpec((1, w, tq), lambda bi, i: (bi, 0, i)),
            pl.BlockSpec((1, s, k_aug.shape[2]), lambda bi, i: (bi, 0, 0)),
            pl.BlockSpec((1, w, s), lambda bi, i: (bi, 0, 0)),
        ],
        out_specs=pl.BlockSpec((1, tq, w), lambda bi, i: (bi, i, 0)),
        out_shape=jax.ShapeDtypeStruct((b, s, w), BF16),
        scratch_shapes=[pltpu.VMEM((FOX_HEADS, 1, tq), F32),
                        pltpu.VMEM((FOX_HEADS, HEAD_DIM + SUM_ROWS, tq), F32)],
        compiler_params=_params(("parallel", "parallel")),
    )(qt, k_aug, vt)


def _moba_kernel(qt_ref, k_ref, vt_ref, kmean_ref, o_ref, m_ref, acc_ref, *, tq):
    i = pl.program_id(1)
    row = _iota((LANES, tq), 0)
    rowf = row.astype(F32)
    own = (i * tq + _iota((LANES, tq), 1)) // MOBA_BLOCK
    past = row < own
    tail = jnp.where(_iota((LANES - MOBA_ALIBI_LANE, tq), 0) < N_PIECES, 1.0, 0.0)
    qa = []
    for h in range(MOBA_HEADS):
        qh = qt_ref[0, h * HEAD_DIM:(h + 1) * HEAD_DIM, :]
        gate = jnp.dot(kmean_ref[0, h], qh, precision=HIGHEST, preferred_element_type=F32)
        g = jnp.where(past, gate, NEG)
        chosen = jnp.zeros((LANES, tq), jnp.bool_)
        for _ in range(MOBA_TOPK):
            mx = jnp.max(g, axis=0, keepdims=True)
            first = jnp.min(jnp.where(g == mx, rowf, float(LANES)), axis=0, keepdims=True)
            pick = (rowf == first) & (mx > 0.5 * NEG)
            chosen = chosen | pick
            g = jnp.where(pick, NEG, g)
        bias = jnp.where(past & jnp.logical_not(chosen), MOBA_MASK_BIAS, 0.0)
        qa.append(jnp.concatenate([qh, bias[:MOBA_MAX_BLOCKS], tail], axis=0).astype(BF16))

    def scores(j, h):
        start = pl.multiple_of(j * tq, tq)
        return jnp.dot(k_ref[0, pl.ds(start, tq), h * LANES:(h + 1) * LANES], qa[h],
                       preferred_element_type=F32)

    def values(j, h):
        start = pl.multiple_of(j * tq, tq)
        return vt_ref[0, h * HEAD_DIM:(h + 1) * HEAD_DIM, pl.ds(start, tq)]

    _flash_t(i, tq, MOBA_HEADS, scores, values, o_ref, m_ref, acc_ref)


def _moba_attention(qt, k_aug, vt, kmean_mat, tq):
    b, w, s = qt.shape
    assert s // MOBA_BLOCK <= MOBA_MAX_BLOCKS and tq % MOBA_BLOCK == 0
    return pl.pallas_call(
        functools.partial(_moba_kernel, tq=tq),
        grid=(b, s // tq),
        in_specs=[
            pl.BlockSpec((1, w, tq), lambda bi, i: (bi, 0, i)),
            pl.BlockSpec((1, s, k_aug.shape[2]), lambda bi, i: (bi, 0, 0)),
            pl.BlockSpec((1, w, s), lambda bi, i: (bi, 0, 0)),
            pl.BlockSpec((1, MOBA_HEADS, LANES, HEAD_DIM), lambda bi, i: (bi, 0, 0, 0)),
        ],
        out_specs=pl.BlockSpec((1, tq, w), lambda bi, i: (bi, i, 0)),
        out_shape=jax.ShapeDtypeStruct((b, s, w), BF16),
        scratch_shapes=[pltpu.VMEM((MOBA_HEADS, 1, tq), F32),
                        pltpu.VMEM((MOBA_HEADS, HEAD_DIM + SUM_ROWS, tq), F32)],
        compiler_params=_params(("parallel", "parallel")),
    )(qt, k_aug, vt, kmean_mat)


def _swa_kernel(sinks_ref, slopes_ref, qt_ref, k_ref, vt_ref, o_ref, *, tq):
    i = pl.program_id(1)
    tk = tq + SWA_WINDOW
    start = pl.multiple_of(jnp.maximum(i * tq - SWA_WINDOW, 0), SWA_WINDOW)
    kw = k_ref[0, pl.ds(start, tk), :]
    dist = (i * tq + _iota((tk, tq), 1)) - (start + _iota((tk, tq), 0))
    valid = (dist >= 0) & (dist < SWA_WINDOW)
    qpos = (i * tq + _iota((1, tq), 1)).astype(F32)
    group = SWA_HEADS // SWA_KV_HEADS
    aug_row = _iota((LANES, tq), 0)
    blank = jnp.zeros((HEAD_DIM, tq), BF16)
    scores, sinks = [], []
    for h in range(SWA_HEADS):
        qh = qt_ref[0, h * HEAD_DIM:(h + 1) * HEAD_DIM, :]
        ones = jnp.where((aug_row >= N_PIECES * h) & (aug_row < N_PIECES * (h + 1)), 1.0, 0.0).astype(BF16)
        qa = jnp.concatenate([qh, blank, ones] if h < group else [blank, qh, ones], axis=0)
        scores.append(jnp.dot(kw, qa, preferred_element_type=F32))
        sinks.append((sinks_ref[h] + slopes_ref[h] * qpos) * LOG2E)
    s = jnp.where(valid[None], jnp.stack(scores), NEG)
    sink = jnp.stack(sinks)
    m = jnp.maximum(jnp.max(s, axis=1, keepdims=True), sink)
    pb = jnp.exp2(s - m).astype(BF16)
    one_row = jnp.where(_iota((SUM_ROWS, tk), 0) == 0, 1.0, 0.0).astype(BF16)
    outs = [jnp.dot(jnp.concatenate([vt_ref[0, (h // group) * HEAD_DIM:(h // group + 1) * HEAD_DIM,
                                            pl.ds(start, tk)], one_row], axis=0), pb[h],
                    preferred_element_type=F32) for h in range(SWA_HEADS)]
    res = jnp.stack(outs)
    out = res[:, :HEAD_DIM] / (res[:, HEAD_DIM:HEAD_DIM + 1] + jnp.exp2(sink - m))
    o_ref[0] = jnp.transpose(out.reshape(SWA_HEADS * HEAD_DIM, tq)).astype(BF16)


def _swa_attention(qt, k_aug, vt, sinks, slopes, tq):
    b, w, s = qt.shape
    smem = pl.BlockSpec(memory_space=pltpu.SMEM)
    return pl.pallas_call(
        functools.partial(_swa_kernel, tq=tq),
        grid=(b, s // tq),
        in_specs=[
            smem, smem,
            pl.BlockSpec((1, w, tq), lambda bi, i: (bi, 0, i)),
            pl.BlockSpec((1, s, k_aug.shape[2]), lambda bi, i: (bi, 0, 0)),
            pl.BlockSpec((1, vt.shape[1], s), lambda bi, i: (bi, 0, 0)),
        ],
        out_specs=pl.BlockSpec((1, tq, w), lambda bi, i: (bi, i, 0)),
        out_shape=jax.ShapeDtypeStruct((b, s, w), BF16),
        compiler_params=_params(("parallel", "parallel")),
    )(sinks, slopes, qt, k_aug, vt)


def _out_kernel(oa_ref, ob_ref, oc_ref, gates_ref, x_ref, g1_ref, sh_ref, sc_ref, gain_ref,
                wa_ref, wb_ref, wc_ref, wo_ref, wr_ref, br_ref,
                x1_ref, hx_ref, route_ref, *, d, parts):
    tm = x_ref.shape[1] // parts
    for part in range(parts):
        _out_rows(slice(part * tm, (part + 1) * tm), tm, d, oa_ref, ob_ref, oc_ref, gates_ref, x_ref, g1_ref,
                  sh_ref, sc_ref, gain_ref, wa_ref, wb_ref, wc_ref, wo_ref, wr_ref, br_ref,
                  x1_ref, hx_ref, route_ref)


def _out_rows(rows, tm, d, oa_ref, ob_ref, oc_ref, gates_ref, x_ref, g1_ref, sh_ref, sc_ref, gain_ref,
              wa_ref, wb_ref, wc_ref, wo_ref, wr_ref, br_ref, x1_ref, hx_ref, route_ref):
    ya = jnp.dot(oa_ref[0, rows, :], wa_ref[...], preferred_element_type=F32)
    yb = jnp.dot(ob_ref[0, rows, :], wb_ref[...], preferred_element_type=F32)
    yc = jnp.dot(oc_ref[0, rows, :], wc_ref[...], preferred_element_type=F32)
    g = gates_ref[0, rows, :]
    mix = (g[:, :d].astype(F32) * ya + g[:, d:2 * d].astype(F32) * yb + g[:, 2 * d:].astype(F32) * yc)
    y = jnp.dot(mix.astype(BF16), wo_ref[...], preferred_element_type=F32)
    x1 = x_ref[0, rows, :] + g1_ref[0] * y
    x1_ref[0, rows, :] = x1
    ms = jnp.mean(x1 * x1, axis=-1, keepdims=True)
    h = x1 * lax.rsqrt(ms + EPS) * gain_ref[...]
    h = h * (1.0 + sc_ref[0]) + sh_ref[0]
    hx_ref[0, rows, :d] = h
    h_hi = h.astype(BF16)
    h_lo = (h - h_hi.astype(F32)).astype(BF16)
    logits = (jnp.dot(h_hi, wr_ref[0], preferred_element_type=F32)
              + jnp.dot(h_lo, wr_ref[0], preferred_element_type=F32)
              + jnp.dot(h_hi, wr_ref[1], preferred_element_type=F32)) + br_ref[...]
    lanef = _iota((tm, LANES), 1).astype(F32)
    vals, idxs = [], []
    for _ in range(TOP_K):
        mx = jnp.max(logits, axis=-1, keepdims=True)
        first = jnp.min(jnp.where(logits == mx, lanef, float(LANES)), axis=-1, keepdims=True)
        vals.append(mx)
        idxs.append(first)
        logits = jnp.where(lanef == first, -jnp.inf, logits)
    exps = [jnp.exp(v - vals[0]) for v in vals]
    den = exps[0]
    for e in exps[1:]:
        den = den + e
    gates = jnp.zeros((tm, LANES), F32)
    info = jnp.zeros((tm, LANES), F32)
    for k in range(TOP_K):
        wk = exps[k] / den
        gates = jnp.where(lanef == idxs[k], wk, gates)
        info = jnp.where(lanef == float(k), idxs[k], info)
    hx_ref[0, rows, d:] = gates
    route_ref[0, rows, :] = info


def _out_proj(oa, ob, oc, gates, x, mod, norm_gain, l, wp, tm):
    b, s, d = x.shape
    tok = lambda bi, j: (bi, j, 0)
    stacked = [wp["w_a"], wp["w_b"], wp["w_c"], wp["w_o"], wp["w_r"], wp["b_r"]]
    return pl.pallas_call(
        functools.partial(_out_kernel, d=d, parts=4),
        grid=(b, s // tm),
        in_specs=[
            pl.BlockSpec((1, tm, W_FOX), tok), pl.BlockSpec((1, tm, W_SWA), tok),
            pl.BlockSpec((1, tm, W_MOBA), tok), pl.BlockSpec((1, tm, 3 * d), tok),
            pl.BlockSpec((1, tm, d), tok),
            _mod_spec(mod, l, 2), _mod_spec(mod, l, 3), _mod_spec(mod, l, 4),
            pl.BlockSpec((None, None, 1, d), lambda bi, j: (l, 1, 0, 0)),
        ] + [_layer_spec(a, l) for a in stacked],
        out_specs=[pl.BlockSpec((1, tm, d), tok), pl.BlockSpec((1, tm, d + LANES), tok),
                   pl.BlockSpec((1, tm, LANES), tok)],
        out_shape=[jax.ShapeDtypeStruct((b, s, d), F32), jax.ShapeDtypeStruct((b, s, d + LANES), F32),
                   jax.ShapeDtypeStruct((b, s, LANES), F32)],
        compiler_params=_params(("parallel", "parallel")),
    )(oa, ob, oc, gates, x, mod, mod, mod, norm_gain, *stacked)


MOE_PAD_ROWS = 8


def _moe_kernel(tile_e_ref, tile_p0_ref, tile_off_ref, tile_n_ref, tok_ref,
                hx_ref, w1g_ref, w1l_ref, b1g_ref, b1l_ref, w2_ref, b2_ref,
                out_ref, xa_ref, xb_ref, ya_ref, yb_ref, *, slots, tm, tc, d):
    c = pl.program_id(0)
    i = pl.program_id(1)
    k = c * slots + i + 1

    def gather(kk, x_ref):
        p0 = tile_p0_ref[kk]
        for r in range(tm):
            tk = tok_ref[p0 + r]
            x_ref[r // 8, r % 8:r % 8 + 1, :] = hx_ref[0, pl.ds(tk, 1), :]

    def expert(x_ref, y_ref):
        xg = x_ref[...].reshape(tm, hx_ref.shape[2])
        xb = xg[:, :d].astype(BF16)
        lane = _iota((tm, LANES), 1)
        wcol = jnp.sum(jnp.where(lane == tile_e_ref[k], xg[:, d:], 0.0), axis=-1, keepdims=True)
        ug = jnp.dot(xb, w1g_ref[0], preferred_element_type=F32) + b1g_ref[0]
        ul = jnp.dot(xb, w1l_ref[0], preferred_element_type=F32) + b1l_ref[0]
        ug = jnp.minimum(ug, SWIGLU_LIMIT)
        ul = jnp.clip(ul, -SWIGLU_LIMIT, SWIGLU_LIMIT)
        act = ug * jax.nn.sigmoid(SWIGLU_ALPHA * ug) * (ul + 1.0)
        y = jnp.dot(act.astype(BF16), w2_ref[0], preferred_element_type=F32) + b2_ref[0]
        y_ref[...] = (y * wcol).reshape(y_ref.shape)

    def scatter(kk, y_ref):
        p0 = tile_p0_ref[kk]
        off = tile_off_ref[kk]
        end = off + tile_n_ref[kk]
        for r0 in range(0, tm, 4):
            toks = [jnp.where((r0 + u >= off) & (r0 + u < end), tok_ref[p0 + r0 + u], tc + u) for u in range(4)]
            rows = [out_ref[0, pl.ds(tk, 1), :] for tk in toks]
            for u in range(4):
                r = r0 + u
                out_ref[0, pl.ds(toks[u], 1), :] = rows[u] + y_ref[r // 8, r % 8:r % 8 + 1, :]

    @pl.when(i == 0)
    def _():
        out_ref[...] = jnp.zeros_like(out_ref)
        ya_ref[...] = jnp.zeros_like(ya_ref)
        yb_ref[...] = jnp.zeros_like(yb_ref)
        gather(k, xa_ref)

    busy = (tile_n_ref[k] > 0) | (tile_n_ref[k - 1] > 0)

    @pl.when(busy & (i % 2 == 0))
    def _():
        gather(k + 1, xb_ref)
        expert(xa_ref, ya_ref)
        scatter(k - 1, yb_ref)

    @pl.when(busy & (i % 2 == 1))
    def _():
        gather(k + 1, xa_ref)
        expert(xb_ref, yb_ref)
        scatter(k - 1, ya_ref)


def _moe(hx, tables, wp, e0, tc, tm):
    b, s, dx = hx.shape
    d = dx - LANES
    nc = (b * s) // tc
    tile_e = tables[0]
    slots = tile_e.shape[0] // nc
    f = wp["w1g"].shape[2]
    hx = hx.reshape(nc, tc, dx)
    chunk = lambda c, i, *prefetch: (c, 0, 0)
    expert = lambda c, i, te, *prefetch: (e0 + te[c * slots + i + 1], 0, 0)
    x_tile = pltpu.VMEM((tm // 8, 8, dx), F32)
    y_tile = pltpu.VMEM((tm // 8, 8, d), F32)
    out = pl.pallas_call(
        functools.partial(_moe_kernel, slots=slots, tm=tm, tc=tc, d=d),
        grid_spec=pltpu.PrefetchScalarGridSpec(
            num_scalar_prefetch=len(tables),
            grid=(nc, slots - 2),
            in_specs=[
                pl.BlockSpec((1, tc, dx), chunk, pipeline_mode=pl.Buffered(1)),
                pl.BlockSpec((1, d, f), expert), pl.BlockSpec((1, d, f), expert),
                pl.BlockSpec((1, 1, f), expert), pl.BlockSpec((1, 1, f), expert),
                pl.BlockSpec((1, f, d), expert), pl.BlockSpec((1, 1, d), expert),
            ],
            out_specs=pl.BlockSpec((1, tc + MOE_PAD_ROWS, d), chunk, pipeline_mode=pl.Buffered(1)),
            scratch_shapes=[x_tile, x_tile, y_tile, y_tile],
        ),
        out_shape=jax.ShapeDtypeStruct((nc, tc + MOE_PAD_ROWS, d), F32),
        compiler_params=_params(("parallel", "arbitrary")),
    )(*tables, hx, wp["w1g"], wp["w1l"], wp["b1g"], wp["b1l"], wp["w2"], wp["b2"])
    if tc == s:
        return out
    return out[:, :tc].reshape(b, s, d)


def _routing_tables(route, n_experts, tc, tm):
    b, s, _ = route.shape
    t = b * s
    nc = t // tc
    idx = route[:, :, :TOP_K].astype(jnp.int32).reshape(nc, tc, TOP_K)
    key = idx * tc + jnp.arange(tc, dtype=jnp.int32)[None, :, None]
    key = jnp.sort(key.reshape(nc, tc * TOP_K), axis=1)
    tok = key % tc
    e_sorted = key // tc
    experts = jnp.arange(n_experts, dtype=jnp.int32)
    cstart = jnp.sum(e_sorted[:, :, None] < experts[None, None, :], axis=1).astype(jnp.int32)
    cend = jnp.concatenate([cstart[:, 1:], jnp.full((nc, 1), tc * TOP_K, jnp.int32)], axis=1)
    count = cend - cstart
    tiles = (count + tm - 1) // tm
    tend = jnp.cumsum(tiles, axis=1)
    tstart = tend - tiles
    nt = -(-(tc * TOP_K) // tm) + n_experts
    slot = jnp.arange(-1, nt + 2, dtype=jnp.int32)
    total = tend[:, -1:]
    live = (slot[None, :] >= 0) & (slot[None, :] < total)
    slot_c = jnp.clip(slot[None, :], 0, total - 1)
    e_of = jnp.sum(slot_c[:, :, None] >= tend[:, None, :], axis=2).astype(jnp.int32)
    pick = lambda tbl: jnp.sum(jnp.where(e_of[:, :, None] == experts[None, None, :], tbl[:, None, :], 0), axis=2)
    jj = slot_c - pick(tstart)
    p0 = pick(cstart) + jj * tm
    n = jnp.where(live, jnp.clip(pick(count) - jj * tm, 0, tm), 0)
    p0 = p0 + (jnp.arange(nc, dtype=jnp.int32) * (tc * TOP_K))[:, None]
    first = jnp.minimum(p0, nc * tc * TOP_K - tm)
    flat = lambda a: a.reshape(-1).astype(jnp.int32)
    return flat(e_of), flat(first), flat(p0 - first), flat(n), flat(tok)


def _residual_kernel(x_ref, moe_ref, g_ref, o_ref):
    o_ref[0] = x_ref[0] + g_ref[0] * moe_ref[0]


def _residual(x, moe, mod, l, tm):
    b, s, d = x.shape
    tok = lambda bi, j: (bi, j, 0)
    return pl.pallas_call(
        _residual_kernel,
        grid=(b, s // tm),
        in_specs=[pl.BlockSpec((1, tm, d), tok), pl.BlockSpec((1, tm, d), tok), _mod_spec(mod, l, 5)],
        out_specs=pl.BlockSpec((1, tm, d), tok),
        out_shape=jax.ShapeDtypeStruct((b, s, d), F32),
        compiler_params=_params(("parallel", "parallel")),
    )(x, moe, mod)


def _split_kernel(w_ref, p_ref, g_ref, l_ref):
    half = MXU_DIM // 2
    for c in range(w_ref.shape[2] // MXU_DIM):
        w = w_ref[0, :, c * MXU_DIM:(c + 1) * MXU_DIM].astype(BF16)
        z = jnp.dot(w, p_ref[...], preferred_element_type=F32)
        g_ref[0, :, c * half:(c + 1) * half] = z[:, :half].astype(BF16)
        l_ref[0, :, c * half:(c + 1) * half] = z[:, half:].astype(BF16)


def _split_glu_weights(w):
    n, d, f2 = w.shape
    cw = min(f2, 4 * MXU_DIM)
    half = MXU_DIM // 2
    k = np.arange(half)
    perm = np.zeros((MXU_DIM, MXU_DIM), np.float32)
    perm[2 * k, k] = 1.0
    perm[2 * k + 1, half + k] = 1.0
    out = jax.ShapeDtypeStruct((n, d, f2 // 2), BF16)
    return pl.pallas_call(
        _split_kernel,
        grid=(n, f2 // cw),
        in_specs=[pl.BlockSpec((1, d, cw), lambda i, j: (i, 0, j)),
                  pl.BlockSpec((MXU_DIM, MXU_DIM), lambda i, j: (0, 0))],
        out_specs=[pl.BlockSpec((1, d, cw // 2), lambda i, j: (i, 0, j))] * 2,
        out_shape=[out, out],
        compiler_params=_params(("parallel", "parallel")),
    )(w, jnp.asarray(perm, BF16))


def _pad_heads(w, n_heads):
    lead = w.shape[:-1]
    w = w.reshape(lead + (n_heads, HEAD_DIM))
    w = jnp.concatenate([w, jnp.zeros_like(w)], axis=-1)
    return w.reshape(lead + (n_heads * LANES,))


def _prepare_weights(d, tm, slopes, w_in, b_fgate, qk_gain, w_br_fox, w_br_swa, w_br_moba, w_out,
                     w_router, b_router, w_exp1, b_exp1, w_exp2, b_exp2):
    depth = w_in.shape[0]
    n_exp = w_router.shape[2]
    splits = (W_FOX, W_FOX, W_FOX, FOX_HEADS, W_SWA, W_SWA_KV, W_SWA_KV, W_MOBA, W_MOBA, W_MOBA, 3 * d)
    pts = np.cumsum(splits)[:-1].tolist()
    qa, ka, va, fa, qb, kb, vb, qc, kc, vc, gt = jnp.split(w_in, pts, axis=2)
    fa = jnp.pad(fa, ((0, 0), (0, 0), (0, LANES - FOX_HEADS)))
    w_cat = jnp.concatenate([_pad_heads(ka, FOX_HEADS), fa, kb, _pad_heads(kc, MOBA_HEADS), gt],
                            axis=2).astype(BF16)
    w_t = jnp.swapaxes(jnp.concatenate([qa, va, qc, vc, qb, vb], axis=2), 1, 2).astype(BF16)
    b_f = jnp.pad(b_fgate.astype(F32), ((0, 0), (0, LANES - FOX_HEADS)))[:, None, :]
    g = qk_gain.astype(F32)
    tile = lambda v, n: jnp.tile(v, (1, n))
    ones = lambda n: jnp.ones((depth, n), F32)
    qscale = HEAD_DIM ** -0.5
    gvec = jnp.concatenate([
        _pad_heads(tile(g[:, 1], FOX_HEADS), FOX_HEADS), ones(LANES),
        tile(g[:, 3], SWA_KV_HEADS),
        _pad_heads(tile(g[:, 5], MOBA_HEADS), MOBA_HEADS),
    ], axis=1)[:, None, :]
    gain_t = jnp.concatenate([tile(g[:, 0], FOX_HEADS), tile(g[:, 4], MOBA_HEADS), tile(g[:, 2], SWA_HEADS)],
                             axis=1) * (qscale * LOG2E)
    gain_t = jnp.broadcast_to(gain_t[:, :, None], (depth, W_FOX + W_MOBA + W_SWA, tm))
    heads = MXU_DIM // HEAD_DIM
    gsum = jnp.asarray(np.kron(np.eye(heads), np.ones((HEAD_DIM, HEAD_DIM))), BF16)
    tril = jnp.asarray(np.tril(np.ones((tm, tm))), F32)
    perm = np.zeros((N_PIECES * LANES, W_FOX_AUG), np.float32)
    for h in range(FOX_HEADS):
        for p in range(N_PIECES):
            perm[p * LANES + h, h * LANES + FOX_CUM_LANE + p] = 1.0
    slope_lanes = jnp.repeat(slopes[SWA_HEADS:] * LOG2E, LANES)[None, :]
    swa_slope_lanes = jnp.pad(jnp.repeat(slopes[:SWA_HEADS] * LOG2E, N_PIECES),
                              (0, LANES - N_PIECES * SWA_HEADS))[None, :]
    w_r = jnp.pad(w_router.astype(F32), ((0, 0), (0, 0), (0, LANES - n_exp)))
    w_r_hi = w_r.astype(BF16)
    w_r = jnp.stack([w_r_hi, (w_r - w_r_hi.astype(F32)).astype(BF16)], axis=1)
    b_r = jnp.pad(b_router.astype(F32), ((0, 0), (0, LANES - n_exp)), constant_values=NEG)[:, None, :]
    w1g, w1l = _split_glu_weights(w_exp1.reshape((depth * n_exp,) + w_exp1.shape[2:]))
    f = w_exp2.shape[2]
    b1 = b_exp1.reshape(depth * n_exp, 1, f, 2)
    return dict(
        w_cat=w_cat, w_t=w_t, b_f=b_f, gvec=gvec, gain_t=gain_t, gsum=gsum, tril=tril,
        perm=jnp.asarray(perm, BF16), slope_lanes=slope_lanes, swa_slope_lanes=swa_slope_lanes,
        w_a=w_br_fox.astype(BF16), w_b=w_br_swa.astype(BF16), w_c=w_br_moba.astype(BF16),
        w_o=w_out.astype(BF16), w_r=w_r, b_r=b_r,
        w1g=w1g, w1l=w1l, b1g=b1[..., 0], b1l=b1[..., 1],
        w2=w_exp2.reshape(depth * n_exp, f, d).astype(BF16), b2=b_exp2.reshape(depth * n_exp, 1, d),
    )


def kernel(x, c, w_ada, b_ada, norm_gain, w_in, b_fgate, qk_gain, attn_sinks, w_br_fox, w_br_swa, w_br_moba, w_out, w_router, b_router, w_exp1, b_exp1, w_exp2, b_exp2):
    b, s, d = x.shape
    depth = w_ada.shape[0]
    n_experts = w_router.shape[2]
    tm = min(s, 512)
    tq_fox = min(s, 512)
    tq_swa = min(s, 256)
    tc = min(s, 4096)
    per_expert = tc * TOP_K // n_experts
    tm_moe = max(-(-(per_expert * 9 // 16) // 32) * 32, 64)
    n_alibi = SWA_HEADS + MOBA_HEADS
    slopes = jnp.exp2(-8.0 * jnp.arange(1, n_alibi + 1, dtype=F32) / n_alibi)

    mod = _modulation(c, w_ada, b_ada).reshape(depth, b, 1, 6 * d)
    ng = norm_gain.astype(F32).reshape(depth, 2, 1, d)
    sinks = attn_sinks.astype(F32)
    wp = _prepare_weights(d, tm, slopes, w_in, b_fgate, qk_gain, w_br_fox, w_br_swa, w_br_moba, w_out,
                          w_router, b_router, w_exp1, b_exp1, w_exp2, b_exp2)
    moe = None
    for l in range(depth):
        outs = _in_proj(x, moe, mod, ng, l, wp, tm)
        qft, kfa, vft, qst, ksa, vst, qmt, kma, vmt, kmean, gates = outs[:11]
        if moe is not None:
            x = outs[11]
        o_a = _fox_attention(qft, kfa, vft, tq_fox)
        o_b = _swa_attention(qst, ksa, vst, sinks[l], slopes, tq_swa)
        nblk = s // MOBA_BLOCK
        km4 = kmean.reshape(b, nblk, MOBA_HEADS, LANES)[..., :HEAD_DIM]
        kmat = jnp.zeros((b, MOBA_HEADS, LANES, HEAD_DIM), F32)
        kmat = kmat.at[:, :, :nblk].set(jnp.transpose(km4, (0, 2, 1, 3)))
        o_c = _moba_attention(qmt, kma, vmt, kmat, tq_fox)
        x, hx, route = _out_proj(o_a, o_b, o_c, gates, x, mod, ng, l, wp, tm)
        tables = _routing_tables(route, n_experts, tc, tm_moe)
        moe = _moe(hx, tables, wp, l * n_experts, tc, tm_moe)
    return _residual(x, moe, mod, depth - 1, tm)
```

```python
import functools

import numpy as np
import jax
import jax.numpy as jnp
from jax import lax
from jax.experimental import pallas as pl
from jax.experimental.pallas import tpu as pltpu

HEAD_DIM = 64
FOX_HEADS = 4
SWA_HEADS = 8
SWA_KV_HEADS = 2
SWA_WINDOW = 128
MOBA_HEADS = 4
MOBA_BLOCK = 256
MOBA_TOPK = 3
TOP_K = 4
SWIGLU_ALPHA = 1.702
SWIGLU_LIMIT = 7.0
EPS = 1e-6
NEG = -1e30
LOG2E = 1.4426950408889634
MOBA_MASK_BIAS = -float(2 ** 30)

LANES = 128
MXU_DIM = 256
VMEM_LIMIT = 60 * 1024 * 1024

W_FOX = FOX_HEADS * HEAD_DIM
W_SWA = SWA_HEADS * HEAD_DIM
W_SWA_KV = SWA_KV_HEADS * HEAD_DIM
W_MOBA = MOBA_HEADS * HEAD_DIM
W_FOX_AUG = FOX_HEADS * LANES
FOX_CUM_LANE = HEAD_DIM
W_MOBA_AUG = MOBA_HEADS * LANES
MOBA_BLOCK_LANE = HEAD_DIM
MOBA_MAX_BLOCKS = 16
MOBA_ALIBI_LANE = MOBA_BLOCK_LANE + MOBA_MAX_BLOCKS
N_PIECES = 3
SUM_ROWS = 16

F32 = jnp.float32
BF16 = jnp.bfloat16
HIGHEST = lax.Precision.HIGHEST
NT_DIMS = (((1,), (1,)), ((), ()))


def _params(semantics):
    return pltpu.CompilerParams(dimension_semantics=semantics, vmem_limit_bytes=VMEM_LIMIT)


def _iota(shape, dim, dtype=jnp.int32):
    return lax.broadcasted_iota(dtype, shape, dim)


def _pieces(x):
    hi = x.astype(BF16).astype(F32)
    r = x - hi
    mid = r.astype(BF16).astype(F32)
    return hi, mid, r - mid


def _mod_kernel(c_ref, w_ref, b_ref, o_ref):
    c = c_ref[...]
    s = c * jax.nn.sigmoid(c)
    o_ref[0] = jnp.dot(s, w_ref[0], precision=HIGHEST, preferred_element_type=F32) + b_ref[0]


def _modulation(c, w_ada, b_ada):
    depth, d, n = w_ada.shape
    b = c.shape[0]
    tn = d
    return pl.pallas_call(
        _mod_kernel,
        grid=(depth, n // tn),
        in_specs=[
            pl.BlockSpec((b, d), lambda l, j: (0, 0)),
            pl.BlockSpec((1, d, tn), lambda l, j: (l, 0, j)),
            pl.BlockSpec((1, 1, tn), lambda l, j: (l, 0, j)),
        ],
        out_specs=pl.BlockSpec((1, b, tn), lambda l, j: (l, 0, j)),
        out_shape=jax.ShapeDtypeStruct((depth, b, n), F32),
        compiler_params=_params(("parallel", "parallel")),
    )(c, w_ada, b_ada.reshape(depth, 1, n))


def _head_norm(z, g_ref, gain):
    w = z.shape[1]
    cw = min(w, MXU_DIM)
    outs = []
    for c0 in range(0, w, cw):
        zz = z[:, c0:c0 + cw]
        ss = jnp.dot((zz * zz).astype(BF16), g_ref[:cw, :cw], preferred_element_type=F32)
        outs.append(zz * lax.rsqrt(ss * (1.0 / HEAD_DIM) + EPS))
    zn = outs[0] if len(outs) == 1 else jnp.concatenate(outs, axis=1)
    return zn * gain


def _head_norm_t(zt, gain_t):
    rows, cols = zt.shape
    z3 = zt.reshape(rows // HEAD_DIM, HEAD_DIM, cols)
    ms = jnp.mean(z3 * z3, axis=1, keepdims=True)
    return (z3 * lax.rsqrt(ms + EPS)).reshape(rows, cols) * gain_t


def _in_kernel(*refs, has_prev, tm, d):
    if has_prev:
        x_ref, moe_ref, g2_ref = refs[:3]
        refs = refs[3:]
    else:
        x_ref = refs[0]
        refs = refs[1:]
    (sh_ref, sc_ref, gain_ref, w_ref, wt_ref, bf_ref, gvec_ref, gt_ref, g_ref, tril_ref, perm_ref, slope_ref,
     sslope_ref,
     qft_ref, kfa_ref, vft_ref, qst_ref, ksa_ref, vst_ref, qmt_ref, kma_ref, vmt_ref,
     kmean_ref, gates_ref) = refs[:24]
    refs = refs[24:]
    if has_prev:
        xn_ref, carry_ref = refs
    else:
        (carry_ref,) = refs
    j = pl.program_id(1)

    x = x_ref[0]
    if has_prev:
        x = x + g2_ref[0] * moe_ref[0]
        xn_ref[0] = x
    ms = jnp.mean(x * x, axis=-1, keepdims=True)
    h = x * lax.rsqrt(ms + EPS) * gain_ref[...]
    h = h * (1.0 + sc_ref[0]) + sh_ref[0]
    hb = h.astype(BF16)

    def proj(c0, c1):
        return jnp.dot(hb, w_ref[:, c0:c1], preferred_element_type=F32)

    def gain(c0, c1):
        return gvec_ref[:, c0:c1]

    zt = lax.dot_general(wt_ref[...], hb, NT_DIMS, preferred_element_type=F32)
    qft_ref[0] = _head_norm_t(zt[:W_FOX], gt_ref[:W_FOX]).astype(BF16)
    vft_ref[0] = zt[W_FOX:2 * W_FOX].astype(BF16)
    o = 2 * W_FOX
    qmt_ref[0] = _head_norm_t(zt[o:o + W_MOBA], gt_ref[W_FOX:W_FOX + W_MOBA])
    vmt_ref[0] = zt[o + W_MOBA:o + 2 * W_MOBA].astype(BF16)
    o += 2 * W_MOBA
    qst_ref[0] = _head_norm_t(zt[o:o + W_SWA], gt_ref[W_FOX + W_MOBA:]).astype(BF16)
    vst_ref[0] = zt[o + W_SWA:o + W_SWA + W_SWA_KV].astype(BF16)

    @pl.when(j == 0)
    def _():
        carry_ref[...] = jnp.zeros_like(carry_ref)

    o = 0
    z = proj(o, o + W_FOX_AUG + LANES)
    kf = _head_norm(z[:, :W_FOX_AUG], g_ref, gain(o, o + W_FOX_AUG))
    fa = z[:, W_FOX_AUG:] + bf_ref[...]
    logf = -(jnp.maximum(-fa, 0.0) + jnp.log(1.0 + jnp.exp(-jnp.abs(fa))))
    cum = jnp.dot(tril_ref[...], logf, precision=HIGHEST, preferred_element_type=F32) + carry_ref[0:1, :]
    carry_ref[...] = jnp.broadcast_to(cum[tm - 1:tm, :], carry_ref.shape)
    parts = jnp.concatenate(_pieces(cum * (-LOG2E)), axis=1).astype(BF16)
    kfa_ref[0] = (kf + jnp.dot(parts, perm_ref[...], preferred_element_type=F32)).astype(BF16)
    o += W_FOX_AUG + LANES
    ksw = _head_norm(proj(o, o + W_SWA_KV), g_ref, gain(o, o + W_SWA_KV))
    lane = _iota((tm, LANES), 1)
    pos = (j * tm + _iota((tm, LANES), 0)).astype(F32)
    s_hi, s_mid, s_lo = _pieces(sslope_ref[...] * pos)
    piece = lane % N_PIECES
    alibi = jnp.where(piece == 0, s_hi, jnp.where(piece == 1, s_mid, s_lo))
    ksa_ref[0] = jnp.concatenate([ksw, alibi], axis=1).astype(BF16)
    o += W_SWA_KV
    z = proj(o, o + W_MOBA_AUG)
    kn = _head_norm(z, g_ref, gain(o, o + W_MOBA_AUG))
    nb = tm // MOBA_BLOCK
    kmean_ref[0] = jnp.mean(kn.reshape(nb, MOBA_BLOCK, W_MOBA_AUG), axis=1).reshape(nb, 1, W_MOBA_AUG)
    lane = _iota((tm, W_MOBA_AUG), 1) % LANES
    pos = j * tm + _iota((tm, W_MOBA_AUG), 0)
    a_hi, a_mid, a_lo = _pieces(slope_ref[...] * pos.astype(F32))
    ka = jnp.where(lane == MOBA_BLOCK_LANE + pos // MOBA_BLOCK, 1.0, kn)
    ka = jnp.where(lane == MOBA_ALIBI_LANE, a_hi, ka)
    ka = jnp.where(lane == MOBA_ALIBI_LANE + 1, a_mid, ka)
    ka = jnp.where(lane == MOBA_ALIBI_LANE + 2, a_lo, ka)
    kma_ref[0] = ka.astype(BF16)
    o += W_MOBA_AUG
    for br in range(3):
        zg = proj(o + br * d, o + (br + 1) * d)
        gates_ref[0, :, br * d:(br + 1) * d] = jax.nn.sigmoid(zg).astype(BF16)


def _layer_spec(a, l):
    return pl.BlockSpec((None,) + a.shape[1:], lambda *_: (l,) + (0,) * (a.ndim - 1))


def _const_spec(a):
    return pl.BlockSpec(a.shape, lambda *_: (0,) * a.ndim)


def _mod_spec(mod, l, k):
    d = mod.shape[3] // 6
    return pl.BlockSpec((None, 1, 1, d), lambda bi, j: (l, bi, 0, k))


def _in_proj(x, moe, mod, norm_gain, l, wp, tm):
    b, s, d = x.shape
    has_prev = moe is not None
    tok = lambda bi, j: (bi, j, 0)
    tok_t = lambda bi, j: (bi, 0, j)
    in_specs = [pl.BlockSpec((1, tm, d), tok)]
    args = [x]
    if has_prev:
        in_specs += [pl.BlockSpec((1, tm, d), tok), _mod_spec(mod, l - 1, 5)]
        args += [moe, mod]
    stacked = [wp["w_cat"], wp["w_t"], wp["b_f"], wp["gvec"], wp["gain_t"]]
    shared = [wp["gsum"], wp["tril"], wp["perm"], wp["slope_lanes"], wp["swa_slope_lanes"]]
    in_specs += [_mod_spec(mod, l, 0), _mod_spec(mod, l, 1),
                 pl.BlockSpec((None, None, 1, d), lambda bi, j: (l, 0, 0, 0))]
    in_specs += [_layer_spec(a, l) for a in stacked] + [_const_spec(a) for a in shared]
    args += [mod, mod, norm_gain] + stacked + shared
    nblk = s // MOBA_BLOCK
    out_shape = [
        jax.ShapeDtypeStruct((b, W_FOX, s), BF16), jax.ShapeDtypeStruct((b, s, W_FOX_AUG), BF16),
        jax.ShapeDtypeStruct((b, W_FOX, s), BF16),
        jax.ShapeDtypeStruct((b, W_SWA, s), BF16), jax.ShapeDtypeStruct((b, s, 2 * LANES), BF16),
        jax.ShapeDtypeStruct((b, W_SWA_KV, s), BF16),
        jax.ShapeDtypeStruct((b, W_MOBA, s), F32), jax.ShapeDtypeStruct((b, s, W_MOBA_AUG), BF16),
        jax.ShapeDtypeStruct((b, W_MOBA, s), BF16),
        jax.ShapeDtypeStruct((b, nblk, 1, W_MOBA_AUG), F32),
        jax.ShapeDtypeStruct((b, s, 3 * d), BF16),
    ]
    out_specs = [
        pl.BlockSpec((1, W_FOX, tm), tok_t), pl.BlockSpec((1, tm, W_FOX_AUG), tok),
        pl.BlockSpec((1, W_FOX, tm), tok_t),
        pl.BlockSpec((1, W_SWA, tm), tok_t), pl.BlockSpec((1, tm, 2 * LANES), tok),
        pl.BlockSpec((1, W_SWA_KV, tm), tok_t),
        pl.BlockSpec((1, W_MOBA, tm), tok_t), pl.BlockSpec((1, tm, W_MOBA_AUG), tok),
        pl.BlockSpec((1, W_MOBA, tm), tok_t),
        pl.BlockSpec((1, tm // MOBA_BLOCK, 1, W_MOBA_AUG), lambda bi, j: (bi, j, 0, 0)),
        pl.BlockSpec((1, tm, 3 * d), tok),
    ]
    if has_prev:
        out_shape.append(jax.ShapeDtypeStruct((b, s, d), F32))
        out_specs.append(pl.BlockSpec((1, tm, d), tok))
    return pl.pallas_call(
        functools.partial(_in_kernel, has_prev=has_prev, tm=tm, d=d),
        grid=(b, s // tm),
        in_specs=in_specs,
        out_specs=out_specs,
        out_shape=out_shape,
        scratch_shapes=[pltpu.VMEM((8, LANES), F32)],
        compiler_params=_params(("parallel", "arbitrary")),
    )(*args)


def _flash_t(i, tq, n_heads, scores, values, o_ref, m_ref, acc_ref):
    m_ref[...] = jnp.full_like(m_ref, NEG)
    acc_ref[...] = jnp.zeros_like(acc_ref)
    causal = (_iota((tq, tq), 0) <= _iota((tq, tq), 1))[None]
    one_row = jnp.where(_iota((SUM_ROWS, tq), 0) == 0, 1.0, 0.0).astype(BF16)

    def block(j, diagonal):
        s = jnp.stack([scores(j, h) for h in range(n_heads)])
        if diagonal:
            s = jnp.where(causal, s, NEG)
        m_old = m_ref[...]
        m_new = jnp.maximum(m_old, jnp.max(s, axis=1, keepdims=True))
        alpha = jnp.exp2(m_old - m_new)
        pb = jnp.exp2(s - m_new).astype(BF16)
        m_ref[...] = m_new
        pv = jnp.stack([jnp.dot(jnp.concatenate([values(j, h), one_row], axis=0), pb[h],
                                preferred_element_type=F32) for h in range(n_heads)])
        acc_ref[...] = acc_ref[...] * alpha + pv

    def body(j, carry):
        block(j, False)
        return carry

    lax.fori_loop(0, i, body, 0)
    block(i, True)
    acc = acc_ref[...]
    out = acc[:, :HEAD_DIM] * (1.0 / acc[:, HEAD_DIM:HEAD_DIM + 1])
    o_ref[0] = jnp.transpose(out.reshape(n_heads * HEAD_DIM, tq)).astype(BF16)


def _fox_kernel(qt_ref, k_ref, vt_ref, o_ref, m_ref, acc_ref, *, tq):
    i = pl.program_id(1)
    tail = jnp.where(_iota((LANES - HEAD_DIM, tq), 0) < N_PIECES, 1.0, 0.0).astype(BF16)
    qa = [jnp.concatenate([qt_ref[0, h * HEAD_DIM:(h + 1) * HEAD_DIM, :], tail], axis=0)
          for h in range(FOX_HEADS)]

    def scores(j, h):
        start = pl.multiple_of(j * tq, tq)
        return jnp.dot(k_ref[0, pl.ds(start, tq), h * LANES:(h + 1) * LANES], qa[h],
                       preferred_element_type=F32)

    def values(j, h):
        start = pl.multiple_of(j * tq, tq)
        return vt_ref[0, h * HEAD_DIM:(h + 1) * HEAD_DIM, pl.ds(start, tq)]

    _flash_t(i, tq, FOX_HEADS, scores, values, o_ref, m_ref, acc_ref)


def _fox_attention(qt, k_aug, vt, tq):
    b, w, s = qt.shape
    return pl.pallas_call(
        functools.partial(_fox_kernel, tq=tq),
        grid=(b, s // tq),
        in_specs=[
            pl.BlockSpec((1, w, tq), lambda bi, i: (bi, 0, i)),
            pl.BlockSpec((1, s, k_aug.shape[2]), lambda bi, i: (bi, 0, 0)),
            pl.BlockSpec((1, w, s), lambda bi, i: (bi, 0, 0)),
        ],
        out_specs=pl.BlockSpec((1, tq, w), lambda bi, i: (bi, i, 0)),
        out_shape=jax.ShapeDtypeStruct((b, s, w), BF16),
        scratch_shapes=[pltpu.VMEM((FOX_HEADS, 1, tq), F32),
                        pltpu.VMEM((FOX_HEADS, HEAD_DIM + SUM_ROWS, tq), F32)],
        compiler_params=_params(("parallel", "parallel")),
    )(qt, k_aug, vt)


def _moba_kernel(qt_ref, k_ref, vt_ref, kmean_ref, o_ref, m_ref, acc_ref, *, tq):
    i = pl.program_id(1)
    row = _iota((LANES, tq), 0)
    rowf = row.astype(F32)
    own = (i * tq + _iota((LANES, tq), 1)) // MOBA_BLOCK
    past = row < own
    tail = jnp.where(_iota((LANES - MOBA_ALIBI_LANE, tq), 0) < N_PIECES, 1.0, 0.0)
    qa = []
    for h in range(MOBA_HEADS):
        qh = qt_ref[0, h * HEAD_DIM:(h + 1) * HEAD_DIM, :]
        gate = jnp.dot(kmean_ref[0, h], qh, precision=HIGHEST, preferred_element_type=F32)
        g = jnp.where(past, gate, NEG)
        chosen = jnp.zeros((LANES, tq), jnp.bool_)
        for _ in range(MOBA_TOPK):
            mx = jnp.max(g, axis=0, keepdims=True)
            first = jnp.min(jnp.where(g == mx, rowf, float(LANES)), axis=0, keepdims=True)
            pick = (rowf == first) & (mx > 0.5 * NEG)
            chosen = chosen | pick
            g = jnp.where(pick, NEG, g)
        bias = jnp.where(past & jnp.logical_not(chosen), MOBA_MASK_BIAS, 0.0)
        qa.append(jnp.concatenate([qh, bias[:MOBA_MAX_BLOCKS], tail], axis=0).astype(BF16))

    def scores(j, h):
        start = pl.multiple_of(j * tq, tq)
        return jnp.dot(k_ref[0, pl.ds(start, tq), h * LANES:(h + 1) * LANES], qa[h],
                       preferred_element_type=F32)

    def values(j, h):
        start = pl.multiple_of(j * tq, tq)
        return vt_ref[0, h * HEAD_DIM:(h + 1) * HEAD_DIM, pl.ds(start, tq)]

    _flash_t(i, tq, MOBA_HEADS, scores, values, o_ref, m_ref, acc_ref)


def _moba_attention(qt, k_aug, vt, kmean_mat, tq):
    b, w, s = qt.shape
    assert s // MOBA_BLOCK <= MOBA_MAX_BLOCKS and tq % MOBA_BLOCK == 0
    return pl.pallas_call(
        functools.partial(_moba_kernel, tq=tq),
        grid=(b, s // tq),
        in_specs=[
            pl.BlockSpec((1, w, tq), lambda bi, i: (bi, 0, i)),
            pl.BlockSpec((1, s, k_aug.shape[2]), lambda bi, i: (bi, 0, 0)),
            pl.BlockSpec((1, w, s), lambda bi, i: (bi, 0, 0)),
            pl.BlockSpec((1, MOBA_HEADS, LANES, HEAD_DIM), lambda bi, i: (bi, 0, 0, 0)),
        ],
        out_specs=pl.BlockSpec((1, tq, w), lambda bi, i: (bi, i, 0)),
        out_shape=jax.ShapeDtypeStruct((b, s, w), BF16),
        scratch_shapes=[pltpu.VMEM((MOBA_HEADS, 1, tq), F32),
                        pltpu.VMEM((MOBA_HEADS, HEAD_DIM + SUM_ROWS, tq), F32)],
        compiler_params=_params(("parallel", "parallel")),
    )(qt, k_aug, vt, kmean_mat)


def _swa_kernel(sinks_ref, slopes_ref, qt_ref, k_ref, vt_ref, o_ref, *, tq):
    i = pl.program_id(1)
    tk = tq + SWA_WINDOW
    start = pl.multiple_of(jnp.maximum(i * tq - SWA_WINDOW, 0), SWA_WINDOW)
    kw = k_ref[0, pl.ds(start, tk), :]
    dist = (i * tq + _iota((tk, tq), 1)) - (start + _iota((tk, tq), 0))
    valid = (dist >= 0) & (dist < SWA_WINDOW)
    qpos = (i * tq + _iota((1, tq), 1)).astype(F32)
    group = SWA_HEADS // SWA_KV_HEADS
    aug_row = _iota((LANES, tq), 0)
    blank = jnp.zeros((HEAD_DIM, tq), BF16)
    scores, sinks = [], []
    for h in range(SWA_HEADS):
        qh = qt_ref[0, h * HEAD_DIM:(h + 1) * HEAD_DIM, :]
        ones = jnp.where((aug_row >= N_PIECES * h) & (aug_row < N_PIECES * (h + 1)), 1.0, 0.0).astype(BF16)
        qa = jnp.concatenate([qh, blank, ones] if h < group else [blank, qh, ones], axis=0)
        scores.append(jnp.dot(kw, qa, preferred_element_type=F32))
        sinks.append((sinks_ref[h] + slopes_ref[h] * qpos) * LOG2E)
    s = jnp.where(valid[None], jnp.stack(scores), NEG)
    sink = jnp.stack(sinks)
    m = jnp.maximum(jnp.max(s, axis=1, keepdims=True), sink)
    pb = jnp.exp2(s - m).astype(BF16)
    one_row = jnp.where(_iota((SUM_ROWS, tk), 0) == 0, 1.0, 0.0).astype(BF16)
    outs = [jnp.dot(jnp.concatenate([vt_ref[0, (h // group) * HEAD_DIM:(h // group + 1) * HEAD_DIM,
                                            pl.ds(start, tk)], one_row], axis=0), pb[h],
                    preferred_element_type=F32) for h in range(SWA_HEADS)]
    res = jnp.stack(outs)
    out = res[:, :HEAD_DIM] / (res[:, HEAD_DIM:HEAD_DIM + 1] + jnp.exp2(sink - m))
    o_ref[0] = jnp.transpose(out.reshape(SWA_HEADS * HEAD_DIM, tq)).astype(BF16)


def _swa_attention(qt, k_aug, vt, sinks, slopes, tq):
    b, w, s = qt.shape
    smem = pl.BlockSpec(memory_space=pltpu.SMEM)
    return pl.pallas_call(
        functools.partial(_swa_kernel, tq=tq),
        grid=(b, s // tq),
        in_specs=[
            smem, smem,
            pl.BlockSpec((1, w, tq), lambda bi, i: (bi, 0, i)),
            pl.BlockSpec((1, s, k_aug.shape[2]), lambda bi, i: (bi, 0, 0)),
            pl.BlockSpec((1, vt.shape[1], s), lambda bi, i: (bi, 0, 0)),
        ],
        out_specs=pl.BlockSpec((1, tq, w), lambda bi, i: (bi, i, 0)),
        out_shape=jax.ShapeDtypeStruct((b, s, w), BF16),
        compiler_params=_params(("parallel", "parallel")),
    )(sinks, slopes, qt, k_aug, vt)


def _out_kernel(oa_ref, ob_ref, oc_ref, gates_ref, x_ref, g1_ref, sh_ref, sc_ref, gain_ref,
                wa_ref, wb_ref, wc_ref, wo_ref, wr_ref, br_ref,
                x1_ref, hx_ref, route_ref, *, d, parts):
    tm = x_ref.shape[1] // parts
    for part in range(parts):
        _out_rows(slice(part * tm, (part + 1) * tm), tm, d, oa_ref, ob_ref, oc_ref, gates_ref, x_ref, g1_ref,
                  sh_ref, sc_ref, gain_ref, wa_ref, wb_ref, wc_ref, wo_ref, wr_ref, br_ref,
                  x1_ref, hx_ref, route_ref)


def _out_rows(rows, tm, d, oa_ref, ob_ref, oc_ref, gates_ref, x_ref, g1_ref, sh_ref, sc_ref, gain_ref,
              wa_ref, wb_ref, wc_ref, wo_ref, wr_ref, br_ref, x1_ref, hx_ref, route_ref):
    ya = jnp.dot(oa_ref[0, rows, :], wa_ref[...], preferred_element_type=F32)
    yb = jnp.dot(ob_ref[0, rows, :], wb_ref[...], preferred_element_type=F32)
    yc = jnp.dot(oc_ref[0, rows, :], wc_ref[...], preferred_element_type=F32)
    g = gates_ref[0, rows, :]
    mix = (g[:, :d].astype(F32) * ya + g[:, d:2 * d].astype(F32) * yb + g[:, 2 * d:].astype(F32) * yc)
    y = jnp.dot(mix.astype(BF16), wo_ref[...], preferred_element_type=F32)
    x1 = x_ref[0, rows, :] + g1_ref[0] * y
    x1_ref[0, rows, :] = x1
    ms = jnp.mean(x1 * x1, axis=-1, keepdims=True)
    h = x1 * lax.rsqrt(ms + EPS) * gain_ref[...]
    h = h * (1.0 + sc_ref[0]) + sh_ref[0]
    hx_ref[0, rows, :d] = h
    h_hi = h.astype(BF16)
    h_lo = (h - h_hi.astype(F32)).astype(BF16)
    logits = (jnp.dot(h_hi, wr_ref[0], preferred_element_type=F32)
              + jnp.dot(h_lo, wr_ref[0], preferred_element_type=F32)
              + jnp.dot(h_hi, wr_ref[1], preferred_element_type=F32)) + br_ref[...]
    lanef = _iota((tm, LANES), 1).astype(F32)
    vals, idxs = [], []
    for _ in range(TOP_K):
        mx = jnp.max(logits, axis=-1, keepdims=True)
        first = jnp.min(jnp.where(logits == mx, lanef, float(LANES)), axis=-1, keepdims=True)
        vals.append(mx)
        idxs.append(first)
        logits = jnp.where(lanef == first, -jnp.inf, logits)
    exps = [jnp.exp(v - vals[0]) for v in vals]
    den = exps[0]
    for e in exps[1:]:
        den = den + e
    gates = jnp.zeros((tm, LANES), F32)
    info = jnp.zeros((tm, LANES), F32)
    for k in range(TOP_K):
        wk = exps[k] / den
        gates = jnp.where(lanef == idxs[k], wk, gates)
        info = jnp.where(lanef == float(k), idxs[k], info)
    hx_ref[0, rows, d:] = gates
    route_ref[0, rows, :] = info


def _out_proj(oa, ob, oc, gates, x, mod, norm_gain, l, wp, tm):
    b, s, d = x.shape
    tok = lambda bi, j: (bi, j, 0)
    stacked = [wp["w_a"], wp["w_b"], wp["w_c"], wp["w_o"], wp["w_r"], wp["b_r"]]
    return pl.pallas_call(
        functools.partial(_out_kernel, d=d, parts=4),
        grid=(b, s // tm),
        in_specs=[
            pl.BlockSpec((1, tm, W_FOX), tok), pl.BlockSpec((1, tm, W_SWA), tok),
            pl.BlockSpec((1, tm, W_MOBA), tok), pl.BlockSpec((1, tm, 3 * d), tok),
            pl.BlockSpec((1, tm, d), tok),
            _mod_spec(mod, l, 2), _mod_spec(mod, l, 3), _mod_spec(mod, l, 4),
            pl.BlockSpec((None, None, 1, d), lambda bi, j: (l, 1, 0, 0)),
        ] + [_layer_spec(a, l) for a in stacked],
        out_specs=[pl.BlockSpec((1, tm, d), tok), pl.BlockSpec((1, tm, d + LANES), tok),
                   pl.BlockSpec((1, tm, LANES), tok)],
        out_shape=[jax.ShapeDtypeStruct((b, s, d), F32), jax.ShapeDtypeStruct((b, s, d + LANES), F32),
                   jax.ShapeDtypeStruct((b, s, LANES), F32)],
        compiler_params=_params(("parallel", "parallel")),
    )(oa, ob, oc, gates, x, mod, mod, mod, norm_gain, *stacked)


MOE_PAD_ROWS = 8


def _moe_kernel(tile_e_ref, tile_p0_ref, tile_off_ref, tile_n_ref, tok_ref,
                hx_ref, w1g_ref, w1l_ref, b1g_ref, b1l_ref, w2_ref, b2_ref,
                out_ref, xa_ref, xb_ref, ya_ref, yb_ref, *, slots, tm, tc, d):
    c = pl.program_id(0)
    i = pl.program_id(1)
    k = c * slots + i + 1

    def gather(kk, x_ref):
        p0 = tile_p0_ref[kk]
        for r in range(tm):
            tk = tok_ref[p0 + r]
            x_ref[r // 8, r % 8:r % 8 + 1, :] = hx_ref[0, pl.ds(tk, 1), :]

    def expert(x_ref, y_ref):
        xg = x_ref[...].reshape(tm, hx_ref.shape[2])
        xb = xg[:, :d].astype(BF16)
        lane = _iota((tm, LANES), 1)
        wcol = jnp.sum(jnp.where(lane == tile_e_ref[k], xg[:, d:], 0.0), axis=-1, keepdims=True)
        ug = jnp.dot(xb, w1g_ref[0], preferred_element_type=F32) + b1g_ref[0]
        ul = jnp.dot(xb, w1l_ref[0], preferred_element_type=F32) + b1l_ref[0]
        ug = jnp.minimum(ug, SWIGLU_LIMIT)
        ul = jnp.clip(ul, -SWIGLU_LIMIT, SWIGLU_LIMIT)
        act = ug * jax.nn.sigmoid(SWIGLU_ALPHA * ug) * (ul + 1.0)
        y = jnp.dot(act.astype(BF16), w2_ref[0], preferred_element_type=F32) + b2_ref[0]
        y_ref[...] = (y * wcol).reshape(y_ref.shape)

    def scatter(kk, y_ref):
        p0 = tile_p0_ref[kk]
        off = tile_off_ref[kk]
        end = off + tile_n_ref[kk]
        for r0 in range(0, tm, 4):
            toks = [jnp.where((r0 + u >= off) & (r0 + u < end), tok_ref[p0 + r0 + u], tc + u) for u in range(4)]
            rows = [out_ref[0, pl.ds(tk, 1), :] for tk in toks]
            for u in range(4):
                r = r0 + u
                out_ref[0, pl.ds(toks[u], 1), :] = rows[u] + y_ref[r // 8, r % 8:r % 8 + 1, :]

    @pl.when(i == 0)
    def _():
        out_ref[...] = jnp.zeros_like(out_ref)
        ya_ref[...] = jnp.zeros_like(ya_ref)
        yb_ref[...] = jnp.zeros_like(yb_ref)
        gather(k, xa_ref)

    busy = (tile_n_ref[k] > 0) | (tile_n_ref[k - 1] > 0)

    @pl.when(busy & (i % 2 == 0))
    def _():
        gather(k + 1, xb_ref)
        expert(xa_ref, ya_ref)
        scatter(k - 1, yb_ref)

    @pl.when(busy & (i % 2 == 1))
    def _():
        gather(k + 1, xa_ref)
        expert(xb_ref, yb_ref)
        scatter(k - 1, ya_ref)


def _moe(hx, tables, wp, e0, tc, tm):
    b, s, dx = hx.shape
    d = dx - LANES
    nc = (b * s) // tc
    tile_e = tables[0]
    slots = tile_e.shape[0] // nc
    f = wp["w1g"].shape[2]
    hx = hx.reshape(nc, tc, dx)
    chunk = lambda c, i, *prefetch: (c, 0, 0)
    expert = lambda c, i, te, *prefetch: (e0 + te[c * slots + i + 1], 0, 0)
    x_tile = pltpu.VMEM((tm // 8, 8, dx), F32)
    y_tile = pltpu.VMEM((tm // 8, 8, d), F32)
    out = pl.pallas_call(
        functools.partial(_moe_kernel, slots=slots, tm=tm, tc=tc, d=d),
        grid_spec=pltpu.PrefetchScalarGridSpec(
            num_scalar_prefetch=len(tables),
            grid=(nc, slots - 2),
            in_specs=[
                pl.BlockSpec((1, tc, dx), chunk, pipeline_mode=pl.Buffered(1)),
                pl.BlockSpec((1, d, f), expert), pl.BlockSpec((1, d, f), expert),
                pl.BlockSpec((1, 1, f), expert), pl.BlockSpec((1, 1, f), expert),
                pl.BlockSpec((1, f, d), expert), pl.BlockSpec((1, 1, d), expert),
            ],
            out_specs=pl.BlockSpec((1, tc + MOE_PAD_ROWS, d), chunk, pipeline_mode=pl.Buffered(1)),
            scratch_shapes=[x_tile, x_tile, y_tile, y_tile],
        ),
        out_shape=jax.ShapeDtypeStruct((nc, tc + MOE_PAD_ROWS, d), F32),
        compiler_params=_params(("parallel", "arbitrary")),
    )(*tables, hx, wp["w1g"], wp["w1l"], wp["b1g"], wp["b1l"], wp["w2"], wp["b2"])
    if tc == s:
        return out
    return out[:, :tc].reshape(b, s, d)


def _routing_tables(route, n_experts, tc, tm):
    b, s, _ = route.shape
    t = b * s
    nc = t // tc
    idx = route[:, :, :TOP_K].astype(jnp.int32).reshape(nc, tc, TOP_K)
    key = idx * tc + jnp.arange(tc, dtype=jnp.int32)[None, :, None]
    key = jnp.sort(key.reshape(nc, tc * TOP_K), axis=1)
    tok = key % tc
    e_sorted = key // tc
    experts = jnp.arange(n_experts, dtype=jnp.int32)
    cstart = jnp.sum(e_sorted[:, :, None] < experts[None, None, :], axis=1).astype(jnp.int32)
    cend = jnp.concatenate([cstart[:, 1:], jnp.full((nc, 1), tc * TOP_K, jnp.int32)], axis=1)
    count = cend - cstart
    tiles = (count + tm - 1) // tm
    tend = jnp.cumsum(tiles, axis=1)
    tstart = tend - tiles
    nt = -(-(tc * TOP_K) // tm) + n_experts
    slot = jnp.arange(-1, nt + 2, dtype=jnp.int32)
    total = tend[:, -1:]
    live = (slot[None, :] >= 0) & (slot[None, :] < total)
    slot_c = jnp.clip(slot[None, :], 0, total - 1)
    e_of = jnp.sum(slot_c[:, :, None] >= tend[:, None, :], axis=2).astype(jnp.int32)
    pick = lambda tbl: jnp.sum(jnp.where(e_of[:, :, None] == experts[None, None, :], tbl[:, None, :], 0), axis=2)
    jj = slot_c - pick(tstart)
    p0 = pick(cstart) + jj * tm
    n = jnp.where(live, jnp.clip(pick(count) - jj * tm, 0, tm), 0)
    p0 = p0 + (jnp.arange(nc, dtype=jnp.int32) * (tc * TOP_K))[:, None]
    first = jnp.minimum(p0, nc * tc * TOP_K - tm)
    flat = lambda a: a.reshape(-1).astype(jnp.int32)
    return flat(e_of), flat(first), flat(p0 - first), flat(n), flat(tok)


def _residual_kernel(x_ref, moe_ref, g_ref, o_ref):
    o_ref[0] = x_ref[0] + g_ref[0] * moe_ref[0]


def _residual(x, moe, mod, l, tm):
    b, s, d = x.shape
    tok = lambda bi, j: (bi, j, 0)
    return pl.pallas_call(
        _residual_kernel,
        grid=(b, s // tm),
        in_specs=[pl.BlockSpec((1, tm, d), tok), pl.BlockSpec((1, tm, d), tok), _mod_spec(mod, l, 5)],
        out_specs=pl.BlockSpec((1, tm, d), tok),
        out_shape=jax.ShapeDtypeStruct((b, s, d), F32),
        compiler_params=_params(("parallel", "parallel")),
    )(x, moe, mod)


def _split_kernel(w_ref, p_ref, g_ref, l_ref):
    half = MXU_DIM // 2
    for c in range(w_ref.shape[2] // MXU_DIM):
        w = w_ref[0, :, c * MXU_DIM:(c + 1) * MXU_DIM].astype(BF16)
        z = jnp.dot(w, p_ref[...], preferred_element_type=F32)
        g_ref[0, :, c * half:(c + 1) * half] = z[:, :half].astype(BF16)
        l_ref[0, :, c * half:(c + 1) * half] = z[:, half:].astype(BF16)


def _split_glu_weights(w):
    n, d, f2 = w.shape
    cw = min(f2, 4 * MXU_DIM)
    half = MXU_DIM // 2
    k = np.arange(half)
    perm = np.zeros((MXU_DIM, MXU_DIM), np.float32)
    perm[2 * k, k] = 1.0
    perm[2 * k + 1, half + k] = 1.0
    out = jax.ShapeDtypeStruct((n, d, f2 // 2), BF16)
    return pl.pallas_call(
        _split_kernel,
        grid=(n, f2 // cw),
        in_specs=[pl.BlockSpec((1, d, cw), lambda i, j: (i, 0, j)),
                  pl.BlockSpec((MXU_DIM, MXU_DIM), lambda i, j: (0, 0))],
        out_specs=[pl.BlockSpec((1, d, cw // 2), lambda i, j: (i, 0, j))] * 2,
        out_shape=[out, out],
        compiler_params=_params(("parallel", "parallel")),
    )(w, jnp.asarray(perm, BF16))


def _pad_heads(w, n_heads):
    lead = w.shape[:-1]
    w = w.reshape(lead + (n_heads, HEAD_DIM))
    w = jnp.concatenate([w, jnp.zeros_like(w)], axis=-1)
    return w.reshape(lead + (n_heads * LANES,))


def _prepare_weights(d, tm, slopes, w_in, b_fgate, qk_gain, w_br_fox, w_br_swa, w_br_moba, w_out,
                     w_router, b_router, w_exp1, b_exp1, w_exp2, b_exp2):
    depth = w_in.shape[0]
    n_exp = w_router.shape[2]
    splits = (W_FOX, W_FOX, W_FOX, FOX_HEADS, W_SWA, W_SWA_KV, W_SWA_KV, W_MOBA, W_MOBA, W_MOBA, 3 * d)
    pts = np.cumsum(splits)[:-1].tolist()
    qa, ka, va, fa, qb, kb, vb, qc, kc, vc, gt = jnp.split(w_in, pts, axis=2)
    fa = jnp.pad(fa, ((0, 0), (0, 0), (0, LANES - FOX_HEADS)))
    w_cat = jnp.concatenate([_pad_heads(ka, FOX_HEADS), fa, kb, _pad_heads(kc, MOBA_HEADS), gt],
                            axis=2).astype(BF16)
    w_t = jnp.swapaxes(jnp.concatenate([qa, va, qc, vc, qb, vb], axis=2), 1, 2).astype(BF16)
    b_f = jnp.pad(b_fgate.astype(F32), ((0, 0), (0, LANES - FOX_HEADS)))[:, None, :]
    g = qk_gain.astype(F32)
    tile = lambda v, n: jnp.tile(v, (1, n))
    ones = lambda n: jnp.ones((depth, n), F32)
    qscale = HEAD_DIM ** -0.5
    gvec = jnp.concatenate([
        _pad_heads(tile(g[:, 1], FOX_HEADS), FOX_HEADS), ones(LANES),
        tile(g[:, 3], SWA_KV_HEADS),
        _pad_heads(tile(g[:, 5], MOBA_HEADS), MOBA_HEADS),
    ], axis=1)[:, None, :]
    gain_t = jnp.concatenate([tile(g[:, 0], FOX_HEADS), tile(g[:, 4], MOBA_HEADS), tile(g[:, 2], SWA_HEADS)],
                             axis=1) * (qscale * LOG2E)
    gain_t = jnp.broadcast_to(gain_t[:, :, None], (depth, W_FOX + W_MOBA + W_SWA, tm))
    heads = MXU_DIM // HEAD_DIM
    gsum = jnp.asarray(np.kron(np.eye(heads), np.ones((HEAD_DIM, HEAD_DIM))), BF16)
    tril = jnp.asarray(np.tril(np.ones((tm, tm))), F32)
    perm = np.zeros((N_PIECES * LANES, W_FOX_AUG), np.float32)
    for h in range(FOX_HEADS):
        for p in range(N_PIECES):
            perm[p * LANES + h, h * LANES + FOX_CUM_LANE + p] = 1.0
    slope_lanes = jnp.repeat(slopes[SWA_HEADS:] * LOG2E, LANES)[None, :]
    swa_slope_lanes = jnp.pad(jnp.repeat(slopes[:SWA_HEADS] * LOG2E, N_PIECES),
                              (0, LANES - N_PIECES * SWA_HEADS))[None, :]
    w_r = jnp.pad(w_router.astype(F32), ((0, 0), (0, 0), (0, LANES - n_exp)))
    w_r_hi = w_r.astype(BF16)
    w_r = jnp.stack([w_r_hi, (w_r - w_r_hi.astype(F32)).astype(BF16)], axis=1)
    b_r = jnp.pad(b_router.astype(F32), ((0, 0), (0, LANES - n_exp)), constant_values=NEG)[:, None, :]
    w1g, w1l = _split_glu_weights(w_exp1.reshape((depth * n_exp,) + w_exp1.shape[2:]))
    f = w_exp2.shape[2]
    b1 = b_exp1.reshape(depth * n_exp, 1, f, 2)
    return dict(
        w_cat=w_cat, w_t=w_t, b_f=b_f, gvec=gvec, gain_t=gain_t, gsum=gsum, tril=tril,
        perm=jnp.asarray(perm, BF16), slope_lanes=slope_lanes, swa_slope_lanes=swa_slope_lanes,
        w_a=w_br_fox.astype(BF16), w_b=w_br_swa.astype(BF16), w_c=w_br_moba.astype(BF16),
        w_o=w_out.astype(BF16), w_r=w_r, b_r=b_r,
        w1g=w1g, w1l=w1l, b1g=b1[..., 0], b1l=b1[..., 1],
        w2=w_exp2.reshape(depth * n_exp, f, d).astype(BF16), b2=b_exp2.reshape(depth * n_exp, 1, d),
    )


def kernel(x, c, w_ada, b_ada, norm_gain, w_in, b_fgate, qk_gain, attn_sinks, w_br_fox, w_br_swa, w_br_moba, w_out, w_router, b_router, w_exp1, b_exp1, w_exp2, b_exp2):
    b, s, d = x.shape
    depth = w_ada.shape[0]
    n_experts = w_router.shape[2]
    tm = min(s, 512)
    tq_fox = min(s, 512)
    tq_swa = min(s, 256)
    tc = min(s, 4096)
    tm_moe = min(256, tc * TOP_K // n_experts)
    n_alibi = SWA_HEADS + MOBA_HEADS
    slopes = jnp.exp2(-8.0 * jnp.arange(1, n_alibi + 1, dtype=F32) / n_alibi)

    mod = _modulation(c, w_ada, b_ada).reshape(depth, b, 1, 6 * d)
    ng = norm_gain.astype(F32).reshape(depth, 2, 1, d)
    sinks = attn_sinks.astype(F32)
    wp = _prepare_weights(d, tm, slopes, w_in, b_fgate, qk_gain, w_br_fox, w_br_swa, w_br_moba, w_out,
                          w_router, b_router, w_exp1, b_exp1, w_exp2, b_exp2)
    moe = None
    for l in range(depth):
        outs = _in_proj(x, moe, mod, ng, l, wp, tm)
        qft, kfa, vft, qst, ksa, vst, qmt, kma, vmt, kmean, gates = outs[:11]
        if moe is not None:
            x = outs[11]
        o_a = _fox_attention(qft, kfa, vft, tq_fox)
        o_b = _swa_attention(qst, ksa, vst, sinks[l], slopes, tq_swa)
        nblk = s // MOBA_BLOCK
        km4 = kmean.reshape(b, nblk, MOBA_HEADS, LANES)[..., :HEAD_DIM]
        kmat = jnp.zeros((b, MOBA_HEADS, LANES, HEAD_DIM), F32)
        kmat = kmat.at[:, :, :nblk].set(jnp.transpose(km4, (0, 2, 1, 3)))
        o_c = _moba_attention(qmt, kma, vmt, kmat, tq_fox)
        x, hx, route = _out_proj(o_a, o_b, o_c, gates, x, mod, ng, l, wp, tm)
        tables = _routing_tables(route, n_experts, tc, tm_moe)
        moe = _moe(hx, tables, wp, l * n_experts, tc, tm_moe)
    return _residual(x, moe, mod, depth - 1, tm)
```

```python
import functools

import numpy as np
import jax
import jax.numpy as jnp
from jax import lax
from jax.experimental import pallas as pl
from jax.experimental.pallas import tpu as pltpu

HEAD_DIM = 64
FOX_HEADS = 4
SWA_HEADS = 8
SWA_KV_HEADS = 2
SWA_WINDOW = 128
MOBA_HEADS = 4
MOBA_BLOCK = 256
MOBA_TOPK = 3
TOP_K = 4
SWIGLU_ALPHA = 1.702
SWIGLU_LIMIT = 7.0
EPS = 1e-6
NEG = -1e30
LOG2E = 1.4426950408889634
MOBA_MASK_BIAS = -float(2 ** 30)

LANES = 128
SUBLANES = 8
MXU_DIM = 256
VMEM_LIMIT = 60 * 1024 * 1024

W_FOX = FOX_HEADS * HEAD_DIM
W_SWA = SWA_HEADS * HEAD_DIM
W_SWA_KV = SWA_KV_HEADS * HEAD_DIM
W_MOBA = MOBA_HEADS * HEAD_DIM
W_FOX_AUG = FOX_HEADS * LANES
FOX_CUM_LANE = HEAD_DIM
W_MOBA_AUG = MOBA_HEADS * LANES
MOBA_BLOCK_LANE = HEAD_DIM
MOBA_MAX_BLOCKS = 16
MOBA_ALIBI_LANE = MOBA_BLOCK_LANE + MOBA_MAX_BLOCKS
N_PIECES = 3
SUM_ROWS = 16

F32 = jnp.float32
BF16 = jnp.bfloat16
HIGHEST = lax.Precision.HIGHEST
NT_DIMS = (((1,), (1,)), ((), ()))


def _params(semantics):
    return pltpu.CompilerParams(dimension_semantics=semantics, vmem_limit_bytes=VMEM_LIMIT)


def _iota(shape, dim, dtype=jnp.int32):
    return lax.broadcasted_iota(dtype, shape, dim)


def _pieces(x):
    hi = x.astype(BF16).astype(F32)
    r = x - hi
    mid = r.astype(BF16).astype(F32)
    return hi, mid, r - mid


def _mod_kernel(c_ref, w_ref, b_ref, o_ref):
    c = c_ref[...]
    s = c * jax.nn.sigmoid(c)
    o_ref[0] = jnp.dot(s, w_ref[0], precision=HIGHEST, preferred_element_type=F32) + b_ref[0]


def _modulation(c, w_ada, b_ada):
    depth, d, n = w_ada.shape
    b = c.shape[0]
    tn = d
    return pl.pallas_call(
        _mod_kernel,
        grid=(depth, n // tn),
        in_specs=[
            pl.BlockSpec((b, d), lambda l, j: (0, 0)),
            pl.BlockSpec((1, d, tn), lambda l, j: (l, 0, j)),
            pl.BlockSpec((1, 1, tn), lambda l, j: (l, 0, j)),
        ],
        out_specs=pl.BlockSpec((1, b, tn), lambda l, j: (l, 0, j)),
        out_shape=jax.ShapeDtypeStruct((depth, b, n), F32),
        compiler_params=_params(("parallel", "parallel")),
    )(c, w_ada, b_ada.reshape(depth, 1, n))


def _head_norm(z, g_ref, gain):
    w = z.shape[1]
    cw = min(w, MXU_DIM)
    outs = []
    for c0 in range(0, w, cw):
        zz = z[:, c0:c0 + cw]
        ss = jnp.dot((zz * zz).astype(BF16), g_ref[:cw, :cw], preferred_element_type=F32)
        outs.append(zz * lax.rsqrt(ss * (1.0 / HEAD_DIM) + EPS))
    zn = outs[0] if len(outs) == 1 else jnp.concatenate(outs, axis=1)
    return zn * gain


def _head_norm_t(zt, gain_t):
    rows, cols = zt.shape
    z3 = zt.reshape(rows // HEAD_DIM, HEAD_DIM, cols)
    ms = jnp.mean(z3 * z3, axis=1, keepdims=True)
    return (z3 * lax.rsqrt(ms + EPS)).reshape(rows, cols) * gain_t


def _in_kernel(*refs, has_prev, tm, d):
    if has_prev:
        x_ref, moe_ref, g2_ref = refs[:3]
        refs = refs[3:]
    else:
        x_ref = refs[0]
        refs = refs[1:]
    (sh_ref, sc_ref, gain_ref, w_ref, wt_ref, bf_ref, gvec_ref, gt_ref, g_ref, tril_ref, perm_ref, slope_ref,
     sslope_ref,
     qft_ref, kfa_ref, vft_ref, qst_ref, ksa_ref, vst_ref, qmt_ref, kma_ref, vmt_ref,
     kmean_ref, gates_ref) = refs[:24]
    refs = refs[24:]
    if has_prev:
        xn_ref, carry_ref = refs
    else:
        (carry_ref,) = refs
    j = pl.program_id(1)

    x = x_ref[0]
    if has_prev:
        x = x + g2_ref[0] * moe_ref[0]
        xn_ref[0] = x
    ms = jnp.mean(x * x, axis=-1, keepdims=True)
    h = x * lax.rsqrt(ms + EPS) * gain_ref[...]
    h = h * (1.0 + sc_ref[0]) + sh_ref[0]
    hb = h.astype(BF16)

    def proj(c0, c1):
        return jnp.dot(hb, w_ref[:, c0:c1], preferred_element_type=F32)

    def gain(c0, c1):
        return gvec_ref[:, c0:c1]

    zt = lax.dot_general(wt_ref[...], hb, NT_DIMS, preferred_element_type=F32)
    qft_ref[0] = _head_norm_t(zt[:W_FOX], gt_ref[:W_FOX]).astype(BF16)
    vft_ref[0] = zt[W_FOX:2 * W_FOX].astype(BF16)
    o = 2 * W_FOX
    qmt_ref[0] = _head_norm_t(zt[o:o + W_MOBA], gt_ref[W_FOX:W_FOX + W_MOBA])
    vmt_ref[0] = zt[o + W_MOBA:o + 2 * W_MOBA].astype(BF16)
    o += 2 * W_MOBA
    qst_ref[0] = _head_norm_t(zt[o:o + W_SWA], gt_ref[W_FOX + W_MOBA:]).astype(BF16)
    vst_ref[0] = zt[o + W_SWA:o + W_SWA + W_SWA_KV].astype(BF16)

    @pl.when(j == 0)
    def _():
        carry_ref[...] = jnp.zeros_like(carry_ref)

    o = 0
    z = proj(o, o + W_FOX_AUG + LANES)
    kf = _head_norm(z[:, :W_FOX_AUG], g_ref, gain(o, o + W_FOX_AUG))
    fa = z[:, W_FOX_AUG:] + bf_ref[...]
    logf = -(jnp.maximum(-fa, 0.0) + jnp.log(1.0 + jnp.exp(-jnp.abs(fa))))
    cum = jnp.dot(tril_ref[...], logf, precision=HIGHEST, preferred_element_type=F32) + carry_ref[0:1, :]
    carry_ref[...] = jnp.broadcast_to(cum[tm - 1:tm, :], carry_ref.shape)
    parts = jnp.concatenate(_pieces(cum * (-LOG2E)), axis=1).astype(BF16)
    kfa_ref[0] = (kf + jnp.dot(parts, perm_ref[...], preferred_element_type=F32)).astype(BF16)
    o += W_FOX_AUG + LANES
    ksw = _head_norm(proj(o, o + W_SWA_KV), g_ref, gain(o, o + W_SWA_KV))
    lane = _iota((tm, LANES), 1)
    pos = (j * tm + _iota((tm, LANES), 0)).astype(F32)
    s_hi, s_mid, s_lo = _pieces(sslope_ref[...] * pos)
    piece = lane % N_PIECES
    alibi = jnp.where(piece == 0, s_hi, jnp.where(piece == 1, s_mid, s_lo))
    ksa_ref[0] = jnp.concatenate([ksw, alibi], axis=1).astype(BF16)
    o += W_SWA_KV
    z = proj(o, o + W_MOBA_AUG)
    kn = _head_norm(z, g_ref, gain(o, o + W_MOBA_AUG))
    nb = tm // MOBA_BLOCK
    kmean_ref[0] = jnp.mean(kn.reshape(nb, MOBA_BLOCK, W_MOBA_AUG), axis=1).reshape(nb, 1, W_MOBA_AUG)
    lane = _iota((tm, W_MOBA_AUG), 1) % LANES
    pos = j * tm + _iota((tm, W_MOBA_AUG), 0)
    a_hi, a_mid, a_lo = _pieces(slope_ref[...] * pos.astype(F32))
    ka = jnp.where(lane == MOBA_BLOCK_LANE + pos // MOBA_BLOCK, 1.0, kn)
    ka = jnp.where(lane == MOBA_ALIBI_LANE, a_hi, ka)
    ka = jnp.where(lane == MOBA_ALIBI_LANE + 1, a_mid, ka)
    ka = jnp.where(lane == MOBA_ALIBI_LANE + 2, a_lo, ka)
    kma_ref[0] = ka.astype(BF16)
    o += W_MOBA_AUG
    for br in range(3):
        zg = proj(o + br * d, o + (br + 1) * d)
        gates_ref[0, :, br * d:(br + 1) * d] = jax.nn.sigmoid(zg).astype(BF16)


def _layer_spec(a, l):
    return pl.BlockSpec((None,) + a.shape[1:], lambda *_: (l,) + (0,) * (a.ndim - 1))


def _const_spec(a):
    return pl.BlockSpec(a.shape, lambda *_: (0,) * a.ndim)


def _mod_spec(mod, l, k):
    d = mod.shape[3] // 6
    return pl.BlockSpec((None, 1, 1, d), lambda bi, j: (l, bi, 0, k))


def _in_proj(x, moe, mod, norm_gain, l, wp, tm):
    b, s, d = x.shape
    has_prev = moe is not None
    tok = lambda bi, j: (bi, j, 0)
    tok_t = lambda bi, j: (bi, 0, j)
    in_specs = [pl.BlockSpec((1, tm, d), tok)]
    args = [x]
    if has_prev:
        in_specs += [pl.BlockSpec((1, tm, d), tok), _mod_spec(mod, l - 1, 5)]
        args += [moe, mod]
    stacked = [wp["w_cat"], wp["w_t"], wp["b_f"], wp["gvec"], wp["gain_t"]]
    shared = [wp["gsum"], wp["tril"], wp["perm"], wp["slope_lanes"], wp["swa_slope_lanes"]]
    in_specs += [_mod_spec(mod, l, 0), _mod_spec(mod, l, 1),
                 pl.BlockSpec((None, None, 1, d), lambda bi, j: (l, 0, 0, 0))]
    in_specs += [_layer_spec(a, l) for a in stacked] + [_const_spec(a) for a in shared]
    args += [mod, mod, norm_gain] + stacked + shared
    nblk = s // MOBA_BLOCK
    out_shape = [
        jax.ShapeDtypeStruct((b, W_FOX, s), BF16), jax.ShapeDtypeStruct((b, s, W_FOX_AUG), BF16),
        jax.ShapeDtypeStruct((b, W_FOX, s), BF16),
        jax.ShapeDtypeStruct((b, W_SWA, s), BF16), jax.ShapeDtypeStruct((b, s, 2 * LANES), BF16),
        jax.ShapeDtypeStruct((b, W_SWA_KV, s), BF16),
        jax.ShapeDtypeStruct((b, W_MOBA, s), F32), jax.ShapeDtypeStruct((b, s, W_MOBA_AUG), BF16),
        jax.ShapeDtypeStruct((b, W_MOBA, s), BF16),
        jax.ShapeDtypeStruct((b, nblk, 1, W_MOBA_AUG), F32),
        jax.ShapeDtypeStruct((b, s, 3 * d), BF16),
    ]
    out_specs = [
        pl.BlockSpec((1, W_FOX, tm), tok_t), pl.BlockSpec((1, tm, W_FOX_AUG), tok),
        pl.BlockSpec((1, W_FOX, tm), tok_t),
        pl.BlockSpec((1, W_SWA, tm), tok_t), pl.BlockSpec((1, tm, 2 * LANES), tok),
        pl.BlockSpec((1, W_SWA_KV, tm), tok_t),
        pl.BlockSpec((1, W_MOBA, tm), tok_t), pl.BlockSpec((1, tm, W_MOBA_AUG), tok),
        pl.BlockSpec((1, W_MOBA, tm), tok_t),
        pl.BlockSpec((1, tm // MOBA_BLOCK, 1, W_MOBA_AUG), lambda bi, j: (bi, j, 0, 0)),
        pl.BlockSpec((1, tm, 3 * d), tok),
    ]
    if has_prev:
        out_shape.append(jax.ShapeDtypeStruct((b, s, d), F32))
        out_specs.append(pl.BlockSpec((1, tm, d), tok))
    return pl.pallas_call(
        functools.partial(_in_kernel, has_prev=has_prev, tm=tm, d=d),
        grid=(b, s // tm),
        in_specs=in_specs,
        out_specs=out_specs,
        out_shape=out_shape,
        scratch_shapes=[pltpu.VMEM((SUBLANES, LANES), F32)],
        compiler_params=_params(("parallel", "arbitrary")),
    )(*args)


def _flash_t(i, tq, n_heads, scores, values, o_ref, m_ref, acc_ref):
    m_ref[...] = jnp.full_like(m_ref, NEG)
    acc_ref[...] = jnp.zeros_like(acc_ref)
    causal = (_iota((tq, tq), 0) <= _iota((tq, tq), 1))[None]
    one_row = jnp.where(_iota((SUM_ROWS, tq), 0) == 0, 1.0, 0.0).astype(BF16)

    def block(j, diagonal):
        s = jnp.stack([scores(j, h) for h in range(n_heads)])
        if diagonal:
            s = jnp.where(causal, s, NEG)
        m_old = m_ref[...]
        m_new = jnp.maximum(m_old, jnp.max(s, axis=1, keepdims=True))
        alpha = jnp.exp2(m_old - m_new)
        pb = jnp.exp2(s - m_new).astype(BF16)
        m_ref[...] = m_new
        pv = jnp.stack([jnp.dot(jnp.concatenate([values(j, h), one_row], axis=0), pb[h],
                                preferred_element_type=F32) for h in range(n_heads)])
        acc_ref[...] = acc_ref[...] * alpha + pv

    def body(j, carry):
        block(j, False)
        return carry

    lax.fori_loop(0, i, body, 0)
    block(i, True)
    acc = acc_ref[...]
    out = acc[:, :HEAD_DIM] * (1.0 / acc[:, HEAD_DIM:HEAD_DIM + 1])
    o_ref[0] = jnp.transpose(out.reshape(n_heads * HEAD_DIM, tq)).astype(BF16)


def _fox_kernel(qt_ref, k_ref, vt_ref, o_ref, m_ref, acc_ref, *, tq):
    i = pl.program_id(1)
    tail = jnp.where(_iota((LANES - HEAD_DIM, tq), 0) < N_PIECES, 1.0, 0.0).astype(BF16)
    qa = [jnp.concatenate([qt_ref[0, h * HEAD_DIM:(h + 1) * HEAD_DIM, :], tail], axis=0)
          for h in range(FOX_HEADS)]

    def scores(j, h):
        start = pl.multiple_of(j * tq, tq)
        return jnp.dot(k_ref[0, pl.ds(start, tq), h * LANES:(h + 1) * LANES], qa[h],
                       preferred_element_type=F32)

    def values(j, h):
        start = pl.multiple_of(j * tq, tq)
        return vt_ref[0, h * HEAD_DIM:(h + 1) * HEAD_DIM, pl.ds(start, tq)]

    _flash_t(i, tq, FOX_HEADS, scores, values, o_ref, m_ref, acc_ref)


def _fox_attention(qt, k_aug, vt, tq):
    b, w, s = qt.shape
    return pl.pallas_call(
        functools.partial(_fox_kernel, tq=tq),
        grid=(b, s // tq),
        in_specs=[
            pl.BlockSpec((1, w, tq), lambda bi, i: (bi, 0, i)),
            pl.BlockSpec((1, s, k_aug.shape[2]), lambda bi, i: (bi, 0, 0)),
            pl.BlockSpec((1, w, s), lambda bi, i: (bi, 0, 0)),
        ],
        out_specs=pl.BlockSpec((1, tq, w), lambda bi, i: (bi, i, 0)),
        out_shape=jax.ShapeDtypeStruct((b, s, w), BF16),
        scratch_shapes=[pltpu.VMEM((FOX_HEADS, 1, tq), F32),
                        pltpu.VMEM((FOX_HEADS, HEAD_DIM + SUM_ROWS, tq), F32)],
        compiler_params=_params(("parallel", "parallel")),
    )(qt, k_aug, vt)


def _moba_kernel(qt_ref, k_ref, vt_ref, kmean_ref, o_ref, m_ref, acc_ref, *, tq):
    i = pl.program_id(1)
    row = _iota((LANES, tq), 0)
    rowf = row.astype(F32)
    own = (i * tq + _iota((LANES, tq), 1)) // MOBA_BLOCK
    past = row < own
    tail = jnp.where(_iota((LANES - MOBA_ALIBI_LANE, tq), 0) < N_PIECES, 1.0, 0.0)
    qa = []
    for h in range(MOBA_HEADS):
        qh = qt_ref[0, h * HEAD_DIM:(h + 1) * HEAD_DIM, :]
        gate = jnp.dot(kmean_ref[0, h], qh, precision=HIGHEST, preferred_element_type=F32)
        g = jnp.where(past, gate, NEG)
        chosen = jnp.zeros((LANES, tq), jnp.bool_)
        for _ in range(MOBA_TOPK):
            mx = jnp.max(g, axis=0, keepdims=True)
            first = jnp.min(jnp.where(g == mx, rowf, float(LANES)), axis=0, keepdims=True)
            pick = (rowf == first) & (mx > 0.5 * NEG)
            chosen = chosen | pick
            g = jnp.where(pick, NEG, g)
        bias = jnp.where(past & jnp.logical_not(chosen), MOBA_MASK_BIAS, 0.0)
        qa.append(jnp.concatenate([qh, bias[:MOBA_MAX_BLOCKS], tail], axis=0).astype(BF16))

    def scores(j, h):
        start = pl.multiple_of(j * tq, tq)
        return jnp.dot(k_ref[0, pl.ds(start, tq), h * LANES:(h + 1) * LANES], qa[h],
                       preferred_element_type=F32)

    def values(j, h):
        start = pl.multiple_of(j * tq, tq)
        return vt_ref[0, h * HEAD_DIM:(h + 1) * HEAD_DIM, pl.ds(start, tq)]

    _flash_t(i, tq, MOBA_HEADS, scores, values, o_ref, m_ref, acc_ref)


def _moba_attention(qt, k_aug, vt, kmean_mat, tq):
    b, w, s = qt.shape
    assert s // MOBA_BLOCK <= MOBA_MAX_BLOCKS and tq % MOBA_BLOCK == 0
    return pl.pallas_call(
        functools.partial(_moba_kernel, tq=tq),
        grid=(b, s // tq),
        in_specs=[
            pl.BlockSpec((1, w, tq), lambda bi, i: (bi, 0, i)),
            pl.BlockSpec((1, s, k_aug.shape[2]), lambda bi, i: (bi, 0, 0)),
            pl.BlockSpec((1, w, s), lambda bi, i: (bi, 0, 0)),
            pl.BlockSpec((1, MOBA_HEADS, LANES, HEAD_DIM), lambda bi, i: (bi, 0, 0, 0)),
        ],
        out_specs=pl.BlockSpec((1, tq, w), lambda bi, i: (bi, i, 0)),
        out_shape=jax.ShapeDtypeStruct((b, s, w), BF16),
        scratch_shapes=[pltpu.VMEM((MOBA_HEADS, 1, tq), F32),
                        pltpu.VMEM((MOBA_HEADS, HEAD_DIM + SUM_ROWS, tq), F32)],
        compiler_params=_params(("parallel", "parallel")),
    )(qt, k_aug, vt, kmean_mat)


def _swa_kernel(sinks_ref, slopes_ref, qt_ref, k_ref, vt_ref, o_ref, *, tq):
    i = pl.program_id(1)
    tk = tq + SWA_WINDOW
    start = pl.multiple_of(jnp.maximum(i * tq - SWA_WINDOW, 0), SWA_WINDOW)
    kw = k_ref[0, pl.ds(start, tk), :]
    dist = (i * tq + _iota((tk, tq), 1)) - (start + _iota((tk, tq), 0))
    valid = (dist >= 0) & (dist < SWA_WINDOW)
    qpos = (i * tq + _iota((1, tq), 1)).astype(F32)
    group = SWA_HEADS // SWA_KV_HEADS
    aug_row = _iota((LANES, tq), 0)
    blank = jnp.zeros((HEAD_DIM, tq), BF16)
    scores, sinks = [], []
    for h in range(SWA_HEADS):
        qh = qt_ref[0, h * HEAD_DIM:(h + 1) * HEAD_DIM, :]
        ones = jnp.where((aug_row >= N_PIECES * h) & (aug_row < N_PIECES * (h + 1)), 1.0, 0.0).astype(BF16)
        qa = jnp.concatenate([qh, blank, ones] if h < group else [blank, qh, ones], axis=0)
        scores.append(jnp.dot(kw, qa, preferred_element_type=F32))
        sinks.append((sinks_ref[h] + slopes_ref[h] * qpos) * LOG2E)
    s = jnp.where(valid[None], jnp.stack(scores), NEG)
    sink = jnp.stack(sinks)
    m = jnp.maximum(jnp.max(s, axis=1, keepdims=True), sink)
    pb = jnp.exp2(s - m).astype(BF16)
    one_row = jnp.where(_iota((SUM_ROWS, tk), 0) == 0, 1.0, 0.0).astype(BF16)
    outs = [jnp.dot(jnp.concatenate([vt_ref[0, (h // group) * HEAD_DIM:(h // group + 1) * HEAD_DIM,
                                            pl.ds(start, tk)], one_row], axis=0), pb[h],
                    preferred_element_type=F32) for h in range(SWA_HEADS)]
    res = jnp.stack(outs)
    out = res[:, :HEAD_DIM] / (res[:, HEAD_DIM:HEAD_DIM + 1] + jnp.exp2(sink - m))
    o_ref[0] = jnp.transpose(out.reshape(SWA_HEADS * HEAD_DIM, tq)).astype(BF16)


def _swa_attention(qt, k_aug, vt, sinks, slopes, tq):
    b, w, s = qt.shape
    smem = pl.BlockSpec(memory_space=pltpu.SMEM)
    return pl.pallas_call(
        functools.partial(_swa_kernel, tq=tq),
        grid=(b, s // tq),
        in_specs=[
            smem, smem,
            pl.BlockSpec((1, w, tq), lambda bi, i: (bi, 0, i)),
            pl.BlockSpec((1, s, k_aug.shape[2]), lambda bi, i: (bi, 0, 0)),
            pl.BlockSpec((1, vt.shape[1], s), lambda bi, i: (bi, 0, 0)),
        ],
        out_specs=pl.BlockSpec((1, tq, w), lambda bi, i: (bi, i, 0)),
        out_shape=jax.ShapeDtypeStruct((b, s, w), BF16),
        compiler_params=_params(("parallel", "parallel")),
    )(sinks, slopes, qt, k_aug, vt)


def _out_kernel(oa_ref, ob_ref, oc_ref, gates_ref, x_ref, g1_ref, sh_ref, sc_ref, gain_ref,
                wa_ref, wb_ref, wc_ref, wo_ref, wr_ref, br_ref,
                x1_ref, hx_ref, route_ref, *, d, parts):
    tm = x_ref.shape[1] // parts
    for part in range(parts):
        _out_rows(slice(part * tm, (part + 1) * tm), tm, d, oa_ref, ob_ref, oc_ref, gates_ref, x_ref, g1_ref,
                  sh_ref, sc_ref, gain_ref, wa_ref, wb_ref, wc_ref, wo_ref, wr_ref, br_ref,
                  x1_ref, hx_ref, route_ref)


def _out_rows(rows, tm, d, oa_ref, ob_ref, oc_ref, gates_ref, x_ref, g1_ref, sh_ref, sc_ref, gain_ref,
              wa_ref, wb_ref, wc_ref, wo_ref, wr_ref, br_ref, x1_ref, hx_ref, route_ref):
    ya = jnp.dot(oa_ref[0, rows, :], wa_ref[...], preferred_element_type=F32)
    yb = jnp.dot(ob_ref[0, rows, :], wb_ref[...], preferred_element_type=F32)
    yc = jnp.dot(oc_ref[0, rows, :], wc_ref[...], preferred_element_type=F32)
    g = gates_ref[0, rows, :]
    mix = (g[:, :d].astype(F32) * ya + g[:, d:2 * d].astype(F32) * yb + g[:, 2 * d:].astype(F32) * yc)
    y = jnp.dot(mix.astype(BF16), wo_ref[...], preferred_element_type=F32)
    x1 = x_ref[0, rows, :] + g1_ref[0] * y
    x1_ref[0, rows, :] = x1
    ms = jnp.mean(x1 * x1, axis=-1, keepdims=True)
    h = x1 * lax.rsqrt(ms + EPS) * gain_ref[...]
    h = h * (1.0 + sc_ref[0]) + sh_ref[0]
    hx_ref[0, rows, :d] = h
    h_hi = h.astype(BF16)
    h_lo = (h - h_hi.astype(F32)).astype(BF16)
    logits = (jnp.dot(h_hi, wr_ref[0], preferred_element_type=F32)
              + jnp.dot(h_lo, wr_ref[0], preferred_element_type=F32)
              + jnp.dot(h_hi, wr_ref[1], preferred_element_type=F32)) + br_ref[...]
    lanef = _iota((tm, LANES), 1).astype(F32)
    vals, idxs = [], []
    for _ in range(TOP_K):
        mx = jnp.max(logits, axis=-1, keepdims=True)
        first = jnp.min(jnp.where(logits == mx, lanef, float(LANES)), axis=-1, keepdims=True)
        vals.append(mx)
        idxs.append(first)
        logits = jnp.where(lanef == first, -jnp.inf, logits)
    exps = [jnp.exp(v - vals[0]) for v in vals]
    den = exps[0]
    for e in exps[1:]:
        den = den + e
    gates = jnp.zeros((tm, LANES), F32)
    info = jnp.zeros((tm, LANES), F32)
    for k in range(TOP_K):
        wk = exps[k] / den
        gates = jnp.where(lanef == idxs[k], wk, gates)
        info = jnp.where(lanef == float(k), idxs[k], info)
    hx_ref[0, rows, d:] = gates
    route_ref[0, rows, :] = info


def _out_proj(oa, ob, oc, gates, x, mod, norm_gain, l, wp, tm):
    b, s, d = x.shape
    tok = lambda bi, j: (bi, j, 0)
    stacked = [wp["w_a"], wp["w_b"], wp["w_c"], wp["w_o"], wp["w_r"], wp["b_r"]]
    return pl.pallas_call(
        functools.partial(_out_kernel, d=d, parts=4),
        grid=(b, s // tm),
        in_specs=[
            pl.BlockSpec((1, tm, W_FOX), tok), pl.BlockSpec((1, tm, W_SWA), tok),
            pl.BlockSpec((1, tm, W_MOBA), tok), pl.BlockSpec((1, tm, 3 * d), tok),
            pl.BlockSpec((1, tm, d), tok),
            _mod_spec(mod, l, 2), _mod_spec(mod, l, 3), _mod_spec(mod, l, 4),
            pl.BlockSpec((None, None, 1, d), lambda bi, j: (l, 1, 0, 0)),
        ] + [_layer_spec(a, l) for a in stacked],
        out_specs=[pl.BlockSpec((1, tm, d), tok), pl.BlockSpec((1, tm, d + LANES), tok),
                   pl.BlockSpec((1, tm, LANES), tok)],
        out_shape=[jax.ShapeDtypeStruct((b, s, d), F32), jax.ShapeDtypeStruct((b, s, d + LANES), F32),
                   jax.ShapeDtypeStruct((b, s, LANES), F32)],
        compiler_params=_params(("parallel", "parallel")),
    )(oa, ob, oc, gates, x, mod, mod, mod, norm_gain, *stacked)


MOE_PAD_ROWS = SUBLANES


def _moe_kernel(tile_e_ref, tile_p0_ref, tile_off_ref, tile_n_ref, tok_ref,
                hx_ref, w1g_ref, w1l_ref, b1g_ref, b1l_ref, w2_ref, b2_ref,
                out_ref, xa_ref, xb_ref, ya_ref, yb_ref, *, slots, tm, tc, d):
    c = pl.program_id(0)
    i = pl.program_id(1)
    k = c * slots + i + 1

    def gather(kk, x_ref):
        p0 = tile_p0_ref[kk]
        for r in range(tm):
            tk = tok_ref[p0 + r]
            x_ref[r // SUBLANES, r % SUBLANES:r % SUBLANES + 1, :] = hx_ref[0, pl.ds(tk, 1), :]

    def expert(x_ref, y_ref):
        xg = x_ref[...].reshape(tm, hx_ref.shape[2])
        xb = xg[:, :d].astype(BF16)
        lane = _iota((tm, LANES), 1)
        wcol = jnp.sum(jnp.where(lane == tile_e_ref[k], xg[:, d:], 0.0), axis=-1, keepdims=True)
        ug = jnp.dot(xb, w1g_ref[0], preferred_element_type=F32) + b1g_ref[0]
        ul = jnp.dot(xb, w1l_ref[0], preferred_element_type=F32) + b1l_ref[0]
        ug = jnp.minimum(ug, SWIGLU_LIMIT)
        ul = jnp.clip(ul, -SWIGLU_LIMIT, SWIGLU_LIMIT)
        act = ug * jax.nn.sigmoid(SWIGLU_ALPHA * ug) * (ul + 1.0)
        y = jnp.dot(act.astype(BF16), w2_ref[0], preferred_element_type=F32) + b2_ref[0]
        y_ref[...] = (y * wcol).reshape(y_ref.shape)

    def scatter(kk, y_ref):
        p0 = tile_p0_ref[kk]
        off = tile_off_ref[kk]
        end = off + tile_n_ref[kk]
        for r0 in range(0, tm, 4):
            toks = [jnp.where((r0 + u >= off) & (r0 + u < end), tok_ref[p0 + r0 + u], tc + u) for u in range(4)]
            rows = [out_ref[0, pl.ds(tk, 1), :] for tk in toks]
            for u in range(4):
                r = r0 + u
                out_ref[0, pl.ds(toks[u], 1), :] = (
                    rows[u] + y_ref[r // SUBLANES, r % SUBLANES:r % SUBLANES + 1, :])

    @pl.when(i == 0)
    def _():
        out_ref[...] = jnp.zeros_like(out_ref)
        ya_ref[...] = jnp.zeros_like(ya_ref)
        yb_ref[...] = jnp.zeros_like(yb_ref)
        gather(k, xa_ref)

    busy = (tile_n_ref[k] > 0) | (tile_n_ref[k - 1] > 0)

    @pl.when(busy & (i % 2 == 0))
    def _():
        gather(k + 1, xb_ref)
        expert(xa_ref, ya_ref)
        scatter(k - 1, yb_ref)

    @pl.when(busy & (i % 2 == 1))
    def _():
        gather(k + 1, xa_ref)
        expert(xb_ref, yb_ref)
        scatter(k - 1, ya_ref)


def _moe(hx, tables, wp, e0, tc, tm):
    b, s, dx = hx.shape
    d = dx - LANES
    nc = (b * s) // tc
    tile_e = tables[0]
    slots = tile_e.shape[0] // nc
    f = wp["w1g"].shape[2]
    hx = hx.reshape(nc, tc, dx)
    chunk = lambda c, i, *prefetch: (c, 0, 0)
    expert = lambda c, i, te, *prefetch: (e0 + te[c * slots + i + 1], 0, 0)
    x_tile = pltpu.VMEM((tm // SUBLANES, SUBLANES, dx), F32)
    y_tile = pltpu.VMEM((tm // SUBLANES, SUBLANES, d), F32)
    out = pl.pallas_call(
        functools.partial(_moe_kernel, slots=slots, tm=tm, tc=tc, d=d),
        grid_spec=pltpu.PrefetchScalarGridSpec(
            num_scalar_prefetch=len(tables),
            grid=(nc, slots - 2),
            in_specs=[
                pl.BlockSpec((1, tc, dx), chunk, pipeline_mode=pl.Buffered(1)),
                pl.BlockSpec((1, d, f), expert), pl.BlockSpec((1, d, f), expert),
                pl.BlockSpec((1, 1, f), expert), pl.BlockSpec((1, 1, f), expert),
                pl.BlockSpec((1, f, d), expert), pl.BlockSpec((1, 1, d), expert),
            ],
            out_specs=pl.BlockSpec((1, tc + MOE_PAD_ROWS, d), chunk, pipeline_mode=pl.Buffered(1)),
            scratch_shapes=[x_tile, x_tile, y_tile, y_tile],
        ),
        out_shape=jax.ShapeDtypeStruct((nc, tc + MOE_PAD_ROWS, d), F32),
        compiler_params=_params(("parallel", "arbitrary")),
    )(*tables, hx, wp["w1g"], wp["w1l"], wp["b1g"], wp["b1l"], wp["w2"], wp["b2"])
    if tc == s:
        return out
    return out[:, :tc].reshape(b, s, d)


def _routing_tables(route, n_experts, tc, tm):
    b, s, _ = route.shape
    t = b * s
    nc = t // tc
    idx = route[:, :, :TOP_K].astype(jnp.int32).reshape(nc, tc, TOP_K)
    key = idx * tc + jnp.arange(tc, dtype=jnp.int32)[None, :, None]
    key = jnp.sort(key.reshape(nc, tc * TOP_K), axis=1)
    tok = key % tc
    e_sorted = key // tc
    experts = jnp.arange(n_experts, dtype=jnp.int32)
    cstart = jnp.sum(e_sorted[:, :, None] < experts[None, None, :], axis=1).astype(jnp.int32)
    cend = jnp.concatenate([cstart[:, 1:], jnp.full((nc, 1), tc * TOP_K, jnp.int32)], axis=1)
    count = cend - cstart
    tiles = (count + tm - 1) // tm
    tend = jnp.cumsum(tiles, axis=1)
    tstart = tend - tiles
    nt = -(-(tc * TOP_K) // tm) + n_experts
    slot = jnp.arange(-1, nt + 2, dtype=jnp.int32)
    total = tend[:, -1:]
    live = (slot[None, :] >= 0) & (slot[None, :] < total)
    slot_c = jnp.clip(slot[None, :], 0, total - 1)
    e_of = jnp.sum(slot_c[:, :, None] >= tend[:, None, :], axis=2).astype(jnp.int32)
    pick = lambda tbl: jnp.sum(jnp.where(e_of[:, :, None] == experts[None, None, :], tbl[:, None, :], 0), axis=2)
    jj = slot_c - pick(tstart)
    p0 = pick(cstart) + jj * tm
    n = jnp.where(live, jnp.clip(pick(count) - jj * tm, 0, tm), 0)
    p0 = p0 + (jnp.arange(nc, dtype=jnp.int32) * (tc * TOP_K))[:, None]
    first = jnp.minimum(p0, nc * tc * TOP_K - tm)
    flat = lambda a: a.reshape(-1).astype(jnp.int32)
    return flat(e_of), flat(first), flat(p0 - first), flat(n), flat(tok)


def _residual_kernel(x_ref, moe_ref, g_ref, o_ref):
    o_ref[0] = x_ref[0] + g_ref[0] * moe_ref[0]


def _residual(x, moe, mod, l, tm):
    b, s, d = x.shape
    tok = lambda bi, j: (bi, j, 0)
    return pl.pallas_call(
        _residual_kernel,
        grid=(b, s // tm),
        in_specs=[pl.BlockSpec((1, tm, d), tok), pl.BlockSpec((1, tm, d), tok), _mod_spec(mod, l, 5)],
        out_specs=pl.BlockSpec((1, tm, d), tok),
        out_shape=jax.ShapeDtypeStruct((b, s, d), F32),
        compiler_params=_params(("parallel", "parallel")),
    )(x, moe, mod)


def _split_kernel(w_ref, p_ref, g_ref, l_ref):
    half = MXU_DIM // 2
    for c in range(w_ref.shape[2] // MXU_DIM):
        w = w_ref[0, :, c * MXU_DIM:(c + 1) * MXU_DIM].astype(BF16)
        z = jnp.dot(w, p_ref[...], preferred_element_type=F32)
        g_ref[0, :, c * half:(c + 1) * half] = z[:, :half].astype(BF16)
        l_ref[0, :, c * half:(c + 1) * half] = z[:, half:].astype(BF16)


def _split_glu_weights(w):
    n, d, f2 = w.shape
    cw = min(f2, 4 * MXU_DIM)
    half = MXU_DIM // 2
    k = np.arange(half)
    perm = np.zeros((MXU_DIM, MXU_DIM), np.float32)
    perm[2 * k, k] = 1.0
    perm[2 * k + 1, half + k] = 1.0
    out = jax.ShapeDtypeStruct((n, d, f2 // 2), BF16)
    return pl.pallas_call(
        _split_kernel,
        grid=(n, f2 // cw),
        in_specs=[pl.BlockSpec((1, d, cw), lambda i, j: (i, 0, j)),
                  pl.BlockSpec((MXU_DIM, MXU_DIM), lambda i, j: (0, 0))],
        out_specs=[pl.BlockSpec((1, d, cw // 2), lambda i, j: (i, 0, j))] * 2,
        out_shape=[out, out],
        compiler_params=_params(("parallel", "parallel")),
    )(w, jnp.asarray(perm, BF16))


def _pad_heads(w, n_heads):
    lead = w.shape[:-1]
    w = w.reshape(lead + (n_heads, HEAD_DIM))
    w = jnp.concatenate([w, jnp.zeros_like(w)], axis=-1)
    return w.reshape(lead + (n_heads * LANES,))


def _prepare_weights(d, tm, slopes, w_in, b_fgate, qk_gain, w_br_fox, w_br_swa, w_br_moba, w_out,
                     w_router, b_router, w_exp1, b_exp1, w_exp2, b_exp2):
    depth = w_in.shape[0]
    n_exp = w_router.shape[2]
    splits = (W_FOX, W_FOX, W_FOX, FOX_HEADS, W_SWA, W_SWA_KV, W_SWA_KV, W_MOBA, W_MOBA, W_MOBA, 3 * d)
    pts = np.cumsum(splits)[:-1].tolist()
    qa, ka, va, fa, qb, kb, vb, qc, kc, vc, gt = jnp.split(w_in, pts, axis=2)
    fa = jnp.pad(fa, ((0, 0), (0, 0), (0, LANES - FOX_HEADS)))
    w_cat = jnp.concatenate([_pad_heads(ka, FOX_HEADS), fa, kb, _pad_heads(kc, MOBA_HEADS), gt],
                            axis=2).astype(BF16)
    w_t = jnp.swapaxes(jnp.concatenate([qa, va, qc, vc, qb, vb], axis=2), 1, 2).astype(BF16)
    b_f = jnp.pad(b_fgate.astype(F32), ((0, 0), (0, LANES - FOX_HEADS)))[:, None, :]
    g = qk_gain.astype(F32)
    tile = lambda v, n: jnp.tile(v, (1, n))
    ones = lambda n: jnp.ones((depth, n), F32)
    qscale = HEAD_DIM ** -0.5
    gvec = jnp.concatenate([
        _pad_heads(tile(g[:, 1], FOX_HEADS), FOX_HEADS), ones(LANES),
        tile(g[:, 3], SWA_KV_HEADS),
        _pad_heads(tile(g[:, 5], MOBA_HEADS), MOBA_HEADS),
    ], axis=1)[:, None, :]
    gain_t = jnp.concatenate([tile(g[:, 0], FOX_HEADS), tile(g[:, 4], MOBA_HEADS), tile(g[:, 2], SWA_HEADS)],
                             axis=1) * (qscale * LOG2E)
    gain_t = jnp.broadcast_to(gain_t[:, :, None], (depth, W_FOX + W_MOBA + W_SWA, tm))
    heads = MXU_DIM // HEAD_DIM
    gsum = jnp.asarray(np.kron(np.eye(heads), np.ones((HEAD_DIM, HEAD_DIM))), BF16)
    tril = jnp.asarray(np.tril(np.ones((tm, tm))), F32)
    perm = np.zeros((N_PIECES * LANES, W_FOX_AUG), np.float32)
    for h in range(FOX_HEADS):
        for p in range(N_PIECES):
            perm[p * LANES + h, h * LANES + FOX_CUM_LANE + p] = 1.0
    slope_lanes = jnp.repeat(slopes[SWA_HEADS:] * LOG2E, LANES)[None, :]
    swa_slope_lanes = jnp.pad(jnp.repeat(slopes[:SWA_HEADS] * LOG2E, N_PIECES),
                              (0, LANES - N_PIECES * SWA_HEADS))[None, :]
    w_r = jnp.pad(w_router.astype(F32), ((0, 0), (0, 0), (0, LANES - n_exp)))
    w_r_hi = w_r.astype(BF16)
    w_r = jnp.stack([w_r_hi, (w_r - w_r_hi.astype(F32)).astype(BF16)], axis=1)
    b_r = jnp.pad(b_router.astype(F32), ((0, 0), (0, LANES - n_exp)), constant_values=NEG)[:, None, :]
    w1g, w1l = _split_glu_weights(w_exp1.reshape((depth * n_exp,) + w_exp1.shape[2:]))
    f = w_exp2.shape[2]
    b1 = b_exp1.reshape(depth * n_exp, 1, f, 2)
    return dict(
        w_cat=w_cat, w_t=w_t, b_f=b_f, gvec=gvec, gain_t=gain_t, gsum=gsum, tril=tril,
        perm=jnp.asarray(perm, BF16), slope_lanes=slope_lanes, swa_slope_lanes=swa_slope_lanes,
        w_a=w_br_fox.astype(BF16), w_b=w_br_swa.astype(BF16), w_c=w_br_moba.astype(BF16),
        w_o=w_out.astype(BF16), w_r=w_r, b_r=b_r,
        w1g=w1g, w1l=w1l, b1g=b1[..., 0], b1l=b1[..., 1],
        w2=w_exp2.reshape(depth * n_exp, f, d).astype(BF16), b2=b_exp2.reshape(depth * n_exp, 1, d),
    )


def kernel(x, c, w_ada, b_ada, norm_gain, w_in, b_fgate, qk_gain, attn_sinks, w_br_fox, w_br_swa, w_br_moba, w_out, w_router, b_router, w_exp1, b_exp1, w_exp2, b_exp2):
    b, s, d = x.shape
    depth = w_ada.shape[0]
    n_experts = w_router.shape[2]
    tm = min(s, 512)
    tq_fox = min(s, 512)
    tq_swa = min(s, 256)
    tc = min(s, 4096)
    tm_moe = min(256, tc * TOP_K // n_experts)
    n_alibi = SWA_HEADS + MOBA_HEADS
    slopes = jnp.exp2(-8.0 * jnp.arange(1, n_alibi + 1, dtype=F32) / n_alibi)

    mod = _modulation(c, w_ada, b_ada).reshape(depth, b, 1, 6 * d)
    ng = norm_gain.astype(F32).reshape(depth, 2, 1, d)
    sinks = attn_sinks.astype(F32)
    wp = _prepare_weights(d, tm, slopes, w_in, b_fgate, qk_gain, w_br_fox, w_br_swa, w_br_moba, w_out,
                          w_router, b_router, w_exp1, b_exp1, w_exp2, b_exp2)
    moe = None
    for l in range(depth):
        outs = _in_proj(x, moe, mod, ng, l, wp, tm)
        qft, kfa, vft, qst, ksa, vst, qmt, kma, vmt, kmean, gates = outs[:11]
        if moe is not None:
            x = outs[11]
        o_a = _fox_attention(qft, kfa, vft, tq_fox)
        o_b = _swa_attention(qst, ksa, vst, sinks[l], slopes, tq_swa)
        nblk = s // MOBA_BLOCK
        km4 = kmean.reshape(b, nblk, MOBA_HEADS, LANES)[..., :HEAD_DIM]
        kmat = jnp.zeros((b, MOBA_HEADS, LANES, HEAD_DIM), F32)
        kmat = kmat.at[:, :, :nblk].set(jnp.transpose(km4, (0, 2, 1, 3)))
        o_c = _moba_attention(qmt, kma, vmt, kmat, tq_fox)
        x, hx, route = _out_proj(o_a, o_b, o_c, gates, x, mod, ng, l, wp, tm)
        tables = _routing_tables(route, n_experts, tc, tm_moe)
        moe = _moe(hx, tables, wp, l * n_experts, tc, tm_moe)
    return _residual(x, moe, mod, depth - 1, tm)
```
